```python
import math
import jax, jax.numpy as jnp
from jax import lax
import numpy as np

D_MODEL = 1024
BATCH = 8
SEQ = 2048
DEPTH = 1

MEM_LEN = 256
MLA_HEADS = 8
QK_NOPE_DIM = 64
QK_ROPE_DIM = 32
V_HEAD_DIM = 64
Q_LORA_RANK = 256
KV_LORA_RANK = 128
ROPE_THETA = 10000.0
Q_BLOCK = 128
CONV_WIDTH = 512
CONV_KSIZE = 3
XATTN_HEADS = 4
XATTN_HEAD_DIM = 128
N_GROUPS = 8
EXPERTS_PER_GROUP = 8
N_EXPERTS = N_GROUPS * EXPERTS_PER_GROUP
TOP_K_IN_GROUP = 2
EXPERT_FF = 256
DISPATCH_BLOCK = 128
N_BRANCHES = 2
EPS = 1e-6

COL_CQ = Q_LORA_RANK
COL_CKV = COL_CQ + KV_LORA_RANK
COL_KROPE = COL_CKV + QK_ROPE_DIM
COL_CONV = COL_KROPE + 3 * CONV_WIDTH
IN_COLS = COL_CONV + N_BRANCHES * D_MODEL

kernel_name = "hybrid_mla_shortconv_gated_hmoe_block"


def rmsnorm(x, g):
    xf = x.astype(jnp.float32)
    y = xf * lax.rsqrt(jnp.mean(xf * xf, axis=-1, keepdims=True) + EPS)
    return (y * g.astype(jnp.float32)).astype(x.dtype)


def rope_tables(positions):
    inv_freq = ROPE_THETA ** (-jnp.arange(0, QK_ROPE_DIM, 2, dtype=jnp.float32) / QK_ROPE_DIM)
    ang = positions.astype(jnp.float32)[..., None] * inv_freq
    return jnp.cos(ang), jnp.sin(ang)


def apply_rope(x, cos, sin):
    half = x.shape[-1] // 2
    x1 = x[..., :half].astype(jnp.float32)
    x2 = x[..., half:].astype(jnp.float32)
    return jnp.concatenate([x1 * cos - x2 * sin, x1 * sin + x2 * cos], axis=-1).astype(x.dtype)


def mla_attention(c_q, c_kv, k_rope_raw, cos, sin, g_cq, w_uq, g_ckv, w_ukv):
    B, S, _ = c_q.shape
    q = (rmsnorm(c_q, g_cq) @ w_uq).reshape(B, S, MLA_HEADS, QK_NOPE_DIM + QK_ROPE_DIM)
    q_nope = q[..., :QK_NOPE_DIM]
    q_rope = apply_rope(q[..., QK_NOPE_DIM:], cos[:, :, None, :], sin[:, :, None, :])
    kv = (rmsnorm(c_kv, g_ckv) @ w_ukv).reshape(B, S, MLA_HEADS, QK_NOPE_DIM + V_HEAD_DIM)
    k_nope, v = kv[..., :QK_NOPE_DIM], kv[..., QK_NOPE_DIM:]
    k_rope = apply_rope(k_rope_raw, cos, sin)
    scale = 1.0 / math.sqrt(QK_NOPE_DIM + QK_ROPE_DIM)
    outs = []
    for i in range(S // Q_BLOCK):
        q0, q1 = i * Q_BLOCK, (i + 1) * Q_BLOCK
        s = (jnp.einsum("bqhd,bkhd->bhqk", q_nope[:, q0:q1], k_nope[:, :q1])
             + jnp.einsum("bqhr,bkr->bhqk", q_rope[:, q0:q1], k_rope[:, :q1]))
        s = s.astype(jnp.float32) * scale
        causal = jnp.arange(q1)[None, :] <= (q0 + jnp.arange(Q_BLOCK))[:, None]
        p = jax.nn.softmax(jnp.where(causal, s, -jnp.inf), axis=-1).astype(v.dtype)
        outs.append(jnp.einsum("bhqk,bkhd->bqhd", p, v[:, :q1]))
    return jnp.concatenate(outs, axis=1).reshape(B, S, MLA_HEADS * V_HEAD_DIM)


def gated_short_conv(bcx, conv_w):
    b_gate, c_gate, xv = jnp.split(bcx, 3, axis=-1)
    u = c_gate * xv
    S = u.shape[1]
    u_pad = jnp.pad(u, ((0, 0), (CONV_KSIZE - 1, 0), (0, 0)))
    y = conv_w[0] * u_pad[:, 0:S]
    for k in range(1, CONV_KSIZE):
        y = y + conv_w[k] * u_pad[:, k:k + S]
    return b_gate * y


def memory_cross_attention(h, mem_n, w_q, w_kv, w_o):
    B, S, _ = h.shape
    M = mem_n.shape[1]
    q = (h @ w_q).reshape(B, S, XATTN_HEADS, XATTN_HEAD_DIM)
    kv = (mem_n @ w_kv).reshape(B, M, 2, XATTN_HEADS, XATTN_HEAD_DIM)
    k, v = kv[:, :, 0], kv[:, :, 1]
    s = jnp.einsum("bqhd,bmhd->bhqm", q, k).astype(jnp.float32) / math.sqrt(XATTN_HEAD_DIM)
    p = jax.nn.softmax(s, axis=-1).astype(v.dtype)
    o = jnp.einsum("bhqm,bmhd->bqhd", p, v).reshape(B, S, XATTN_HEADS * XATTN_HEAD_DIM)
    return o @ w_o


def hierarchical_moe(h, w_rg, b_rg, w_re, b_re, w_gate_up, w_down):
    B, S, D = h.shape
    N = B * S
    hf = h.reshape(N, D)
    group_logits = (hf @ w_rg).astype(jnp.float32) + b_rg.astype(jnp.float32)
    group_prob = jax.nn.softmax(group_logits, axis=-1)
    _, g_idx = lax.top_k(group_logits, 1)
    p_group = jnp.take_along_axis(group_prob, g_idx, axis=-1)
    exp_logits = ((hf @ w_re).astype(jnp.float32) + b_re.astype(jnp.float32)).reshape(N, N_GROUPS, EXPERTS_PER_GROUP)
    idx = jnp.broadcast_to(g_idx[:, :, None], (N, 1, EXPERTS_PER_GROUP))
    in_group = jnp.take_along_axis(exp_logits, idx, axis=1)[:, 0]
    top_vals, top_local = lax.top_k(in_group, TOP_K_IN_GROUP)
    combine = p_group * jax.nn.softmax(top_vals, axis=-1)
    expert_id = (g_idx * EXPERTS_PER_GROUP + top_local).reshape(-1)
    token_id = jnp.repeat(jnp.arange(N, dtype=jnp.int32), TOP_K_IN_GROUP)
    weight = combine.reshape(-1)
    NK = N * TOP_K_IN_GROUP
    order = jnp.argsort(expert_id)
    e_sorted, tok_sorted, w_sorted = expert_id[order], token_id[order], weight[order]
    counts = jnp.zeros((N_EXPERTS,), jnp.int32).at[expert_id].add(1)
    padded = (counts + DISPATCH_BLOCK - 1) // DISPATCH_BLOCK * DISPATCH_BLOCK
    start = jnp.cumsum(counts) - counts
    pad_end = jnp.cumsum(padded)
    pad_start = pad_end - padded
    dest = pad_start[e_sorted] + (jnp.arange(NK, dtype=jnp.int32) - start[e_sorted])
    P = (NK + DISPATCH_BLOCK - 1) // DISPATCH_BLOCK * DISPATCH_BLOCK + N_EXPERTS * DISPATCH_BLOCK
    n_blocks = P // DISPATCH_BLOCK
    x_disp = jnp.zeros((P, D), h.dtype).at[dest].set(hf[tok_sorted])
    block_expert = jnp.minimum(
        jnp.searchsorted(pad_end, jnp.arange(n_blocks, dtype=jnp.int32) * DISPATCH_BLOCK, side="right"),
        N_EXPERTS - 1)

    def expert_block(args):
        xb, e = args
        gate, up = jnp.split(xb @ w_gate_up[e], 2, axis=-1)
        return (jax.nn.silu(gate) * up) @ w_down[e]

    y_disp = lax.map(expert_block, (x_disp.reshape(n_blocks, DISPATCH_BLOCK, D), block_expert)).reshape(P, D)
    y = jnp.zeros((N, D), h.dtype).at[tok_sorted].add(y_disp[dest] * w_sorted[:, None].astype(h.dtype))
    return y.reshape(B, S, D)


def setup_inputs(seed: int = 0) -> dict:
    key = jax.random.key(seed)
    ks = jax.random.split(key, 32)

    def nrm(k, shape, scale):
        return jax.random.normal(k, shape, jnp.float32) * scale

    def gain(k, n):
        return 1.0 + 0.02 * jax.random.normal(k, (DEPTH, n), jnp.float32)

    offsets = jax.random.randint(ks[2], (BATCH, 1), 0, 4096, dtype=jnp.int32)
    positions = offsets + jnp.arange(SEQ, dtype=jnp.int32)[None, :]
    return {
        "x": nrm(ks[0], (BATCH, SEQ, D_MODEL), 1.0),
        "mem": nrm(ks[1], (BATCH, MEM_LEN, D_MODEL), 1.0),
        "positions": positions,
        "g_mix": gain(ks[3], D_MODEL),
        "w_in": nrm(ks[4], (DEPTH, D_MODEL, IN_COLS), D_MODEL ** -0.5),
        "b_gate": nrm(ks[5], (DEPTH, N_BRANCHES * D_MODEL), 0.02),
        "g_cq": gain(ks[6], Q_LORA_RANK),
        "w_uq": nrm(ks[7], (DEPTH, Q_LORA_RANK, MLA_HEADS * (QK_NOPE_DIM + QK_ROPE_DIM)), Q_LORA_RANK ** -0.5),
        "g_ckv": gain(ks[8], KV_LORA_RANK),
        "w_ukv": nrm(ks[9], (DEPTH, KV_LORA_RANK, MLA_HEADS * (QK_NOPE_DIM + V_HEAD_DIM)), KV_LORA_RANK ** -0.5),
        "conv_w": nrm(ks[10], (DEPTH, CONV_KSIZE, CONV_WIDTH), CONV_KSIZE ** -0.5),
        "w_branch_a": nrm(ks[11], (DEPTH, MLA_HEADS * V_HEAD_DIM, D_MODEL), (MLA_HEADS * V_HEAD_DIM) ** -0.5),
        "w_branch_b": nrm(ks[12], (DEPTH, CONV_WIDTH, D_MODEL), CONV_WIDTH ** -0.5),
        "w_out": nrm(ks[13], (DEPTH, D_MODEL, D_MODEL), D_MODEL ** -0.5),
        "g_xattn": gain(ks[14], D_MODEL),
        "g_mem": gain(ks[15], D_MODEL),
        "w_q_mem": nrm(ks[16], (DEPTH, D_MODEL, XATTN_HEADS * XATTN_HEAD_DIM), D_MODEL ** -0.5),
        "w_kv_mem": nrm(ks[17], (DEPTH, D_MODEL, 2 * XATTN_HEADS * XATTN_HEAD_DIM), D_MODEL ** -0.5),
        "w_o_mem": nrm(ks[18], (DEPTH, XATTN_HEADS * XATTN_HEAD_DIM, D_MODEL), (XATTN_HEADS * XATTN_HEAD_DIM) ** -0.5),
        "g_moe": gain(ks[19], D_MODEL),
        "w_router_group": nrm(ks[20], (DEPTH, D_MODEL, N_GROUPS), D_MODEL ** -0.5),
        "b_router_group": nrm(ks[21], (DEPTH, N_GROUPS), 0.01),
        "w_router_expert": nrm(ks[22], (DEPTH, D_MODEL, N_EXPERTS), D_MODEL ** -0.5),
        "b_router_expert": nrm(ks[23], (DEPTH, N_EXPERTS), 0.01),
        "w_gate_up": nrm(ks[24], (DEPTH, N_EXPERTS, D_MODEL, 2 * EXPERT_FF), D_MODEL ** -0.5),
        "w_down": nrm(ks[25], (DEPTH, N_EXPERTS, EXPERT_FF, D_MODEL), EXPERT_FF ** -0.5),
        "g_final": 1.0 + 0.02 * jax.random.normal(ks[26], (D_MODEL,), jnp.float32),
    }


def reference(x, mem, positions, g_mix, w_in, b_gate, g_cq, w_uq, g_ckv, w_ukv, conv_w,
              w_branch_a, w_branch_b, w_out, g_xattn, g_mem, w_q_mem, w_kv_mem, w_o_mem,
              g_moe, w_router_group, b_router_group, w_router_expert, b_router_expert,
              w_gate_up, w_down, g_final):
    cos, sin = rope_tables(positions)
    for l in range(DEPTH):
        h = rmsnorm(x, g_mix[l])
        proj = h @ w_in[l]
        c_q, c_kv, k_rope_raw, bcx, gate_logits = jnp.split(
            proj, [COL_CQ, COL_CKV, COL_KROPE, COL_CONV], axis=-1)
        attn = mla_attention(c_q, c_kv, k_rope_raw, cos, sin, g_cq[l], w_uq[l], g_ckv[l], w_ukv[l])
        conv = gated_short_conv(bcx, conv_w[l])
        gates = jax.nn.sigmoid((gate_logits + b_gate[l]).astype(jnp.float32)).astype(x.dtype)
        g_a, g_b = jnp.split(gates, N_BRANCHES, axis=-1)
        mixed = g_a * (attn @ w_branch_a[l]) + g_b * (conv @ w_branch_b[l])
        x = x + mixed @ w_out[l]
        x = x + memory_cross_attention(rmsnorm(x, g_xattn[l]), rmsnorm(mem, g_mem[l]),
                                       w_q_mem[l], w_kv_mem[l], w_o_mem[l])
        x = x + hierarchical_moe(rmsnorm(x, g_moe[l]), w_router_group[l], b_router_group[l],
                                 w_router_expert[l], b_router_expert[l], w_gate_up[l], w_down[l])
    return rmsnorm(x, g_final)
```

```python
import functools
import math

import jax
import jax.numpy as jnp
from jax import lax
from jax.experimental import pallas as pl
from jax.experimental.pallas import tpu as pltpu

F32 = jnp.float32
BF16 = jnp.bfloat16
I32 = jnp.int32

D_MODEL = 1024
MEM_LEN = 256
MLA_HEADS = 8
QK_NOPE_DIM = 64
QK_ROPE_DIM = 32
V_HEAD_DIM = 64
Q_LORA_RANK = 256
KV_LORA_RANK = 128
ROPE_THETA = 10000.0
CONV_WIDTH = 512
XATTN_HEADS = 4
XATTN_HEAD_DIM = 128
N_GROUPS = 8
EXPERTS_PER_GROUP = 8
N_EXPERTS = N_GROUPS * EXPERTS_PER_GROUP
EXPERT_FF = 256
EPS = 1e-6

COL_CQ = Q_LORA_RANK
COL_CKV = COL_CQ + KV_LORA_RANK
COL_KROPE = COL_CKV + QK_ROPE_DIM
COL_CONV = COL_KROPE + 3 * CONV_WIDTH

LANES = 128
HEAD_SLOT = LANES
ROPE_LO = QK_NOPE_DIM
ROPE_HALF = QK_ROPE_DIM // 2
SLOT_BLOCK = 256
ROUTE_LANE0 = N_GROUPS
NEG_BIG = -1e30

TM = 512
TQ = 256
TK = 256
TROW = 256
DMA_UNROLL = 8
VMEM_LIMIT = 56 * 1024 * 1024


def _rms(x, g):
    return x * lax.rsqrt(jnp.mean(x * x, axis=-1, keepdims=True) + EPS) * g


def _const_spec(shape):
    nd = len(shape)
    return pl.BlockSpec(shape, lambda *_: (0,) * nd)


def _params(sem):
    return pltpu.CompilerParams(dimension_semantics=sem, vmem_limit_bytes=VMEM_LIMIT)


def _mem_kv_kernel(mem_ref, g_ref, w_ref, k_ref, v_ref):
    h = _rms(mem_ref[...], g_ref[...]).astype(BF16)
    kv = jnp.dot(h, w_ref[...], preferred_element_type=F32)
    half = XATTN_HEADS * XATTN_HEAD_DIM
    k_ref[...] = kv[:, :half].astype(BF16)
    v_ref[...] = kv[:, half:].astype(BF16)


def _mem_kv(mem2d, g_mem, w_kv):
    rows = mem2d.shape[0]
    half = XATTN_HEADS * XATTN_HEAD_DIM
    return pl.pallas_call(
        _mem_kv_kernel,
        grid=(rows // MEM_LEN,),
        in_specs=[pl.BlockSpec((MEM_LEN, D_MODEL), lambda i: (i, 0)),
                  _const_spec((1, D_MODEL)),
                  _const_spec((D_MODEL, 2 * half))],
        out_specs=[pl.BlockSpec((MEM_LEN, half), lambda i: (i, 0)),
                   pl.BlockSpec((MEM_LEN, half), lambda i: (i, 0))],
        out_shape=[jax.ShapeDtypeStruct((rows, half), BF16)] * 2,
        compiler_params=_params(("arbitrary",)),
        name="mem_kv",
    )(mem2d, g_mem, w_kv)


def _in_proj_kernel(x_ref, pos_ref, invf_ref, g_mix_ref, w1_ref, w2_ref, w3_ref, b_gate_ref,
                    g_cq_ref, wqa_ref, wqb_ref, g_ckv_ref, wkk_ref, wkv_ref, conv_w_ref, wbb_ref,
                    q_ref, k_ref, v_ref, ga_ref, mixb_ref, carry_ref, *, tiles_per_seq):
    step = pl.program_id(0)
    tm = x_ref.shape[0]
    h = _rms(x_ref[...], g_mix_ref[...]).astype(BF16)

    lane = lax.broadcasted_iota(I32, (tm, HEAD_SLOT), 1)
    rope_lane = (lane >= ROPE_LO) & (lane < ROPE_LO + QK_ROPE_DIM)
    ang = pos_ref[...].astype(F32) * invf_ref[...]
    cos_t = jnp.where(rope_lane, jnp.cos(ang), 0.0)
    sin_t = jnp.where(rope_lane, jnp.sin(ang), 0.0)

    p1 = jnp.dot(h, w1_ref[...], preferred_element_type=F32)
    c_q = p1[:, :COL_CQ]
    c_kv = p1[:, COL_CQ:COL_CKV]
    kr_a = p1[:, COL_CKV:COL_CKV + HEAD_SLOT]
    kr_b = p1[:, COL_CKV + HEAD_SLOT:COL_CKV + 2 * HEAD_SLOT]
    k_rot = kr_a * cos_t + kr_b * sin_t

    scale = math.log2(math.e) / math.sqrt(QK_NOPE_DIM + QK_ROPE_DIM)
    q_cos = jnp.where(lane < ROPE_LO, scale, cos_t * scale)
    q_sin = sin_t * scale
    cqn = _rms(c_q, g_cq_ref[...]).astype(BF16)
    qa = jnp.dot(cqn, wqa_ref[...], preferred_element_type=F32)
    qb = jnp.dot(cqn, wqb_ref[...], preferred_element_type=F32)
    ckvn = _rms(c_kv, g_ckv_ref[...]).astype(BF16)
    kk = jnp.dot(ckvn, wkk_ref[...], preferred_element_type=F32)
    for hd in range(MLA_HEADS):
        sl = slice(hd * HEAD_SLOT, (hd + 1) * HEAD_SLOT)
        q_ref[:, sl] = (qa[:, sl] * q_cos + qb[:, sl] * q_sin).astype(BF16)
        k_ref[:, sl] = (kk[:, sl] + k_rot).astype(BF16)
    v_ref[...] = jnp.dot(ckvn, wkv_ref[...], preferred_element_type=F32).astype(BF16)

    p2 = jnp.dot(h, w2_ref[...], preferred_element_type=F32)
    b_gate = p2[:, :CONV_WIDTH]
    u = p2[:, CONV_WIDTH:2 * CONV_WIDTH] * p2[:, 2 * CONV_WIDTH:]

    @pl.when(step % tiles_per_seq == 0)
    def _():
        carry_ref[...] = jnp.zeros_like(carry_ref)

    prev1 = carry_ref[7:8, :]
    prev2 = carry_ref[6:7, :]
    row = lax.broadcasted_iota(I32, (tm, CONV_WIDTH), 0)
    u1 = jnp.where(row == 0, prev1, pltpu.roll(u, 1, 0))
    u2 = jnp.where(row == 0, prev2, jnp.where(row == 1, prev1, pltpu.roll(u, 2, 0)))
    carry_ref[...] = u[tm - 8:, :]
    cw = conv_w_ref[...]
    conv = b_gate * (cw[0:1, :] * u2 + cw[1:2, :] * u1 + cw[2:3, :] * u)
    branch_b = jnp.dot(conv.astype(BF16), wbb_ref[...], preferred_element_type=F32)

    gates = jax.nn.sigmoid(jnp.dot(h, w3_ref[...], preferred_element_type=F32) + b_gate_ref[...])
    ga_ref[...] = gates[:, :D_MODEL].astype(BF16)
    mixb_ref[...] = (gates[:, D_MODEL:] * branch_b).astype(BF16)


def _in_proj(x2d, pos2d, invf, g_mix, w1, w2, w3, b_gate, g_cq, wqa, wqb, g_ckv, wkk, wkv,
             conv_w, wbb, seq):
    n = x2d.shape[0]
    row_spec = lambda c: pl.BlockSpec((TM, c), lambda i: (i, 0))
    consts = [invf, g_mix, w1, w2, w3, b_gate, g_cq, wqa, wqb, g_ckv, wkk, wkv, conv_w, wbb]
    kv_cols = MLA_HEADS * HEAD_SLOT
    return pl.pallas_call(
        functools.partial(_in_proj_kernel, tiles_per_seq=seq // TM),
        grid=(n // TM,),
        in_specs=[row_spec(D_MODEL), row_spec(1)] + [_const_spec(c.shape) for c in consts],
        out_specs=[row_spec(kv_cols), row_spec(kv_cols), row_spec(MLA_HEADS * V_HEAD_DIM),
                   row_spec(D_MODEL), row_spec(D_MODEL)],
        out_shape=[jax.ShapeDtypeStruct((n, kv_cols), BF16),
                   jax.ShapeDtypeStruct((n, kv_cols), BF16),
                   jax.ShapeDtypeStruct((n, MLA_HEADS * V_HEAD_DIM), BF16),
                   jax.ShapeDtypeStruct((n, D_MODEL), BF16),
                   jax.ShapeDtypeStruct((n, D_MODEL), BF16)],
        scratch_shapes=[pltpu.VMEM((8, CONV_WIDTH), F32)],
        compiler_params=_params(("arbitrary",)),
        name="in_proj",
    )(x2d, pos2d, *consts)


def _attn_kernel(q_ref, k_ref, v_ref, o_ref):
    qi = pl.program_id(1)
    tq = q_ref.shape[0]
    row = qi * tq + lax.broadcasted_iota(I32, (tq, TK), 0)
    col0 = lax.broadcasted_iota(I32, (tq, TK), 1)

    def step(j, carry, masked):
        start = pl.multiple_of(j * TK, TK)
        out = []
        for hd in range(MLA_HEADS):
            m, l, acc = carry[hd]
            hsl = slice(hd * HEAD_SLOT, (hd + 1) * HEAD_SLOT)
            psl = slice((hd // 2) * 2 * V_HEAD_DIM, (hd // 2 + 1) * 2 * V_HEAD_DIM)
            kb = k_ref[pl.ds(start, TK), hsl]
            vb = v_ref[pl.ds(start, TK), psl]
            s = lax.dot_general(q_ref[:, hsl], kb, (((1,), (1,)), ((), ())),
                                preferred_element_type=F32)
            if masked:
                s = jnp.where(col0 + j * TK <= row, s, NEG_BIG)
            m_new = jnp.maximum(m, jnp.max(s, axis=1, keepdims=True))
            alpha = jnp.exp2(m - m_new)
            p = jnp.exp2(s - m_new)
            l = alpha * l + jnp.sum(p, axis=1, keepdims=True)
            acc = alpha * acc + jnp.dot(p.astype(BF16), vb, preferred_element_type=F32)
            out.append((m_new, l, acc))
        return tuple(out)

    init = tuple((jnp.full((tq, 1), NEG_BIG, F32), jnp.zeros((tq, 1), F32),
                  jnp.zeros((tq, 2 * V_HEAD_DIM), F32)) for _ in range(MLA_HEADS))
    n_full = qi * (tq // TK)
    carry = lax.fori_loop(0, n_full, functools.partial(step, masked=False), init)
    for jj in range(tq // TK):
        carry = step(n_full + jj, carry, True)
    lane = lax.broadcasted_iota(I32, (tq, 2 * V_HEAD_DIM), 1)
    for hp in range(MLA_HEADS // 2):
        (_, l0, a0), (_, l1, a1) = carry[2 * hp], carry[2 * hp + 1]
        psl = slice(hp * 2 * V_HEAD_DIM, (hp + 1) * 2 * V_HEAD_DIM)
        o_ref[:, psl] = jnp.where(lane < V_HEAD_DIM, a0 / l0, a1 / l1).astype(BF16)


def _mla_attn(q, k, v, batch, seq):
    n = q.shape[0]
    nq = seq // TQ
    return pl.pallas_call(
        _attn_kernel,
        grid=(batch, nq),
        in_specs=[pl.BlockSpec((TQ, q.shape[1]), lambda b, i: (b * nq + i, 0)),
                  pl.BlockSpec((seq, k.shape[1]), lambda b, i: (b, 0)),
                  pl.BlockSpec((seq, v.shape[1]), lambda b, i: (b, 0))],
        out_specs=pl.BlockSpec((TQ, v.shape[1]), lambda b, i: (b * nq + i, 0)),
        out_shape=jax.ShapeDtypeStruct((n, v.shape[1]), BF16),
        compiler_params=_params(("arbitrary", "arbitrary")),
        name="mla_attn",
    )(q, k, v)


def _post_attn_kernel(x_ref, attn_ref, ga_ref, mixb_ref, wba_ref, wout_ref, g_x_ref, wq_ref,
                      km_ref, vm_ref, wo_ref, g_moe_ref, wr_ref, br_ref,
                      x2_ref, h3_ref, meta_ref, wts_ref, cnt_ref, run_ref):
    step = pl.program_id(0)
    tm = x_ref.shape[0]

    @pl.when(step == 0)
    def _():
        run_ref[...] = jnp.zeros_like(run_ref)

    a = jnp.dot(attn_ref[...], wba_ref[...], preferred_element_type=F32)
    mixed = ga_ref[...].astype(F32) * a + mixb_ref[...].astype(F32)
    x1 = x_ref[...] + jnp.dot(mixed.astype(BF16), wout_ref[...], preferred_element_type=F32)

    h2 = _rms(x1, g_x_ref[...]).astype(BF16)
    qm = jnp.dot(h2, wq_ref[...], preferred_element_type=F32) * (1.0 / math.sqrt(XATTN_HEAD_DIM))
    qm = qm.astype(BF16)
    heads = []
    for hd in range(XATTN_HEADS):
        sl = slice(hd * XATTN_HEAD_DIM, (hd + 1) * XATTN_HEAD_DIM)
        s = lax.dot_general(qm[:, sl], km_ref[:, sl], (((1,), (1,)), ((), ())),
                            preferred_element_type=F32)
        p = jnp.exp(s - jnp.max(s, axis=1, keepdims=True))
        o = jnp.dot(p.astype(BF16), vm_ref[:, sl], preferred_element_type=F32)
        heads.append((o / jnp.sum(p, axis=1, keepdims=True)).astype(BF16))
    o_all = jnp.concatenate(heads, axis=1)
    x2 = x1 + jnp.dot(o_all, wo_ref[...], preferred_element_type=F32)
    x2_ref[...] = x2

    h3 = _rms(x2, g_moe_ref[...])
    h3_ref[...] = h3
    logits = jnp.dot(h3.astype(BF16), wr_ref[...], preferred_element_type=F32) + br_ref[...]
    lane = lax.broadcasted_iota(I32, (tm, LANES), 1)
    lane_f = lane.astype(F32)

    def first_lane(hit):
        return jnp.min(jnp.where(hit, lane_f, float(LANES)), axis=1, keepdims=True).astype(I32)

    gl = jnp.where(lane < N_GROUPS, logits, -jnp.inf)
    gmax = jnp.max(gl, axis=1, keepdims=True)
    g_idx = first_lane(gl == gmax)
    p_group = 1.0 / jnp.sum(jnp.exp(gl - gmax), axis=1, keepdims=True)
    e_lo = ROUTE_LANE0 + g_idx * EXPERTS_PER_GROUP
    el = jnp.where((lane >= e_lo) & (lane < e_lo + EXPERTS_PER_GROUP), logits, -jnp.inf)
    v1 = jnp.max(el, axis=1, keepdims=True)
    i1 = first_lane(el == v1)
    el2 = jnp.where(lane == i1, -jnp.inf, el)
    v2 = jnp.max(el2, axis=1, keepdims=True)
    i2 = first_lane(el2 == v2)
    t = jnp.exp(v2 - v1)
    w0 = p_group / (1.0 + t)
    w1 = p_group * t / (1.0 + t)

    oh0 = lane == i1
    oh1 = lane == i2
    oh = (oh0 | oh1).astype(BF16)
    r_i = lax.broadcasted_iota(I32, (tm, tm), 0)
    c_i = lax.broadcasted_iota(I32, (tm, tm), 1)
    lower = (c_i < r_i).astype(BF16)
    before = jnp.dot(lower, oh, preferred_element_type=F32) + run_ref[...]
    rank0 = jnp.sum(jnp.where(oh0, before, 0.0), axis=1, keepdims=True).astype(I32)
    rank1 = jnp.sum(jnp.where(oh1, before, 0.0), axis=1, keepdims=True).astype(I32)
    run_ref[...] = run_ref[...] + jnp.sum(oh.astype(F32), axis=0, keepdims=True)
    cnt_ref[...] = jnp.broadcast_to(run_ref[...], cnt_ref.shape).astype(I32)

    meta = jnp.where(lane == 0, i1 - ROUTE_LANE0, 0)
    meta = jnp.where(lane == 1, i2 - ROUTE_LANE0, meta)
    meta = jnp.where(lane == 2, rank0, meta)
    meta = jnp.where(lane == 3, rank1, meta)
    meta_ref[...] = meta
    wts_ref[...] = jnp.where(lane == 0, w0, jnp.where(lane == 1, w1, 0.0))


def _post_attn(x2d, attn, ga, mixb, wba, wout, g_x, wq, km, vm, wo, g_moe, wr, br, seq):
    n = x2d.shape[0]
    tps = seq // TM
    row_spec = lambda c: pl.BlockSpec((TM, c), lambda i: (i, 0))
    mem_spec = pl.BlockSpec((MEM_LEN, XATTN_HEADS * XATTN_HEAD_DIM), lambda i: (i // tps, 0))
    return pl.pallas_call(
        _post_attn_kernel,
        grid=(n // TM,),
        in_specs=[row_spec(D_MODEL), row_spec(MLA_HEADS * V_HEAD_DIM), row_spec(D_MODEL),
                  row_spec(D_MODEL), _const_spec(wba.shape), _const_spec(wout.shape),
                  _const_spec(g_x.shape), _const_spec(wq.shape), mem_spec, mem_spec,
                  _const_spec(wo.shape), _const_spec(g_moe.shape), _const_spec(wr.shape),
                  _const_spec(br.shape)],
        out_specs=[row_spec(D_MODEL), row_spec(D_MODEL), row_spec(LANES), row_spec(LANES),
                   _const_spec((8, LANES))],
        out_shape=[jax.ShapeDtypeStruct((n, D_MODEL), F32),
                   jax.ShapeDtypeStruct((n, D_MODEL), F32),
                   jax.ShapeDtypeStruct((n, LANES), I32),
                   jax.ShapeDtypeStruct((n, LANES), F32),
                   jax.ShapeDtypeStruct((8, LANES), I32)],
        scratch_shapes=[pltpu.VMEM((1, LANES), F32)],
        compiler_params=_params(("arbitrary",)),
        name="post_attn",
    )(x2d, attn, ga, mixb, wba, wout, g_x, wq, km, vm, wo, g_moe, wr, br)


def _row_copy(src_ref, src_row, dst_ref, dst_row, sem):
    return pltpu.make_async_copy(src_ref.at[pl.ds(src_row, 1), :], dst_ref.at[pl.ds(dst_row, 1), :],
                                 sem)


def _dispatch_kernel(fill_ref, dest_ref, zeros_ref, h3_ref, xd_ref, row_sem, fill_sem):
    i = pl.program_id(0)

    def fill_copy(b):
        return pltpu.make_async_copy(
            zeros_ref, xd_ref.at[pl.ds(pl.multiple_of(b * SLOT_BLOCK, SLOT_BLOCK), SLOT_BLOCK)],
            fill_sem)

    @pl.when(i == 0)
    def _():
        def start(b, c):
            @pl.when(fill_ref[b] > 0)
            def _():
                fill_copy(b).start()
            return c

        def wait(b, c):
            @pl.when(fill_ref[b] > 0)
            def _():
                fill_copy(b).wait()
            return c

        lax.fori_loop(0, fill_ref.shape[0], start, 0)
        lax.fori_loop(0, fill_ref.shape[0], wait, 0)

    def copies(r):
        return (_row_copy(h3_ref, r, xd_ref, dest_ref[0, 0, 2 * r], row_sem),
                _row_copy(h3_ref, r, xd_ref, dest_ref[0, 0, 2 * r + 1], row_sem))

    def issue(r, c):
        for cp in copies(r):
            cp.start()
        return c

    def drain(r, c):
        for cp in copies(r):
            cp.wait()
        return c

    lax.fori_loop(0, TROW, issue, 0, unroll=DMA_UNROLL)
    lax.fori_loop(0, TROW, drain, 0, unroll=DMA_UNROLL)


def _dispatch(fill_blk, dest3, h3, n_slots):
    n, d = h3.shape
    grid_spec = pltpu.PrefetchScalarGridSpec(
        num_scalar_prefetch=1,
        grid=(n // TROW,),
        in_specs=[pl.BlockSpec((1, 1, 2 * TROW), lambda i, fb: (i, 0, 0), memory_space=pltpu.SMEM),
                  pl.BlockSpec((SLOT_BLOCK, d), lambda i, fb: (0, 0)),
                  pl.BlockSpec((TROW, d), lambda i, fb: (i, 0))],
        out_specs=pl.BlockSpec(memory_space=pl.ANY),
        scratch_shapes=[pltpu.SemaphoreType.DMA, pltpu.SemaphoreType.DMA],
    )
    return pl.pallas_call(
        _dispatch_kernel,
        grid_spec=grid_spec,
        out_shape=jax.ShapeDtypeStruct((n_slots, d), F32),
        compiler_params=_params(("arbitrary",)),
        name="dispatch",
    )(fill_blk, dest3, jnp.zeros((SLOT_BLOCK, d), F32), h3)


def _experts_kernel(be_ref, nb_ref, xd_ref, wgu_ref, wdn_ref, yd_ref, wgu_bf, wdn_bf):
    i = pl.program_id(0)

    @pl.when(i < nb_ref[0])
    def _():
        @pl.when((i == 0) | (be_ref[i] != be_ref[jnp.maximum(i - 1, 0)]))
        def _():
            wgu_bf[...] = wgu_ref[0].astype(BF16)
            wdn_bf[...] = wdn_ref[0].astype(BF16)

        gu = jnp.dot(xd_ref[...].astype(BF16), wgu_bf[...], preferred_element_type=F32)
        gate = gu[:, :EXPERT_FF]
        up = gu[:, EXPERT_FF:]
        act = (gate * jax.nn.sigmoid(gate) * up).astype(BF16)
        yd_ref[...] = jnp.dot(act, wdn_bf[...], preferred_element_type=F32)


def _experts(block_expert, n_blocks_used, xd, w_gate_up, w_down):
    n_slots, d = xd.shape
    slot_map = lambda i, be, nb: (jnp.minimum(i, nb[0] - 1), 0)
    grid_spec = pltpu.PrefetchScalarGridSpec(
        num_scalar_prefetch=2,
        grid=(n_slots // SLOT_BLOCK,),
        in_specs=[pl.BlockSpec((SLOT_BLOCK, d), slot_map),
                  pl.BlockSpec((1, D_MODEL, 2 * EXPERT_FF), lambda i, be, nb: (be[i], 0, 0)),
                  pl.BlockSpec((1, EXPERT_FF, D_MODEL), lambda i, be, nb: (be[i], 0, 0))],
        out_specs=pl.BlockSpec((SLOT_BLOCK, d), slot_map),
        scratch_shapes=[pltpu.VMEM((D_MODEL, 2 * EXPERT_FF), BF16),
                        pltpu.VMEM((EXPERT_FF, D_MODEL), BF16)],
    )
    return pl.pallas_call(
        _experts_kernel,
        grid_spec=grid_spec,
        out_shape=jax.ShapeDtypeStruct(xd.shape, F32),
        input_output_aliases={2: 0},
        compiler_params=_params(("arbitrary",)),
        name="experts",
    )(block_expert, n_blocks_used, xd, w_gate_up, w_down)


def _combine_kernel(dest_ref, dest_next_ref, x2_ref, wts_ref, g_ref, yd_ref, o_ref, buf, sem):
    i = pl.program_id(0)
    last = pl.num_programs(0) - 1
    rows = x2_ref.shape[0]
    slot = i % 2

    def copies(d_ref, slt, r):
        return (_row_copy(yd_ref, d_ref[0, 0, 2 * r], buf.at[slt, 0], r, sem.at[slt]),
                _row_copy(yd_ref, d_ref[0, 0, 2 * r + 1], buf.at[slt, 1], r, sem.at[slt]))

    def issue(d_ref, slt):
        def body(r, c):
            for cp in copies(d_ref, slt, r):
                cp.start()
            return c
        lax.fori_loop(0, rows, body, 0, unroll=DMA_UNROLL)

    @pl.when(i == 0)
    def _():
        issue(dest_ref, 0)

    @pl.when(i < last)
    def _():
        issue(dest_next_ref, 1 - slot)

    def drain(r, c):
        for cp in copies(dest_ref, slot, r):
            cp.wait()
        return c

    lax.fori_loop(0, rows, drain, 0, unroll=DMA_UNROLL)

    lane = lax.broadcasted_iota(I32, (rows, LANES), 1)
    wts = wts_ref[...]
    w0 = jnp.sum(jnp.where(lane == 0, wts, 0.0), axis=1, keepdims=True)
    w1 = jnp.sum(jnp.where(lane == 1, wts, 0.0), axis=1, keepdims=True)
    y = buf[slot, 0] * w0 + buf[slot, 1] * w1
    o_ref[...] = _rms(x2_ref[...] + y, g_ref[...])


def _combine(dest3, x2, wts, g_final, yd):
    n = x2.shape[0]
    steps = n // TROW
    dest_spec = lambda f: pl.BlockSpec((1, 1, 2 * TROW), f, memory_space=pltpu.SMEM)
    return pl.pallas_call(
        _combine_kernel,
        grid=(steps,),
        in_specs=[dest_spec(lambda i: (i, 0, 0)),
                  dest_spec(lambda i: (jnp.minimum(i + 1, steps - 1), 0, 0)),
                  pl.BlockSpec((TROW, D_MODEL), lambda i: (i, 0)),
                  pl.BlockSpec((TROW, LANES), lambda i: (i, 0)),
                  _const_spec((1, D_MODEL)),
                  pl.BlockSpec(memory_space=pl.ANY)],
        out_specs=pl.BlockSpec((TROW, D_MODEL), lambda i: (i, 0)),
        out_shape=jax.ShapeDtypeStruct((n, D_MODEL), F32),
        scratch_shapes=[pltpu.VMEM((2, 2, TROW, D_MODEL), F32),
                        pltpu.SemaphoreType.DMA((2,))],
        compiler_params=_params(("arbitrary",)),
        name="combine",
    )(dest3, dest3, x2, wts, g_final, yd)


def _head_slots(w, per_head, pieces):
    rows = w.shape[0]
    wh = w.reshape(rows, MLA_HEADS, per_head)
    out = jnp.zeros((rows, MLA_HEADS, HEAD_SLOT), w.dtype)
    for src_lo, src_hi, dst_lo, sign in pieces:
        out = out.at[:, :, dst_lo:dst_lo + (src_hi - src_lo)].set(sign * wh[:, :, src_lo:src_hi])
    return out.reshape(rows, MLA_HEADS * HEAD_SLOT)


def _layer(x2d, pos2d, invf, km, vm, batch, seq, g_mix, w_in, b_gate, g_cq, w_uq, g_ckv, w_ukv,
           conv_w, w_branch_a, w_branch_b, w_out, g_xattn, w_q_mem, w_o_mem, g_moe,
           w_router_group, b_router_group, w_router_expert, b_router_expert, w_gate_up, w_down,
           g_final):
    n = x2d.shape[0]
    row = lambda v: v.reshape(1, -1)
    lo, mid, hi = ROPE_LO, ROPE_LO + ROPE_HALF, ROPE_LO + QK_ROPE_DIM

    w_kr = w_in[:, COL_CKV:COL_KROPE]
    kr_a = jnp.zeros((D_MODEL, HEAD_SLOT), F32).at[:, lo:hi].set(w_kr)
    kr_b = (jnp.zeros((D_MODEL, HEAD_SLOT), F32)
            .at[:, lo:mid].set(-w_kr[:, ROPE_HALF:]).at[:, mid:hi].set(w_kr[:, :ROPE_HALF]))
    w1 = jnp.concatenate([w_in[:, :COL_CKV], kr_a, kr_b], axis=1).astype(BF16)
    w2 = w_in[:, COL_KROPE:COL_CONV].astype(BF16)
    w3 = w_in[:, COL_CONV:].astype(BF16)
    qd = QK_NOPE_DIM + QK_ROPE_DIM
    wqa = _head_slots(w_uq, qd, [(0, qd, 0, 1.0)]).astype(BF16)
    wqb = _head_slots(w_uq, qd, [(mid, hi, lo, -1.0), (lo, mid, mid, 1.0)]).astype(BF16)
    kvd = QK_NOPE_DIM + V_HEAD_DIM
    wkk = _head_slots(w_ukv, kvd, [(0, QK_NOPE_DIM, 0, 1.0)]).astype(BF16)
    wkv = (w_ukv.reshape(KV_LORA_RANK, MLA_HEADS, kvd)[:, :, QK_NOPE_DIM:]
           .reshape(KV_LORA_RANK, MLA_HEADS * V_HEAD_DIM).astype(BF16))
    conv_w8 = jnp.zeros((8, CONV_WIDTH), F32).at[:conv_w.shape[0]].set(conv_w)

    q, k, v, ga, mixb = _in_proj(x2d, pos2d, invf, row(g_mix), w1, w2, w3, row(b_gate), row(g_cq),
                                 wqa, wqb, row(g_ckv), wkk, wkv, conv_w8,
                                 w_branch_b.astype(BF16), seq)
    attn = _mla_attn(q, k, v, batch, seq)

    wr = (jnp.zeros((D_MODEL, LANES), F32).at[:, :N_GROUPS].set(w_router_group)
          .at[:, ROUTE_LANE0:ROUTE_LANE0 + N_EXPERTS].set(w_router_expert).astype(BF16))
    br = (jnp.zeros((1, LANES), F32).at[0, :N_GROUPS].set(b_router_group)
          .at[0, ROUTE_LANE0:ROUTE_LANE0 + N_EXPERTS].set(b_router_expert))
    x2, h3, meta, wts, cnt = _post_attn(x2d, attn, ga, mixb, w_branch_a.astype(BF16),
                                   w_out.astype(BF16), row(g_xattn), w_q_mem.astype(BF16), km, vm,
                                   w_o_mem.astype(BF16), row(g_moe), wr, br, seq)

    counts = cnt[0, ROUTE_LANE0:ROUTE_LANE0 + N_EXPERTS]
    padded = (counts + SLOT_BLOCK - 1) // SLOT_BLOCK * SLOT_BLOCK
    pad_end = jnp.cumsum(padded)
    pad_start = pad_end - padded
    n_slots = (2 * n + SLOT_BLOCK - 1) // SLOT_BLOCK * SLOT_BLOCK + N_EXPERTS * SLOT_BLOCK
    expert_ids = meta[:, 0:2]
    onehot = expert_ids[:, :, None] == jnp.arange(N_EXPERTS, dtype=I32)[None, None, :]
    dest = jnp.sum(jnp.where(onehot, pad_start[None, None, :], 0), axis=-1) + meta[:, 2:4]
    dest3 = dest.astype(I32).reshape(n // TROW, 1, 2 * TROW)
    block_lo = jnp.arange(n_slots // SLOT_BLOCK, dtype=I32) * SLOT_BLOCK
    block_expert = jnp.minimum(jnp.sum(pad_end[None, :] <= block_lo[:, None], axis=1),
                               N_EXPERTS - 1).astype(I32)
    n_blocks_used = (pad_end[-1:] // SLOT_BLOCK).astype(I32)
    fill_blk = ((block_lo + SLOT_BLOCK == pad_end[block_expert])
                | (block_lo >= pad_end[-1])).astype(I32)

    xd = _dispatch(fill_blk, dest3, h3, n_slots)
    yd = _experts(block_expert, n_blocks_used, xd, w_gate_up, w_down)
    return _combine(dest3, x2, wts, row(g_final), yd)


def kernel(x, mem, positions, g_mix, w_in, b_gate, g_cq, w_uq, g_ckv, w_ukv, conv_w, w_branch_a,
           w_branch_b, w_out, g_xattn, g_mem, w_q_mem, w_kv_mem, w_o_mem, g_moe, w_router_group,
           b_router_group, w_router_expert, b_router_expert, w_gate_up, w_down, g_final):
    batch, seq, d = x.shape
    depth = g_mix.shape[0]
    assert d == D_MODEL and depth == 1 and seq % TM == 0 and seq % TQ == 0
    n = batch * seq
    x2d = x.reshape(n, d)
    pos2d = positions.reshape(n, 1).astype(I32)
    inv_freq = ROPE_THETA ** (-jnp.arange(0, QK_ROPE_DIM, 2, dtype=F32) / QK_ROPE_DIM)
    invf = (jnp.zeros((1, HEAD_SLOT), F32).at[0, ROPE_LO:ROPE_LO + ROPE_HALF].set(inv_freq)
            .at[0, ROPE_LO + ROPE_HALF:ROPE_LO + QK_ROPE_DIM].set(inv_freq))
    l = 0
    km, vm = _mem_kv(mem.reshape(batch * MEM_LEN, d), g_mem[l].reshape(1, -1),
                     w_kv_mem[l].astype(BF16))
    out = _layer(x2d, pos2d, invf, km, vm, batch, seq, g_mix[l], w_in[l], b_gate[l], g_cq[l],
                 w_uq[l], g_ckv[l], w_ukv[l], conv_w[l], w_branch_a[l], w_branch_b[l], w_out[l],
                 g_xattn[l], w_q_mem[l], w_o_mem[l], g_moe[l], w_router_group[l],
                 b_router_group[l], w_router_expert[l], b_router_expert[l], w_gate_up[l],
                 w_down[l], g_final)
    return out.reshape(batch, seq, d)
```

```python
import functools
import math

import jax
import jax.numpy as jnp
from jax import lax
from jax.experimental import pallas as pl
from jax.experimental.pallas import tpu as pltpu

F32 = jnp.float32
BF16 = jnp.bfloat16
I32 = jnp.int32

D_MODEL = 1024
MEM_LEN = 256
MLA_HEADS = 8
QK_NOPE_DIM = 64
QK_ROPE_DIM = 32
V_HEAD_DIM = 64
Q_LORA_RANK = 256
KV_LORA_RANK = 128
ROPE_THETA = 10000.0
CONV_WIDTH = 512
XATTN_HEADS = 4
XATTN_HEAD_DIM = 128
N_GROUPS = 8
EXPERTS_PER_GROUP = 8
N_EXPERTS = N_GROUPS * EXPERTS_PER_GROUP
EXPERT_FF = 256
EPS = 1e-6

COL_CQ = Q_LORA_RANK
COL_CKV = COL_CQ + KV_LORA_RANK
COL_KROPE = COL_CKV + QK_ROPE_DIM
COL_CONV = COL_KROPE + 3 * CONV_WIDTH

LANES = 128
HEAD_SLOT = LANES
ROPE_LO = QK_NOPE_DIM
ROPE_HALF = QK_ROPE_DIM // 2
L_LANE = V_HEAD_DIM
SLOT_BLOCK = 256
ROUTE_LANE0 = N_GROUPS
NEG_BIG = -1e30

TM = 512
TQ = 512
TK = 512
ATTN_CHAINS = 8
TROW = 256
DMA_UNROLL = 8
VMEM_LIMIT = 56 * 1024 * 1024


def _rms(x, g):
    return x * lax.rsqrt(jnp.mean(x * x, axis=-1, keepdims=True) + EPS) * g


def _const_spec(shape):
    nd = len(shape)
    return pl.BlockSpec(shape, lambda *_: (0,) * nd)


def _params(sem):
    return pltpu.CompilerParams(dimension_semantics=sem, vmem_limit_bytes=VMEM_LIMIT)


def _mem_kv_kernel(mem_ref, g_ref, w_ref, k_ref, v_ref):
    h = _rms(mem_ref[...], g_ref[...]).astype(BF16)
    kv = jnp.dot(h, w_ref[...], preferred_element_type=F32)
    half = XATTN_HEADS * XATTN_HEAD_DIM
    k_ref[...] = kv[:, :half].astype(BF16)
    v_ref[...] = kv[:, half:].astype(BF16)


def _mem_kv(mem2d, g_mem, w_kv):
    rows = mem2d.shape[0]
    half = XATTN_HEADS * XATTN_HEAD_DIM
    return pl.pallas_call(
        _mem_kv_kernel,
        grid=(rows // MEM_LEN,),
        in_specs=[pl.BlockSpec((MEM_LEN, D_MODEL), lambda i: (i, 0)),
                  _const_spec((1, D_MODEL)),
                  _const_spec((D_MODEL, 2 * half))],
        out_specs=[pl.BlockSpec((MEM_LEN, half), lambda i: (i, 0)),
                   pl.BlockSpec((MEM_LEN, half), lambda i: (i, 0))],
        out_shape=[jax.ShapeDtypeStruct((rows, half), BF16)] * 2,
        compiler_params=_params(("arbitrary",)),
        name="mem_kv",
    )(mem2d, g_mem, w_kv)


def _in_proj_kernel(x_ref, pos_ref, invf_ref, g_mix_ref, w1_ref, w2_ref, w3_ref, b_gate_ref,
                    g_cq_ref, wqa_ref, wqb_ref, g_ckv_ref, wkk_ref, wkv_ref, conv_w_ref, wbb_ref,
                    q_ref, k_ref, v_ref, ga_ref, mixb_ref, carry_ref, *, tiles_per_seq):
    step = pl.program_id(0)
    tm = x_ref.shape[0]
    h = _rms(x_ref[...], g_mix_ref[...]).astype(BF16)

    lane = lax.broadcasted_iota(I32, (tm, HEAD_SLOT), 1)
    rope_lane = (lane >= ROPE_LO) & (lane < ROPE_LO + QK_ROPE_DIM)
    ang = pos_ref[...].astype(F32) * invf_ref[...]
    cos_t = jnp.where(rope_lane, jnp.cos(ang), 0.0)
    sin_t = jnp.where(rope_lane, jnp.sin(ang), 0.0)

    p1 = jnp.dot(h, w1_ref[...], preferred_element_type=F32)
    c_q = p1[:, :COL_CQ]
    c_kv = p1[:, COL_CQ:COL_CKV]
    kr_a = p1[:, COL_CKV:COL_CKV + HEAD_SLOT]
    kr_b = p1[:, COL_CKV + HEAD_SLOT:COL_CKV + 2 * HEAD_SLOT]
    k_rot = kr_a * cos_t + kr_b * sin_t

    scale = math.log2(math.e) / math.sqrt(QK_NOPE_DIM + QK_ROPE_DIM)
    q_cos = jnp.where(lane < ROPE_LO, scale, cos_t * scale)
    q_sin = sin_t * scale
    cqn = _rms(c_q, g_cq_ref[...]).astype(BF16)
    qa = jnp.dot(cqn, wqa_ref[...], preferred_element_type=F32)
    qb = jnp.dot(cqn, wqb_ref[...], preferred_element_type=F32)
    ckvn = _rms(c_kv, g_ckv_ref[...]).astype(BF16)
    kk = jnp.dot(ckvn, wkk_ref[...], preferred_element_type=F32)
    vv = jnp.dot(ckvn, wkv_ref[...], preferred_element_type=F32)
    ones_lane = jnp.where(lane == L_LANE, 1.0, 0.0)
    for hd in range(MLA_HEADS):
        sl = slice(hd * HEAD_SLOT, (hd + 1) * HEAD_SLOT)
        q_ref[:, sl] = (qa[:, sl] * q_cos + qb[:, sl] * q_sin).astype(BF16)
        k_ref[:, sl] = (kk[:, sl] + k_rot).astype(BF16)
        v_ref[:, sl] = (vv[:, sl] + ones_lane).astype(BF16)

    p2 = jnp.dot(h, w2_ref[...], preferred_element_type=F32)
    b_gate = p2[:, :CONV_WIDTH]
    u = p2[:, CONV_WIDTH:2 * CONV_WIDTH] * p2[:, 2 * CONV_WIDTH:]

    @pl.when(step % tiles_per_seq == 0)
    def _():
        carry_ref[...] = jnp.zeros_like(carry_ref)

    prev1 = carry_ref[7:8, :]
    prev2 = carry_ref[6:7, :]
    row = lax.broadcasted_iota(I32, (tm, CONV_WIDTH), 0)
    u1 = jnp.where(row == 0, prev1, pltpu.roll(u, 1, 0))
    u2 = jnp.where(row == 0, prev2, jnp.where(row == 1, prev1, pltpu.roll(u, 2, 0)))
    carry_ref[...] = u[tm - 8:, :]
    cw = conv_w_ref[...]
    conv = b_gate * (cw[0:1, :] * u2 + cw[1:2, :] * u1 + cw[2:3, :] * u)
    branch_b = jnp.dot(conv.astype(BF16), wbb_ref[...], preferred_element_type=F32)

    gates = jax.nn.sigmoid(jnp.dot(h, w3_ref[...], preferred_element_type=F32) + b_gate_ref[...])
    ga_ref[...] = gates[:, :D_MODEL].astype(BF16)
    mixb_ref[...] = (gates[:, D_MODEL:] * branch_b).astype(BF16)


def _in_proj(x2d, pos2d, invf, g_mix, w1, w2, w3, b_gate, g_cq, wqa, wqb, g_ckv, wkk, wkv,
             conv_w, wbb, seq):
    n = x2d.shape[0]
    row_spec = lambda c: pl.BlockSpec((TM, c), lambda i: (i, 0))
    consts = [invf, g_mix, w1, w2, w3, b_gate, g_cq, wqa, wqb, g_ckv, wkk, wkv, conv_w, wbb]
    kv_cols = MLA_HEADS * HEAD_SLOT
    return pl.pallas_call(
        functools.partial(_in_proj_kernel, tiles_per_seq=seq // TM),
        grid=(n // TM,),
        in_specs=[row_spec(D_MODEL), row_spec(1)] + [_const_spec(c.shape) for c in consts],
        out_specs=[row_spec(kv_cols), row_spec(kv_cols), row_spec(kv_cols),
                   row_spec(D_MODEL), row_spec(D_MODEL)],
        out_shape=[jax.ShapeDtypeStruct((n, kv_cols), BF16),
                   jax.ShapeDtypeStruct((n, kv_cols), BF16),
                   jax.ShapeDtypeStruct((n, kv_cols), BF16),
                   jax.ShapeDtypeStruct((n, D_MODEL), BF16),
                   jax.ShapeDtypeStruct((n, D_MODEL), BF16)],
        scratch_shapes=[pltpu.VMEM((8, CONV_WIDTH), F32)],
        compiler_params=_params(("arbitrary",)),
        name="in_proj",
    )(x2d, pos2d, *consts)


def _attn_kernel(q_ref, k_ref, v_ref, o_ref):
    qi = pl.program_id(1)
    tq = q_ref.shape[0]
    row = qi * tq + lax.broadcasted_iota(I32, (tq, TK), 0)
    col0 = lax.broadcasted_iota(I32, (tq, TK), 1)

    def step(j, carry, masked, heads):
        start = pl.multiple_of(j * TK, TK)
        out = []
        for c, hd in enumerate(heads):
            m, acc = carry[c]
            hsl = slice(hd * HEAD_SLOT, (hd + 1) * HEAD_SLOT)
            kb = k_ref[pl.ds(start, TK), hsl]
            vb = v_ref[pl.ds(start, TK), hsl]
            s = lax.dot_general(q_ref[:, hsl], kb, (((1,), (1,)), ((), ())),
                                preferred_element_type=F32)
            if masked:
                s = jnp.where(col0 + j * TK <= row, s, NEG_BIG)
            m_new = jnp.maximum(m, jnp.max(s, axis=1, keepdims=True))
            p = jnp.exp2(s - m_new).astype(BF16)
            acc = jnp.exp2(m - m_new) * acc + jnp.dot(p, vb, preferred_element_type=F32)
            out.append((m_new, acc))
        return tuple(out)

    n_full = (qi * tq) // TK
    n_masked = -(-tq // TK)
    lane = lax.broadcasted_iota(I32, (tq, HEAD_SLOT), 1)
    for g in range(MLA_HEADS // ATTN_CHAINS):
        heads = tuple(range(g * ATTN_CHAINS, (g + 1) * ATTN_CHAINS))
        carry = tuple((jnp.full((tq, 1), NEG_BIG, F32), jnp.zeros((tq, HEAD_SLOT), F32))
                      for _ in heads)
        carry = lax.fori_loop(0, n_full, functools.partial(step, masked=False, heads=heads), carry)
        for jj in range(n_masked):
            carry = step(n_full + jj, carry, True, heads)
        outs = []
        for _, acc in carry:
            denom = jnp.sum(jnp.where(lane == L_LANE, acc, 0.0), axis=1, keepdims=True)
            outs.append(acc / denom)
        for c in range(0, ATTN_CHAINS, 2):
            hp = heads[c] // 2
            pair = jnp.where(lane < V_HEAD_DIM, outs[c], pltpu.roll(outs[c + 1], V_HEAD_DIM, 1))
            o_ref[:, hp * HEAD_SLOT:(hp + 1) * HEAD_SLOT] = pair.astype(BF16)


def _mla_attn(q, k, v, batch, seq):
    n = q.shape[0]
    nq = seq // TQ
    out_cols = MLA_HEADS * V_HEAD_DIM
    return pl.pallas_call(
        _attn_kernel,
        grid=(batch, nq),
        in_specs=[pl.BlockSpec((TQ, q.shape[1]), lambda b, i: (b * nq + i, 0)),
                  pl.BlockSpec((seq, k.shape[1]), lambda b, i: (b, 0)),
                  pl.BlockSpec((seq, v.shape[1]), lambda b, i: (b, 0))],
        out_specs=pl.BlockSpec((TQ, out_cols), lambda b, i: (b * nq + i, 0)),
        out_shape=jax.ShapeDtypeStruct((n, out_cols), BF16),
        compiler_params=_params(("arbitrary", "arbitrary")),
        name="mla_attn",
    )(q, k, v)


def _post_attn_kernel(x_ref, attn_ref, ga_ref, mixb_ref, wba_ref, wout_ref, g_x_ref, wq_ref,
                      km_ref, vm_ref, wo_ref, g_moe_ref, wr_ref, br_ref,
                      x2_ref, h3_ref, meta_ref, wts_ref, cnt_ref, run_ref):
    step = pl.program_id(0)
    tm = x_ref.shape[0]

    @pl.when(step == 0)
    def _():
        run_ref[...] = jnp.zeros_like(run_ref)

    a = jnp.dot(attn_ref[...], wba_ref[...], preferred_element_type=F32)
    mixed = ga_ref[...].astype(F32) * a + mixb_ref[...].astype(F32)
    x1 = x_ref[...] + jnp.dot(mixed.astype(BF16), wout_ref[...], preferred_element_type=F32)

    h2 = _rms(x1, g_x_ref[...]).astype(BF16)
    qm = jnp.dot(h2, wq_ref[...], preferred_element_type=F32) * (1.0 / math.sqrt(XATTN_HEAD_DIM))
    qm = qm.astype(BF16)
    heads = []
    for hd in range(XATTN_HEADS):
        sl = slice(hd * XATTN_HEAD_DIM, (hd + 1) * XATTN_HEAD_DIM)
        s = lax.dot_general(qm[:, sl], km_ref[:, sl], (((1,), (1,)), ((), ())),
                            preferred_element_type=F32)
        p = jnp.exp(s - jnp.max(s, axis=1, keepdims=True))
        o = jnp.dot(p.astype(BF16), vm_ref[:, sl], preferred_element_type=F32)
        heads.append((o / jnp.sum(p, axis=1, keepdims=True)).astype(BF16))
    o_all = jnp.concatenate(heads, axis=1)
    x2 = x1 + jnp.dot(o_all, wo_ref[...], preferred_element_type=F32)
    x2_ref[...] = x2

    h3 = _rms(x2, g_moe_ref[...])
    h3_ref[...] = h3
    logits = jnp.dot(h3.astype(BF16), wr_ref[...], preferred_element_type=F32) + br_ref[...]
    lane = lax.broadcasted_iota(I32, (tm, LANES), 1)
    lane_f = lane.astype(F32)

    def first_lane(hit):
        return jnp.min(jnp.where(hit, lane_f, float(LANES)), axis=1, keepdims=True).astype(I32)

    gl = jnp.where(lane < N_GROUPS, logits, -jnp.inf)
    gmax = jnp.max(gl, axis=1, keepdims=True)
    g_idx = first_lane(gl == gmax)
    p_group = 1.0 / jnp.sum(jnp.exp(gl - gmax), axis=1, keepdims=True)
    e_lo = ROUTE_LANE0 + g_idx * EXPERTS_PER_GROUP
    el = jnp.where((lane >= e_lo) & (lane < e_lo + EXPERTS_PER_GROUP), logits, -jnp.inf)
    v1 = jnp.max(el, axis=1, keepdims=True)
    i1 = first_lane(el == v1)
    el2 = jnp.where(lane == i1, -jnp.inf, el)
    v2 = jnp.max(el2, axis=1, keepdims=True)
    i2 = first_lane(el2 == v2)
    t = jnp.exp(v2 - v1)
    w0 = p_group / (1.0 + t)
    w1 = p_group * t / (1.0 + t)

    oh0 = lane == i1
    oh1 = lane == i2
    oh = (oh0 | oh1).astype(BF16)
    r_i = lax.broadcasted_iota(I32, (tm, tm), 0)
    c_i = lax.broadcasted_iota(I32, (tm, tm), 1)
    lower = (c_i < r_i).astype(BF16)
    before = jnp.dot(lower, oh, preferred_element_type=F32) + run_ref[...]
    rank0 = jnp.sum(jnp.where(oh0, before, 0.0), axis=1, keepdims=True).astype(I32)
    rank1 = jnp.sum(jnp.where(oh1, before, 0.0), axis=1, keepdims=True).astype(I32)
    run_ref[...] = run_ref[...] + jnp.sum(oh.astype(F32), axis=0, keepdims=True)
    cnt_ref[...] = jnp.broadcast_to(run_ref[...], cnt_ref.shape).astype(I32)

    meta = jnp.where(lane == 0, i1 - ROUTE_LANE0, 0)
    meta = jnp.where(lane == 1, i2 - ROUTE_LANE0, meta)
    meta = jnp.where(lane == 2, rank0, meta)
    meta = jnp.where(lane == 3, rank1, meta)
    meta_ref[...] = meta
    wts_ref[...] = jnp.where(lane == 0, w0, jnp.where(lane == 1, w1, 0.0))


def _post_attn(x2d, attn, ga, mixb, wba, wout, g_x, wq, km, vm, wo, g_moe, wr, br, seq):
    n = x2d.shape[0]
    tps = seq // TM
    row_spec = lambda c: pl.BlockSpec((TM, c), lambda i: (i, 0))
    mem_spec = pl.BlockSpec((MEM_LEN, XATTN_HEADS * XATTN_HEAD_DIM), lambda i: (i // tps, 0))
    return pl.pallas_call(
        _post_attn_kernel,
        grid=(n // TM,),
        in_specs=[row_spec(D_MODEL), row_spec(MLA_HEADS * V_HEAD_DIM), row_spec(D_MODEL),
                  row_spec(D_MODEL), _const_spec(wba.shape), _const_spec(wout.shape),
                  _const_spec(g_x.shape), _const_spec(wq.shape), mem_spec, mem_spec,
                  _const_spec(wo.shape), _const_spec(g_moe.shape), _const_spec(wr.shape),
                  _const_spec(br.shape)],
        out_specs=[row_spec(D_MODEL), row_spec(D_MODEL), row_spec(LANES), row_spec(LANES),
                   _const_spec((8, LANES))],
        out_shape=[jax.ShapeDtypeStruct((n, D_MODEL), F32),
                   jax.ShapeDtypeStruct((n, D_MODEL), F32),
                   jax.ShapeDtypeStruct((n, LANES), I32),
                   jax.ShapeDtypeStruct((n, LANES), F32),
                   jax.ShapeDtypeStruct((8, LANES), I32)],
        scratch_shapes=[pltpu.VMEM((1, LANES), F32)],
        compiler_params=_params(("arbitrary",)),
        name="post_attn",
    )(x2d, attn, ga, mixb, wba, wout, g_x, wq, km, vm, wo, g_moe, wr, br)


def _row_copy(src_ref, src_row, dst_ref, dst_row, sem):
    return pltpu.make_async_copy(src_ref.at[pl.ds(src_row, 1), :], dst_ref.at[pl.ds(dst_row, 1), :],
                                 sem)


def _dispatch_kernel(fill_ref, dest_ref, zeros_ref, h3_ref, xd_ref, row_sem, fill_sem):
    i = pl.program_id(0)

    def fill_copy(b):
        return pltpu.make_async_copy(
            zeros_ref, xd_ref.at[pl.ds(pl.multiple_of(b * SLOT_BLOCK, SLOT_BLOCK), SLOT_BLOCK)],
            fill_sem)

    @pl.when(i == 0)
    def _():
        def start(b, c):
            @pl.when(fill_ref[b] > 0)
            def _():
                fill_copy(b).start()
            return c

        def wait(b, c):
            @pl.when(fill_ref[b] > 0)
            def _():
                fill_copy(b).wait()
            return c

        lax.fori_loop(0, fill_ref.shape[0], start, 0)
        lax.fori_loop(0, fill_ref.shape[0], wait, 0)

    def copies(r):
        return (_row_copy(h3_ref, r, xd_ref, dest_ref[0, 0, 2 * r], row_sem),
                _row_copy(h3_ref, r, xd_ref, dest_ref[0, 0, 2 * r + 1], row_sem))

    def issue(r, c):
        for cp in copies(r):
            cp.start()
        return c

    def drain(r, c):
        for cp in copies(r):
            cp.wait()
        return c

    lax.fori_loop(0, TROW, issue, 0, unroll=DMA_UNROLL)
    lax.fori_loop(0, TROW, drain, 0, unroll=DMA_UNROLL)


def _dispatch(fill_blk, dest3, h3, n_slots):
    n, d = h3.shape
    grid_spec = pltpu.PrefetchScalarGridSpec(
        num_scalar_prefetch=1,
        grid=(n // TROW,),
        in_specs=[pl.BlockSpec((1, 1, 2 * TROW), lambda i, fb: (i, 0, 0), memory_space=pltpu.SMEM),
                  pl.BlockSpec((SLOT_BLOCK, d), lambda i, fb: (0, 0)),
                  pl.BlockSpec((TROW, d), lambda i, fb: (i, 0))],
        out_specs=pl.BlockSpec(memory_space=pl.ANY),
        scratch_shapes=[pltpu.SemaphoreType.DMA, pltpu.SemaphoreType.DMA],
    )
    return pl.pallas_call(
        _dispatch_kernel,
        grid_spec=grid_spec,
        out_shape=jax.ShapeDtypeStruct((n_slots, d), F32),
        compiler_params=_params(("arbitrary",)),
        name="dispatch",
    )(fill_blk, dest3, jnp.zeros((SLOT_BLOCK, d), F32), h3)


def _experts_kernel(be_ref, nb_ref, nxt_ref, xd_ref, wgu_hbm, wdn_hbm, yd_ref,
                    wgu_f32, wdn_f32, wgu_bf, wdn_bf, slot_ref, sem):
    i = pl.program_id(0)

    def weight_copies(e, slot):
        return (pltpu.make_async_copy(wgu_hbm.at[e], wgu_f32.at[slot], sem.at[slot]),
                pltpu.make_async_copy(wdn_hbm.at[e], wdn_f32.at[slot], sem.at[slot]))

    @pl.when(i == 0)
    def _():
        slot_ref[0] = 0
        for cp in weight_copies(be_ref[0], 0):
            cp.start()

    @pl.when(i < nb_ref[0])
    def _():
        @pl.when((i == 0) | (be_ref[i] != be_ref[jnp.maximum(i - 1, 0)]))
        def _():
            slot = slot_ref[0]
            for cp in weight_copies(be_ref[i], slot):
                cp.wait()
            wgu_bf[...] = wgu_f32[slot].astype(BF16)
            wdn_bf[...] = wdn_f32[slot].astype(BF16)

            @pl.when(nxt_ref[i] >= 0)
            def _():
                for cp in weight_copies(nxt_ref[i], 1 - slot):
                    cp.start()

            slot_ref[0] = 1 - slot

        gu = jnp.dot(xd_ref[...].astype(BF16), wgu_bf[...], preferred_element_type=F32)
        gate = gu[:, :EXPERT_FF]
        up = gu[:, EXPERT_FF:]
        act = (gate * jax.nn.sigmoid(gate) * up).astype(BF16)
        yd_ref[...] = jnp.dot(act, wdn_bf[...], preferred_element_type=F32)


def _experts(block_expert, n_blocks_used, next_expert, xd, w_gate_up, w_down):
    n_slots, d = xd.shape
    slot_map = lambda i, be, nb, nx: (jnp.minimum(i, nb[0] - 1), 0)
    grid_spec = pltpu.PrefetchScalarGridSpec(
        num_scalar_prefetch=3,
        grid=(n_slots // SLOT_BLOCK,),
        in_specs=[pl.BlockSpec((SLOT_BLOCK, d), slot_map),
                  pl.BlockSpec(memory_space=pl.ANY),
                  pl.BlockSpec(memory_space=pl.ANY)],
        out_specs=pl.BlockSpec((SLOT_BLOCK, d), slot_map),
        scratch_shapes=[pltpu.VMEM((2, D_MODEL, 2 * EXPERT_FF), F32),
                        pltpu.VMEM((2, EXPERT_FF, D_MODEL), F32),
                        pltpu.VMEM((D_MODEL, 2 * EXPERT_FF), BF16),
                        pltpu.VMEM((EXPERT_FF, D_MODEL), BF16),
                        pltpu.SMEM((1,), I32),
                        pltpu.SemaphoreType.DMA((2,))],
    )
    return pl.pallas_call(
        _experts_kernel,
        grid_spec=grid_spec,
        out_shape=jax.ShapeDtypeStruct(xd.shape, F32),
        input_output_aliases={3: 0},
        compiler_params=_params(("arbitrary",)),
        name="experts",
    )(block_expert, n_blocks_used, next_expert, xd, w_gate_up, w_down)


def _combine_kernel(dest_ref, dest_next_ref, x2_ref, wts_ref, g_ref, yd_ref, o_ref, buf, sem):
    i = pl.program_id(0)
    last = pl.num_programs(0) - 1
    rows = x2_ref.shape[0]
    slot = i % 2

    def copies(d_ref, slt, r):
        return (_row_copy(yd_ref, d_ref[0, 0, 2 * r], buf.at[slt, 0], r, sem.at[slt]),
                _row_copy(yd_ref, d_ref[0, 0, 2 * r + 1], buf.at[slt, 1], r, sem.at[slt]))

    def issue(d_ref, slt):
        def body(r, c):
            for cp in copies(d_ref, slt, r):
                cp.start()
            return c
        lax.fori_loop(0, rows, body, 0, unroll=DMA_UNROLL)

    @pl.when(i == 0)
    def _():
        issue(dest_ref, 0)

    @pl.when(i < last)
    def _():
        issue(dest_next_ref, 1 - slot)

    def drain(r, c):
        for cp in copies(dest_ref, slot, r):
            cp.wait()
        return c

    lax.fori_loop(0, rows, drain, 0, unroll=DMA_UNROLL)

    lane = lax.broadcasted_iota(I32, (rows, LANES), 1)
    wts = wts_ref[...]
    w0 = jnp.sum(jnp.where(lane == 0, wts, 0.0), axis=1, keepdims=True)
    w1 = jnp.sum(jnp.where(lane == 1, wts, 0.0), axis=1, keepdims=True)
    y = buf[slot, 0] * w0 + buf[slot, 1] * w1
    o_ref[...] = _rms(x2_ref[...] + y, g_ref[...])


def _combine(dest3, x2, wts, g_final, yd):
    n = x2.shape[0]
    steps = n // TROW
    dest_spec = lambda f: pl.BlockSpec((1, 1, 2 * TROW), f, memory_space=pltpu.SMEM)
    return pl.pallas_call(
        _combine_kernel,
        grid=(steps,),
        in_specs=[dest_spec(lambda i: (i, 0, 0)),
                  dest_spec(lambda i: (jnp.minimum(i + 1, steps - 1), 0, 0)),
                  pl.BlockSpec((TROW, D_MODEL), lambda i: (i, 0)),
                  pl.BlockSpec((TROW, LANES), lambda i: (i, 0)),
                  _const_spec((1, D_MODEL)),
                  pl.BlockSpec(memory_space=pl.ANY)],
        out_specs=pl.BlockSpec((TROW, D_MODEL), lambda i: (i, 0)),
        out_shape=jax.ShapeDtypeStruct((n, D_MODEL), F32),
        scratch_shapes=[pltpu.VMEM((2, 2, TROW, D_MODEL), F32),
                        pltpu.SemaphoreType.DMA((2,))],
        compiler_params=_params(("arbitrary",)),
        name="combine",
    )(dest3, dest3, x2, wts, g_final, yd)


def _head_slots(w, per_head, pieces):
    rows = w.shape[0]
    wh = w.reshape(rows, MLA_HEADS, per_head)
    out = jnp.zeros((rows, MLA_HEADS, HEAD_SLOT), w.dtype)
    for src_lo, src_hi, dst_lo, sign in pieces:
        out = out.at[:, :, dst_lo:dst_lo + (src_hi - src_lo)].set(sign * wh[:, :, src_lo:src_hi])
    return out.reshape(rows, MLA_HEADS * HEAD_SLOT)


def _layer(x2d, pos2d, invf, km, vm, batch, seq, g_mix, w_in, b_gate, g_cq, w_uq, g_ckv, w_ukv,
           conv_w, w_branch_a, w_branch_b, w_out, g_xattn, w_q_mem, w_o_mem, g_moe,
           w_router_group, b_router_group, w_router_expert, b_router_expert, w_gate_up, w_down,
           g_final):
    n = x2d.shape[0]
    row = lambda v: v.reshape(1, -1)
    lo, mid, hi = ROPE_LO, ROPE_LO + ROPE_HALF, ROPE_LO + QK_ROPE_DIM

    w_kr = w_in[:, COL_CKV:COL_KROPE]
    kr_a = jnp.zeros((D_MODEL, HEAD_SLOT), F32).at[:, lo:hi].set(w_kr)
    kr_b = (jnp.zeros((D_MODEL, HEAD_SLOT), F32)
            .at[:, lo:mid].set(-w_kr[:, ROPE_HALF:]).at[:, mid:hi].set(w_kr[:, :ROPE_HALF]))
    w1 = jnp.concatenate([w_in[:, :COL_CKV], kr_a, kr_b], axis=1).astype(BF16)
    w2 = w_in[:, COL_KROPE:COL_CONV].astype(BF16)
    w3 = w_in[:, COL_CONV:].astype(BF16)
    qd = QK_NOPE_DIM + QK_ROPE_DIM
    wqa = _head_slots(w_uq, qd, [(0, qd, 0, 1.0)]).astype(BF16)
    wqb = _head_slots(w_uq, qd, [(mid, hi, lo, -1.0), (lo, mid, mid, 1.0)]).astype(BF16)
    kvd = QK_NOPE_DIM + V_HEAD_DIM
    wkk = _head_slots(w_ukv, kvd, [(0, QK_NOPE_DIM, 0, 1.0)]).astype(BF16)
    wkv = _head_slots(w_ukv, kvd, [(QK_NOPE_DIM, kvd, 0, 1.0)]).astype(BF16)
    conv_w8 = jnp.zeros((8, CONV_WIDTH), F32).at[:conv_w.shape[0]].set(conv_w)

    q, k, v, ga, mixb = _in_proj(x2d, pos2d, invf, row(g_mix), w1, w2, w3, row(b_gate), row(g_cq),
                                 wqa, wqb, row(g_ckv), wkk, wkv, conv_w8,
                                 w_branch_b.astype(BF16), seq)
    attn = _mla_attn(q, k, v, batch, seq)

    wr = (jnp.zeros((D_MODEL, LANES), F32).at[:, :N_GROUPS].set(w_router_group)
          .at[:, ROUTE_LANE0:ROUTE_LANE0 + N_EXPERTS].set(w_router_expert).astype(BF16))
    br = (jnp.zeros((1, LANES), F32).at[0, :N_GROUPS].set(b_router_group)
          .at[0, ROUTE_LANE0:ROUTE_LANE0 + N_EXPERTS].set(b_router_expert))
    x2, h3, meta, wts, cnt = _post_attn(x2d, attn, ga, mixb, w_branch_a.astype(BF16),
                                   w_out.astype(BF16), row(g_xattn), w_q_mem.astype(BF16), km, vm,
                                   w_o_mem.astype(BF16), row(g_moe), wr, br, seq)

    counts = cnt[0, ROUTE_LANE0:ROUTE_LANE0 + N_EXPERTS]
    padded = (counts + SLOT_BLOCK - 1) // SLOT_BLOCK * SLOT_BLOCK
    pad_end = jnp.cumsum(padded)
    pad_start = pad_end - padded
    n_slots = (2 * n + SLOT_BLOCK - 1) // SLOT_BLOCK * SLOT_BLOCK + N_EXPERTS * SLOT_BLOCK
    expert_ids = meta[:, 0:2]
    onehot = expert_ids[:, :, None] == jnp.arange(N_EXPERTS, dtype=I32)[None, None, :]
    dest = jnp.sum(jnp.where(onehot, pad_start[None, None, :], 0), axis=-1) + meta[:, 2:4]
    dest3 = dest.astype(I32).reshape(n // TROW, 1, 2 * TROW)
    block_lo = jnp.arange(n_slots // SLOT_BLOCK, dtype=I32) * SLOT_BLOCK
    block_expert = jnp.minimum(jnp.sum(pad_end[None, :] <= block_lo[:, None], axis=1),
                               N_EXPERTS - 1).astype(I32)
    n_blocks_used = (pad_end[-1:] // SLOT_BLOCK).astype(I32)
    fill_blk = ((block_lo + SLOT_BLOCK == pad_end[block_expert])
                | (block_lo >= pad_end[-1])).astype(I32)

    cand = jnp.where(padded > 0, jnp.arange(N_EXPERTS, dtype=I32), N_EXPERTS)
    later = jnp.concatenate([lax.cummin(cand, reverse=True)[1:],
                             jnp.full((1,), N_EXPERTS, I32)])
    next_expert = jnp.where(later < N_EXPERTS, later, -1)[block_expert].astype(I32)

    xd = _dispatch(fill_blk, dest3, h3, n_slots)
    yd = _experts(block_expert, n_blocks_used, next_expert, xd, w_gate_up, w_down)
    return _combine(dest3, x2, wts, row(g_final), yd)


def kernel(x, mem, positions, g_mix, w_in, b_gate, g_cq, w_uq, g_ckv, w_ukv, conv_w, w_branch_a,
           w_branch_b, w_out, g_xattn, g_mem, w_q_mem, w_kv_mem, w_o_mem, g_moe, w_router_group,
           b_router_group, w_router_expert, b_router_expert, w_gate_up, w_down, g_final):
    batch, seq, d = x.shape
    depth = g_mix.shape[0]
    assert d == D_MODEL and depth == 1 and seq % TM == 0 and seq % TQ == 0
    n = batch * seq
    x2d = x.reshape(n, d)
    pos2d = positions.reshape(n, 1).astype(I32)
    inv_freq = ROPE_THETA ** (-jnp.arange(0, QK_ROPE_DIM, 2, dtype=F32) / QK_ROPE_DIM)
    invf = (jnp.zeros((1, HEAD_SLOT), F32).at[0, ROPE_LO:ROPE_LO + ROPE_HALF].set(inv_freq)
            .at[0, ROPE_LO + ROPE_HALF:ROPE_LO + QK_ROPE_DIM].set(inv_freq))
    l = 0
    km, vm = _mem_kv(mem.reshape(batch * MEM_LEN, d), g_mem[l].reshape(1, -1),
                     w_kv_mem[l].astype(BF16))
    out = _layer(x2d, pos2d, invf, km, vm, batch, seq, g_mix[l], w_in[l], b_gate[l], g_cq[l],
                 w_uq[l], g_ckv[l], w_ukv[l], conv_w[l], w_branch_a[l], w_branch_b[l], w_out[l],
                 g_xattn[l], w_q_mem[l], w_o_mem[l], g_moe[l], w_router_group[l],
                 b_router_group[l], w_router_expert[l], b_router_expert[l], w_gate_up[l],
                 w_down[l], g_final)
    return out.reshape(batch, seq, d)
```

```python
import functools
import math

import jax
import jax.numpy as jnp
from jax import lax
from jax.experimental import pallas as pl
from jax.experimental.pallas import tpu as pltpu

F32 = jnp.float32
BF16 = jnp.bfloat16
I32 = jnp.int32

D_MODEL = 1024
MEM_LEN = 256
MLA_HEADS = 8
QK_NOPE_DIM = 64
QK_ROPE_DIM = 32
V_HEAD_DIM = 64
Q_LORA_RANK = 256
KV_LORA_RANK = 128
ROPE_THETA = 10000.0
CONV_WIDTH = 512
XATTN_HEADS = 4
XATTN_HEAD_DIM = 128
N_GROUPS = 8
EXPERTS_PER_GROUP = 8
N_EXPERTS = N_GROUPS * EXPERTS_PER_GROUP
EXPERT_FF = 256
EPS = 1e-6

COL_CQ = Q_LORA_RANK
COL_CKV = COL_CQ + KV_LORA_RANK
COL_KROPE = COL_CKV + QK_ROPE_DIM
COL_CONV = COL_KROPE + 3 * CONV_WIDTH

LANES = 128
HEAD_SLOT = LANES
ROPE_LO = QK_NOPE_DIM
ROPE_HALF = QK_ROPE_DIM // 2
L_LANE = V_HEAD_DIM
SLOT_BLOCK = 256
ROUTE_LANE0 = N_GROUPS
NEG_BIG = -1e30

TM = 512
TQ = 512
TK = 512
DIAG_STRIPS = 2
ATTN_CHAINS = 8
TROW = 256
ROW_GROUP = 8
VMEM_LIMIT = 56 * 1024 * 1024


def _rms(x, g):
    return x * lax.rsqrt(jnp.mean(x * x, axis=-1, keepdims=True) + EPS) * g


def _const_spec(shape):
    nd = len(shape)
    return pl.BlockSpec(shape, lambda *_: (0,) * nd)


def _params(sem):
    return pltpu.CompilerParams(dimension_semantics=sem, vmem_limit_bytes=VMEM_LIMIT)


def _mem_kv_kernel(mem_ref, g_ref, w_ref, k_ref, v_ref):
    h = _rms(mem_ref[...], g_ref[...]).astype(BF16)
    kv = jnp.dot(h, w_ref[...], preferred_element_type=F32)
    half = XATTN_HEADS * XATTN_HEAD_DIM
    k_ref[...] = kv[:, :half].astype(BF16)
    v_ref[...] = kv[:, half:].astype(BF16)


def _mem_kv(mem2d, g_mem, w_kv):
    rows = mem2d.shape[0]
    half = XATTN_HEADS * XATTN_HEAD_DIM
    return pl.pallas_call(
        _mem_kv_kernel,
        grid=(rows // MEM_LEN,),
        in_specs=[pl.BlockSpec((MEM_LEN, D_MODEL), lambda i: (i, 0)),
                  _const_spec((1, D_MODEL)),
                  _const_spec((D_MODEL, 2 * half))],
        out_specs=[pl.BlockSpec((MEM_LEN, half), lambda i: (i, 0)),
                   pl.BlockSpec((MEM_LEN, half), lambda i: (i, 0))],
        out_shape=[jax.ShapeDtypeStruct((rows, half), BF16)] * 2,
        compiler_params=_params(("arbitrary",)),
        name="mem_kv",
    )(mem2d, g_mem, w_kv)


def _in_proj_kernel(x_ref, pos_ref, invf_ref, g_mix_ref, w1_ref, w2_ref, w3_ref, b_gate_ref,
                    g_cq_ref, wqa_ref, wqb_ref, g_ckv_ref, wkk_ref, wkv_ref, conv_w_ref, wbb_ref,
                    q_ref, k_ref, v_ref, ga_ref, mixb_ref, carry_ref, *, tiles_per_seq):
    step = pl.program_id(0)
    tm = x_ref.shape[0]
    h = _rms(x_ref[...], g_mix_ref[...]).astype(BF16)

    lane = lax.broadcasted_iota(I32, (tm, HEAD_SLOT), 1)
    rope_lane = (lane >= ROPE_LO) & (lane < ROPE_LO + QK_ROPE_DIM)
    ang = pos_ref[...].astype(F32) * invf_ref[...]
    cos_t = jnp.where(rope_lane, jnp.cos(ang), 0.0)
    sin_t = jnp.where(rope_lane, jnp.sin(ang), 0.0)

    p1 = jnp.dot(h, w1_ref[...], preferred_element_type=F32)
    c_q = p1[:, :COL_CQ]
    c_kv = p1[:, COL_CQ:COL_CKV]
    kr_a = p1[:, COL_CKV:COL_CKV + HEAD_SLOT]
    kr_b = p1[:, COL_CKV + HEAD_SLOT:COL_CKV + 2 * HEAD_SLOT]
    k_rot = kr_a * cos_t + kr_b * sin_t

    scale = math.log2(math.e) / math.sqrt(QK_NOPE_DIM + QK_ROPE_DIM)
    q_cos = jnp.where(lane < ROPE_LO, scale, cos_t * scale)
    q_sin = sin_t * scale
    cqn = _rms(c_q, g_cq_ref[...]).astype(BF16)
    qa = jnp.dot(cqn, wqa_ref[...], preferred_element_type=F32)
    qb = jnp.dot(cqn, wqb_ref[...], preferred_element_type=F32)
    ckvn = _rms(c_kv, g_ckv_ref[...]).astype(BF16)
    kk = jnp.dot(ckvn, wkk_ref[...], preferred_element_type=F32)
    vv = jnp.dot(ckvn, wkv_ref[...], preferred_element_type=F32)
    ones_lane = jnp.where(lane == L_LANE, 1.0, 0.0)
    for hd in range(MLA_HEADS):
        sl = slice(hd * HEAD_SLOT, (hd + 1) * HEAD_SLOT)
        q_ref[:, sl] = (qa[:, sl] * q_cos + qb[:, sl] * q_sin).astype(BF16)
        k_ref[:, sl] = (kk[:, sl] + k_rot).astype(BF16)
        v_ref[:, sl] = (vv[:, sl] + ones_lane).astype(BF16)

    p2 = jnp.dot(h, w2_ref[...], preferred_element_type=F32)
    b_gate = p2[:, :CONV_WIDTH]
    u = p2[:, CONV_WIDTH:2 * CONV_WIDTH] * p2[:, 2 * CONV_WIDTH:]

    @pl.when(step % tiles_per_seq == 0)
    def _():
        carry_ref[...] = jnp.zeros_like(carry_ref)

    prev1 = carry_ref[7:8, :]
    prev2 = carry_ref[6:7, :]
    row = lax.broadcasted_iota(I32, (tm, CONV_WIDTH), 0)
    u1 = jnp.where(row == 0, prev1, pltpu.roll(u, 1, 0))
    u2 = jnp.where(row == 0, prev2, jnp.where(row == 1, prev1, pltpu.roll(u, 2, 0)))
    carry_ref[...] = u[tm - 8:, :]
    cw = conv_w_ref[...]
    conv = b_gate * (cw[0:1, :] * u2 + cw[1:2, :] * u1 + cw[2:3, :] * u)
    branch_b = jnp.dot(conv.astype(BF16), wbb_ref[...], preferred_element_type=F32)

    gates = jax.nn.sigmoid(jnp.dot(h, w3_ref[...], preferred_element_type=F32) + b_gate_ref[...])
    ga_ref[...] = gates[:, :D_MODEL].astype(BF16)
    mixb_ref[...] = (gates[:, D_MODEL:] * branch_b).astype(BF16)


def _in_proj(x2d, pos2d, invf, g_mix, w1, w2, w3, b_gate, g_cq, wqa, wqb, g_ckv, wkk, wkv,
             conv_w, wbb, seq):
    n = x2d.shape[0]
    row_spec = lambda c: pl.BlockSpec((TM, c), lambda i: (i, 0))
    consts = [invf, g_mix, w1, w2, w3, b_gate, g_cq, wqa, wqb, g_ckv, wkk, wkv, conv_w, wbb]
    kv_cols = MLA_HEADS * HEAD_SLOT
    return pl.pallas_call(
        functools.partial(_in_proj_kernel, tiles_per_seq=seq // TM),
        grid=(n // TM,),
        in_specs=[row_spec(D_MODEL), row_spec(1)] + [_const_spec(c.shape) for c in consts],
        out_specs=[row_spec(kv_cols), row_spec(kv_cols), row_spec(kv_cols),
                   row_spec(D_MODEL), row_spec(D_MODEL)],
        out_shape=[jax.ShapeDtypeStruct((n, kv_cols), BF16),
                   jax.ShapeDtypeStruct((n, kv_cols), BF16),
                   jax.ShapeDtypeStruct((n, kv_cols), BF16),
                   jax.ShapeDtypeStruct((n, D_MODEL), BF16),
                   jax.ShapeDtypeStruct((n, D_MODEL), BF16)],
        scratch_shapes=[pltpu.VMEM((8, CONV_WIDTH), F32)],
        compiler_params=_params(("arbitrary",)),
        name="in_proj",
    )(x2d, pos2d, *consts)


def _attn_kernel(q_ref, k_ref, v_ref, o_ref):
    qi = pl.program_id(1)
    tq = q_ref.shape[0]

    def block(carry, heads, col_start, ncols, row_lo, masked):
        out = []
        for c, hd in enumerate(heads):
            m, acc = carry[c]
            hsl = slice(hd * HEAD_SLOT, (hd + 1) * HEAD_SLOT)
            kb = k_ref[pl.ds(col_start, ncols), hsl]
            vb = v_ref[pl.ds(col_start, ncols), hsl]
            s = lax.dot_general(q_ref[row_lo:, hsl], kb, (((1,), (1,)), ((), ())),
                                preferred_element_type=F32)
            if masked:
                r = qi * tq + row_lo + lax.broadcasted_iota(I32, s.shape, 0)
                s = jnp.where(col_start + lax.broadcasted_iota(I32, s.shape, 1) <= r, s, NEG_BIG)
            m_hi, acc_hi = m[row_lo:], acc[row_lo:]
            m_new = jnp.maximum(m_hi, jnp.max(s, axis=1, keepdims=True))
            p = jnp.exp2(s - m_new).astype(BF16)
            acc_new = jnp.exp2(m_hi - m_new) * acc_hi + jnp.dot(p, vb, preferred_element_type=F32)
            if row_lo:
                m_new = jnp.concatenate([m[:row_lo], m_new], axis=0)
                acc_new = jnp.concatenate([acc[:row_lo], acc_new], axis=0)
            out.append((m_new, acc_new))
        return tuple(out)

    def full_step(j, carry, heads):
        return block(carry, heads, pl.multiple_of(j * TK, TK), TK, 0, False)

    lane = lax.broadcasted_iota(I32, (tq, HEAD_SLOT), 1)
    for g in range(MLA_HEADS // ATTN_CHAINS):
        heads = tuple(range(g * ATTN_CHAINS, (g + 1) * ATTN_CHAINS))
        carry = tuple((jnp.full((tq, 1), NEG_BIG, F32), jnp.zeros((tq, HEAD_SLOT), F32))
                      for _ in heads)
        carry = lax.fori_loop(0, qi, functools.partial(full_step, heads=heads), carry)
        for d in range(DIAG_STRIPS):
            col = pl.multiple_of(qi * TK + d * (TK // DIAG_STRIPS), TK // DIAG_STRIPS)
            carry = block(carry, heads, col, TK // DIAG_STRIPS, d * (tq // DIAG_STRIPS), True)
        outs = []
        for _, acc in carry:
            denom = jnp.sum(jnp.where(lane == L_LANE, acc, 0.0), axis=1, keepdims=True)
            outs.append(acc / denom)
        for c in range(0, ATTN_CHAINS, 2):
            hp = heads[c] // 2
            pair = jnp.where(lane < V_HEAD_DIM, outs[c], pltpu.roll(outs[c + 1], V_HEAD_DIM, 1))
            o_ref[:, hp * HEAD_SLOT:(hp + 1) * HEAD_SLOT] = pair.astype(BF16)


def _mla_attn(q, k, v, batch, seq):
    n = q.shape[0]
    nq = seq // TQ
    out_cols = MLA_HEADS * V_HEAD_DIM
    return pl.pallas_call(
        _attn_kernel,
        grid=(batch, nq),
        in_specs=[pl.BlockSpec((TQ, q.shape[1]), lambda b, i: (b * nq + i, 0)),
                  pl.BlockSpec((seq, k.shape[1]), lambda b, i: (b, 0)),
                  pl.BlockSpec((seq, v.shape[1]), lambda b, i: (b, 0))],
        out_specs=pl.BlockSpec((TQ, out_cols), lambda b, i: (b * nq + i, 0)),
        out_shape=jax.ShapeDtypeStruct((n, out_cols), BF16),
        compiler_params=_params(("arbitrary", "arbitrary")),
        name="mla_attn",
    )(q, k, v)


def _post_attn_kernel(x_ref, attn_ref, ga_ref, mixb_ref, wba_ref, wout_ref, g_x_ref, wq_ref,
                      km_ref, vm_ref, wo_ref, g_moe_ref, wr_ref, br_ref,
                      x2_ref, h3_ref, meta_ref, wts_ref, cnt_ref, run_ref):
    step = pl.program_id(0)
    tm = x_ref.shape[0]

    @pl.when(step == 0)
    def _():
        run_ref[...] = jnp.zeros_like(run_ref)

    a = jnp.dot(attn_ref[...], wba_ref[...], preferred_element_type=F32)
    mixed = ga_ref[...].astype(F32) * a + mixb_ref[...].astype(F32)
    x1 = x_ref[...] + jnp.dot(mixed.astype(BF16), wout_ref[...], preferred_element_type=F32)

    h2 = _rms(x1, g_x_ref[...]).astype(BF16)
    qm = jnp.dot(h2, wq_ref[...], preferred_element_type=F32) * (1.0 / math.sqrt(XATTN_HEAD_DIM))
    qm = qm.astype(BF16)
    heads = []
    for hd in range(XATTN_HEADS):
        sl = slice(hd * XATTN_HEAD_DIM, (hd + 1) * XATTN_HEAD_DIM)
        s = lax.dot_general(qm[:, sl], km_ref[:, sl], (((1,), (1,)), ((), ())),
                            preferred_element_type=F32)
        p = jnp.exp(s - jnp.max(s, axis=1, keepdims=True))
        o = jnp.dot(p.astype(BF16), vm_ref[:, sl], preferred_element_type=F32)
        heads.append((o / jnp.sum(p, axis=1, keepdims=True)).astype(BF16))
    o_all = jnp.concatenate(heads, axis=1)
    x2 = x1 + jnp.dot(o_all, wo_ref[...], preferred_element_type=F32)
    x2_ref[...] = x2

    h3 = _rms(x2, g_moe_ref[...])
    h3_ref[...] = h3
    logits = jnp.dot(h3.astype(BF16), wr_ref[...], preferred_element_type=F32) + br_ref[...]
    lane = lax.broadcasted_iota(I32, (tm, LANES), 1)
    lane_f = lane.astype(F32)

    def first_lane(hit):
        return jnp.min(jnp.where(hit, lane_f, float(LANES)), axis=1, keepdims=True).astype(I32)

    gl = jnp.where(lane < N_GROUPS, logits, -jnp.inf)
    gmax = jnp.max(gl, axis=1, keepdims=True)
    g_idx = first_lane(gl == gmax)
    p_group = 1.0 / jnp.sum(jnp.exp(gl - gmax), axis=1, keepdims=True)
    e_lo = ROUTE_LANE0 + g_idx * EXPERTS_PER_GROUP
    el = jnp.where((lane >= e_lo) & (lane < e_lo + EXPERTS_PER_GROUP), logits, -jnp.inf)
    v1 = jnp.max(el, axis=1, keepdims=True)
    i1 = first_lane(el == v1)
    el2 = jnp.where(lane == i1, -jnp.inf, el)
    v2 = jnp.max(el2, axis=1, keepdims=True)
    i2 = first_lane(el2 == v2)
    t = jnp.exp(v2 - v1)
    w0 = p_group / (1.0 + t)
    w1 = p_group * t / (1.0 + t)

    oh0 = lane == i1
    oh1 = lane == i2
    oh = (oh0 | oh1).astype(BF16)
    r_i = lax.broadcasted_iota(I32, (tm, tm), 0)
    c_i = lax.broadcasted_iota(I32, (tm, tm), 1)
    lower = (c_i < r_i).astype(BF16)
    before = jnp.dot(lower, oh, preferred_element_type=F32) + run_ref[...]
    rank0 = jnp.sum(jnp.where(oh0, before, 0.0), axis=1, keepdims=True).astype(I32)
    rank1 = jnp.sum(jnp.where(oh1, before, 0.0), axis=1, keepdims=True).astype(I32)
    run_ref[...] = run_ref[...] + jnp.sum(oh.astype(F32), axis=0, keepdims=True)
    cnt_ref[...] = jnp.broadcast_to(run_ref[...], cnt_ref.shape).astype(I32)

    meta = jnp.where(lane == 0, i1 - ROUTE_LANE0, 0)
    meta = jnp.where(lane == 1, i2 - ROUTE_LANE0, meta)
    meta = jnp.where(lane == 2, rank0, meta)
    meta = jnp.where(lane == 3, rank1, meta)
    meta_ref[...] = meta
    wts_ref[...] = jnp.where(lane == 0, w0, jnp.where(lane == 1, w1, 0.0))


def _post_attn(x2d, attn, ga, mixb, wba, wout, g_x, wq, km, vm, wo, g_moe, wr, br, seq):
    n = x2d.shape[0]
    tps = seq // TM
    row_spec = lambda c: pl.BlockSpec((TM, c), lambda i: (i, 0))
    mem_spec = pl.BlockSpec((MEM_LEN, XATTN_HEADS * XATTN_HEAD_DIM), lambda i: (i // tps, 0))
    return pl.pallas_call(
        _post_attn_kernel,
        grid=(n // TM,),
        in_specs=[row_spec(D_MODEL), row_spec(MLA_HEADS * V_HEAD_DIM), row_spec(D_MODEL),
                  row_spec(D_MODEL), _const_spec(wba.shape), _const_spec(wout.shape),
                  _const_spec(g_x.shape), _const_spec(wq.shape), mem_spec, mem_spec,
                  _const_spec(wo.shape), _const_spec(g_moe.shape), _const_spec(wr.shape),
                  _const_spec(br.shape)],
        out_specs=[row_spec(D_MODEL), row_spec(D_MODEL), row_spec(LANES), row_spec(LANES),
                   _const_spec((8, LANES))],
        out_shape=[jax.ShapeDtypeStruct((n, D_MODEL), F32),
                   jax.ShapeDtypeStruct((n, D_MODEL), F32),
                   jax.ShapeDtypeStruct((n, LANES), I32),
                   jax.ShapeDtypeStruct((n, LANES), F32),
                   jax.ShapeDtypeStruct((8, LANES), I32)],
        scratch_shapes=[pltpu.VMEM((1, LANES), F32)],
        compiler_params=_params(("arbitrary",)),
        name="post_attn",
    )(x2d, attn, ga, mixb, wba, wout, g_x, wq, km, vm, wo, g_moe, wr, br)


def _tile_row(ref, group, k):
    return ref.at[group, pl.ds(k, 1), :]


def _hbm_row(ref, row):
    return ref.at[pl.ds(row, 1), :]


def _dispatch_kernel(fill_ref, dest_ref, zeros_ref, h3_ref, xd_ref, row_sem, fill_sem):
    i = pl.program_id(0)

    def fill_copy(b):
        return pltpu.make_async_copy(
            zeros_ref, xd_ref.at[pl.ds(pl.multiple_of(b * SLOT_BLOCK, SLOT_BLOCK), SLOT_BLOCK)],
            fill_sem)

    @pl.when(i == 0)
    def _():
        def start(b, c):
            @pl.when(fill_ref[b] > 0)
            def _():
                fill_copy(b).start()
            return c

        def wait(b, c):
            @pl.when(fill_ref[b] > 0)
            def _():
                fill_copy(b).wait()
            return c

        lax.fori_loop(0, fill_ref.shape[0], start, 0)
        lax.fori_loop(0, fill_ref.shape[0], wait, 0)

    def copies(g, k):
        r = g * ROW_GROUP + k
        src = _tile_row(h3_ref, g, k)
        return (pltpu.make_async_copy(src, _hbm_row(xd_ref, dest_ref[0, 0, 2 * r]), row_sem),
                pltpu.make_async_copy(src, _hbm_row(xd_ref, dest_ref[0, 0, 2 * r + 1]), row_sem))

    def issue(g, c):
        for k in range(ROW_GROUP):
            for cp in copies(g, k):
                cp.start()
        return c

    def drain(g, c):
        for k in range(ROW_GROUP):
            for cp in copies(g, k):
                cp.wait()
        return c

    lax.fori_loop(0, TROW // ROW_GROUP, issue, 0)
    lax.fori_loop(0, TROW // ROW_GROUP, drain, 0)


def _dispatch(fill_blk, dest3, h3, n_slots):
    n, d = h3.shape
    grid_spec = pltpu.PrefetchScalarGridSpec(
        num_scalar_prefetch=1,
        grid=(n // TROW,),
        in_specs=[pl.BlockSpec((1, 1, 2 * TROW), lambda i, fb: (i, 0, 0), memory_space=pltpu.SMEM),
                  pl.BlockSpec((SLOT_BLOCK, d), lambda i, fb: (0, 0)),
                  pl.BlockSpec((TROW // ROW_GROUP, ROW_GROUP, d), lambda i, fb: (i, 0, 0))],
        out_specs=pl.BlockSpec(memory_space=pl.ANY),
        scratch_shapes=[pltpu.SemaphoreType.DMA, pltpu.SemaphoreType.DMA],
    )
    return pl.pallas_call(
        _dispatch_kernel,
        grid_spec=grid_spec,
        out_shape=jax.ShapeDtypeStruct((n_slots, d), F32),
        compiler_params=_params(("arbitrary",)),
        name="dispatch",
    )(fill_blk, dest3, jnp.zeros((SLOT_BLOCK, d), F32), h3.reshape(n // ROW_GROUP, ROW_GROUP, d))


def _experts_kernel(be_ref, nb_ref, nxt_ref, xd_ref, wgu_hbm, wdn_hbm, yd_ref,
                    wgu_f32, wdn_f32, wgu_bf, wdn_bf, slot_ref, sem):
    i = pl.program_id(0)

    def weight_copies(e, slot):
        return (pltpu.make_async_copy(wgu_hbm.at[e], wgu_f32.at[slot], sem.at[slot]),
                pltpu.make_async_copy(wdn_hbm.at[e], wdn_f32.at[slot], sem.at[slot]))

    @pl.when(i == 0)
    def _():
        slot_ref[0] = 0
        for cp in weight_copies(be_ref[0], 0):
            cp.start()

    @pl.when(i < nb_ref[0])
    def _():
        @pl.when((i == 0) | (be_ref[i] != be_ref[jnp.maximum(i - 1, 0)]))
        def _():
            slot = slot_ref[0]
            for cp in weight_copies(be_ref[i], slot):
                cp.wait()
            wgu_bf[...] = wgu_f32[slot].astype(BF16)
            wdn_bf[...] = wdn_f32[slot].astype(BF16)

            @pl.when(nxt_ref[i] >= 0)
            def _():
                for cp in weight_copies(nxt_ref[i], 1 - slot):
                    cp.start()

            slot_ref[0] = 1 - slot

        gu = jnp.dot(xd_ref[...].astype(BF16), wgu_bf[...], preferred_element_type=F32)
        gate = gu[:, :EXPERT_FF]
        up = gu[:, EXPERT_FF:]
        act = (gate * jax.nn.sigmoid(gate) * up).astype(BF16)
        yd_ref[...] = jnp.dot(act, wdn_bf[...], preferred_element_type=F32)


def _experts(block_expert, n_blocks_used, next_expert, xd, w_gate_up, w_down):
    n_slots, d = xd.shape
    slot_map = lambda i, be, nb, nx: (jnp.minimum(i, nb[0] - 1), 0)
    grid_spec = pltpu.PrefetchScalarGridSpec(
        num_scalar_prefetch=3,
        grid=(n_slots // SLOT_BLOCK,),
        in_specs=[pl.BlockSpec((SLOT_BLOCK, d), slot_map),
                  pl.BlockSpec(memory_space=pl.ANY),
                  pl.BlockSpec(memory_space=pl.ANY)],
        out_specs=pl.BlockSpec((SLOT_BLOCK, d), slot_map),
        scratch_shapes=[pltpu.VMEM((2, D_MODEL, 2 * EXPERT_FF), F32),
                        pltpu.VMEM((2, EXPERT_FF, D_MODEL), F32),
                        pltpu.VMEM((D_MODEL, 2 * EXPERT_FF), BF16),
                        pltpu.VMEM((EXPERT_FF, D_MODEL), BF16),
                        pltpu.SMEM((1,), I32),
                        pltpu.SemaphoreType.DMA((2,))],
    )
    return pl.pallas_call(
        _experts_kernel,
        grid_spec=grid_spec,
        out_shape=jax.ShapeDtypeStruct(xd.shape, F32),
        input_output_aliases={3: 0},
        compiler_params=_params(("arbitrary",)),
        name="experts",
    )(block_expert, n_blocks_used, next_expert, xd, w_gate_up, w_down)


def _combine_kernel(dest_ref, dest_next_ref, x2_ref, wts_ref, g_ref, yd_ref, o_ref, buf, sem):
    i = pl.program_id(0)
    last = pl.num_programs(0) - 1
    rows = x2_ref.shape[0]
    slot = i % 2

    def copies(d_ref, slt, g, k):
        r = g * ROW_GROUP + k
        return (pltpu.make_async_copy(_hbm_row(yd_ref, d_ref[0, 0, 2 * r]),
                                      _tile_row(buf.at[slt, 0], g, k), sem.at[slt]),
                pltpu.make_async_copy(_hbm_row(yd_ref, d_ref[0, 0, 2 * r + 1]),
                                      _tile_row(buf.at[slt, 1], g, k), sem.at[slt]))

    def issue(d_ref, slt):
        def body(g, c):
            for k in range(ROW_GROUP):
                for cp in copies(d_ref, slt, g, k):
                    cp.start()
            return c
        lax.fori_loop(0, rows // ROW_GROUP, body, 0)

    @pl.when(i == 0)
    def _():
        issue(dest_ref, 0)

    @pl.when(i < last)
    def _():
        issue(dest_next_ref, 1 - slot)

    def drain(g, c):
        for k in range(ROW_GROUP):
            for cp in copies(dest_ref, slot, g, k):
                cp.wait()
        return c

    lax.fori_loop(0, rows // ROW_GROUP, drain, 0)

    lane = lax.broadcasted_iota(I32, (rows, LANES), 1)
    wts = wts_ref[...]
    w0 = jnp.sum(jnp.where(lane == 0, wts, 0.0), axis=1, keepdims=True)
    w1 = jnp.sum(jnp.where(lane == 1, wts, 0.0), axis=1, keepdims=True)
    y0 = buf[slot, 0].reshape(rows, D_MODEL)
    y1 = buf[slot, 1].reshape(rows, D_MODEL)
    o_ref[...] = _rms(x2_ref[...] + y0 * w0 + y1 * w1, g_ref[...])


def _combine(dest3, x2, wts, g_final, yd):
    n = x2.shape[0]
    steps = n // TROW
    dest_spec = lambda f: pl.BlockSpec((1, 1, 2 * TROW), f, memory_space=pltpu.SMEM)
    return pl.pallas_call(
        _combine_kernel,
        grid=(steps,),
        in_specs=[dest_spec(lambda i: (i, 0, 0)),
                  dest_spec(lambda i: (jnp.minimum(i + 1, steps - 1), 0, 0)),
                  pl.BlockSpec((TROW, D_MODEL), lambda i: (i, 0)),
                  pl.BlockSpec((TROW, LANES), lambda i: (i, 0)),
                  _const_spec((1, D_MODEL)),
                  pl.BlockSpec(memory_space=pl.ANY)],
        out_specs=pl.BlockSpec((TROW, D_MODEL), lambda i: (i, 0)),
        out_shape=jax.ShapeDtypeStruct((n, D_MODEL), F32),
        scratch_shapes=[pltpu.VMEM((2, 2, TROW // ROW_GROUP, ROW_GROUP, D_MODEL), F32),
                        pltpu.SemaphoreType.DMA((2,))],
        compiler_params=_params(("arbitrary",)),
        name="combine",
    )(dest3, dest3, x2, wts, g_final, yd)


def _head_slots(w, per_head, pieces):
    rows = w.shape[0]
    wh = w.reshape(rows, MLA_HEADS, per_head)
    out = jnp.zeros((rows, MLA_HEADS, HEAD_SLOT), w.dtype)
    for src_lo, src_hi, dst_lo, sign in pieces:
        out = out.at[:, :, dst_lo:dst_lo + (src_hi - src_lo)].set(sign * wh[:, :, src_lo:src_hi])
    return out.reshape(rows, MLA_HEADS * HEAD_SLOT)


def _layer(x2d, pos2d, invf, km, vm, batch, seq, g_mix, w_in, b_gate, g_cq, w_uq, g_ckv, w_ukv,
           conv_w, w_branch_a, w_branch_b, w_out, g_xattn, w_q_mem, w_o_mem, g_moe,
           w_router_group, b_router_group, w_router_expert, b_router_expert, w_gate_up, w_down,
           g_final):
    n = x2d.shape[0]
    row = lambda v: v.reshape(1, -1)
    lo, mid, hi = ROPE_LO, ROPE_LO + ROPE_HALF, ROPE_LO + QK_ROPE_DIM

    w_kr = w_in[:, COL_CKV:COL_KROPE]
    kr_a = jnp.zeros((D_MODEL, HEAD_SLOT), F32).at[:, lo:hi].set(w_kr)
    kr_b = (jnp.zeros((D_MODEL, HEAD_SLOT), F32)
            .at[:, lo:mid].set(-w_kr[:, ROPE_HALF:]).at[:, mid:hi].set(w_kr[:, :ROPE_HALF]))
    w1 = jnp.concatenate([w_in[:, :COL_CKV], kr_a, kr_b], axis=1).astype(BF16)
    w2 = w_in[:, COL_KROPE:COL_CONV].astype(BF16)
    w3 = w_in[:, COL_CONV:].astype(BF16)
    qd = QK_NOPE_DIM + QK_ROPE_DIM
    wqa = _head_slots(w_uq, qd, [(0, qd, 0, 1.0)]).astype(BF16)
    wqb = _head_slots(w_uq, qd, [(mid, hi, lo, -1.0), (lo, mid, mid, 1.0)]).astype(BF16)
    kvd = QK_NOPE_DIM + V_HEAD_DIM
    wkk = _head_slots(w_ukv, kvd, [(0, QK_NOPE_DIM, 0, 1.0)]).astype(BF16)
    wkv = _head_slots(w_ukv, kvd, [(QK_NOPE_DIM, kvd, 0, 1.0)]).astype(BF16)
    conv_w8 = jnp.zeros((8, CONV_WIDTH), F32).at[:conv_w.shape[0]].set(conv_w)

    q, k, v, ga, mixb = _in_proj(x2d, pos2d, invf, row(g_mix), w1, w2, w3, row(b_gate), row(g_cq),
                                 wqa, wqb, row(g_ckv), wkk, wkv, conv_w8,
                                 w_branch_b.astype(BF16), seq)
    attn = _mla_attn(q, k, v, batch, seq)

    wr = (jnp.zeros((D_MODEL, LANES), F32).at[:, :N_GROUPS].set(w_router_group)
          .at[:, ROUTE_LANE0:ROUTE_LANE0 + N_EXPERTS].set(w_router_expert).astype(BF16))
    br = (jnp.zeros((1, LANES), F32).at[0, :N_GROUPS].set(b_router_group)
          .at[0, ROUTE_LANE0:ROUTE_LANE0 + N_EXPERTS].set(b_router_expert))
    x2, h3, meta, wts, cnt = _post_attn(x2d, attn, ga, mixb, w_branch_a.astype(BF16),
                                   w_out.astype(BF16), row(g_xattn), w_q_mem.astype(BF16), km, vm,
                                   w_o_mem.astype(BF16), row(g_moe), wr, br, seq)

    counts = cnt[0, ROUTE_LANE0:ROUTE_LANE0 + N_EXPERTS]
    padded = (counts + SLOT_BLOCK - 1) // SLOT_BLOCK * SLOT_BLOCK
    pad_end = jnp.cumsum(padded)
    pad_start = pad_end - padded
    n_slots = (2 * n + SLOT_BLOCK - 1) // SLOT_BLOCK * SLOT_BLOCK + N_EXPERTS * SLOT_BLOCK
    expert_ids = meta[:, 0:2]
    onehot = expert_ids[:, :, None] == jnp.arange(N_EXPERTS, dtype=I32)[None, None, :]
    dest = jnp.sum(jnp.where(onehot, pad_start[None, None, :], 0), axis=-1) + meta[:, 2:4]
    dest3 = dest.astype(I32).reshape(n // TROW, 1, 2 * TROW)
    block_lo = jnp.arange(n_slots // SLOT_BLOCK, dtype=I32) * SLOT_BLOCK
    block_expert = jnp.minimum(jnp.sum(pad_end[None, :] <= block_lo[:, None], axis=1),
                               N_EXPERTS - 1).astype(I32)
    n_blocks_used = (pad_end[-1:] // SLOT_BLOCK).astype(I32)
    fill_blk = ((block_lo + SLOT_BLOCK == pad_end[block_expert])
                | (block_lo >= pad_end[-1])).astype(I32)

    cand = jnp.where(padded > 0, jnp.arange(N_EXPERTS, dtype=I32), N_EXPERTS)
    later = jnp.concatenate([lax.cummin(cand, reverse=True)[1:],
                             jnp.full((1,), N_EXPERTS, I32)])
    next_expert = jnp.where(later < N_EXPERTS, later, -1)[block_expert].astype(I32)

    xd = _dispatch(fill_blk, dest3, h3, n_slots)
    yd = _experts(block_expert, n_blocks_used, next_expert, xd, w_gate_up, w_down)
    return _combine(dest3, x2, wts, row(g_final), yd)


def kernel(x, mem, positions, g_mix, w_in, b_gate, g_cq, w_uq, g_ckv, w_ukv, conv_w, w_branch_a,
           w_branch_b, w_out, g_xattn, g_mem, w_q_mem, w_kv_mem, w_o_mem, g_moe, w_router_group,
           b_router_group, w_router_expert, b_router_expert, w_gate_up, w_down, g_final):
    batch, seq, d = x.shape
    depth = g_mix.shape[0]
    assert d == D_MODEL and depth == 1 and seq % TM == 0 and seq % TQ == 0 and TQ == TK
    n = batch * seq
    x2d = x.reshape(n, d)
    pos2d = positions.reshape(n, 1).astype(I32)
    inv_freq = ROPE_THETA ** (-jnp.arange(0, QK_ROPE_DIM, 2, dtype=F32) / QK_ROPE_DIM)
    invf = (jnp.zeros((1, HEAD_SLOT), F32).at[0, ROPE_LO:ROPE_LO + ROPE_HALF].set(inv_freq)
            .at[0, ROPE_LO + ROPE_HALF:ROPE_LO + QK_ROPE_DIM].set(inv_freq))
    l = 0
    km, vm = _mem_kv(mem.reshape(batch * MEM_LEN, d), g_mem[l].reshape(1, -1),
                     w_kv_mem[l].astype(BF16))
    out = _layer(x2d, pos2d, invf, km, vm, batch, seq, g_mix[l], w_in[l], b_gate[l], g_cq[l],
                 w_uq[l], g_ckv[l], w_ukv[l], conv_w[l], w_branch_a[l], w_branch_b[l], w_out[l],
                 g_xattn[l], w_q_mem[l], w_o_mem[l], g_moe[l], w_router_group[l],
                 b_router_group[l], w_router_expert[l], b_router_expert[l], w_gate_up[l],
                 w_down[l], g_final)
    return out.reshape(batch, seq, d)
```

```python
import functools
import math

import jax
import jax.numpy as jnp
from jax import lax
from jax.experimental import pallas as pl
from jax.experimental.pallas import tpu as pltpu

F32 = jnp.float32
BF16 = jnp.bfloat16
I32 = jnp.int32

D_MODEL = 1024
MEM_LEN = 256
MLA_HEADS = 8
QK_NOPE_DIM = 64
QK_ROPE_DIM = 32
V_HEAD_DIM = 64
Q_LORA_RANK = 256
KV_LORA_RANK = 128
ROPE_THETA = 10000.0
CONV_WIDTH = 512
XATTN_HEADS = 4
XATTN_HEAD_DIM = 128
N_GROUPS = 8
EXPERTS_PER_GROUP = 8
N_EXPERTS = N_GROUPS * EXPERTS_PER_GROUP
EXPERT_FF = 256
EPS = 1e-6

COL_CQ = Q_LORA_RANK
COL_CKV = COL_CQ + KV_LORA_RANK
COL_KROPE = COL_CKV + QK_ROPE_DIM
COL_CONV = COL_KROPE + 3 * CONV_WIDTH

LANES = 128
HEAD_SLOT = LANES
ROPE_LO = QK_NOPE_DIM
ROPE_HALF = QK_ROPE_DIM // 2
L_LANE = V_HEAD_DIM
SLOT_BLOCK = 256
ROUTE_LANE0 = N_GROUPS
NEG_BIG = -1e30

TM = 512
TQ = 512
TK = 512
DIAG_STRIPS = 2
ATTN_CHAINS = 8
TROW = 256
ROW_GROUP = 8
VMEM_LIMIT = 56 * 1024 * 1024


def _rms(x, g):
    return x * lax.rsqrt(jnp.mean(x * x, axis=-1, keepdims=True) + EPS) * g


def _const_spec(shape):
    nd = len(shape)
    return pl.BlockSpec(shape, lambda *_: (0,) * nd)


def _params(sem):
    return pltpu.CompilerParams(dimension_semantics=sem, vmem_limit_bytes=VMEM_LIMIT)


def _mem_kv_kernel(mem_ref, g_ref, w_ref, k_ref, v_ref):
    h = _rms(mem_ref[...], g_ref[...]).astype(BF16)
    kv = jnp.dot(h, w_ref[...], preferred_element_type=F32)
    half = XATTN_HEADS * XATTN_HEAD_DIM
    k_ref[...] = kv[:, :half].astype(BF16)
    v_ref[...] = kv[:, half:].astype(BF16)


def _mem_kv(mem2d, g_mem, w_kv):
    rows = mem2d.shape[0]
    half = XATTN_HEADS * XATTN_HEAD_DIM
    return pl.pallas_call(
        _mem_kv_kernel,
        grid=(rows // MEM_LEN,),
        in_specs=[pl.BlockSpec((MEM_LEN, D_MODEL), lambda i: (i, 0)),
                  _const_spec((1, D_MODEL)),
                  _const_spec((D_MODEL, 2 * half))],
        out_specs=[pl.BlockSpec((MEM_LEN, half), lambda i: (i, 0)),
                   pl.BlockSpec((MEM_LEN, half), lambda i: (i, 0))],
        out_shape=[jax.ShapeDtypeStruct((rows, half), BF16)] * 2,
        compiler_params=_params(("arbitrary",)),
        name="mem_kv",
    )(mem2d, g_mem, w_kv)


def _in_proj_kernel(x_ref, pos_ref, invf_ref, g_mix_ref, w1_ref, w2_ref, w3_ref, b_gate_ref,
                    g_cq_ref, wqa_ref, wqb_ref, g_ckv_ref, wkk_ref, wkv_ref, conv_w_ref, wbb_ref,
                    q_ref, k_ref, v_ref, ga_ref, mixb_ref, carry_ref, *, tiles_per_seq):
    step = pl.program_id(0)
    tm = x_ref.shape[0]
    h = _rms(x_ref[...], g_mix_ref[...]).astype(BF16)

    lane = lax.broadcasted_iota(I32, (tm, HEAD_SLOT), 1)
    rope_lane = (lane >= ROPE_LO) & (lane < ROPE_LO + QK_ROPE_DIM)
    ang = pos_ref[...].astype(F32) * invf_ref[...]
    cos_t = jnp.where(rope_lane, jnp.cos(ang), 0.0)
    sin_t = jnp.where(rope_lane, jnp.sin(ang), 0.0)

    p1 = jnp.dot(h, w1_ref[...], preferred_element_type=F32)
    c_q = p1[:, :COL_CQ]
    c_kv = p1[:, COL_CQ:COL_CKV]
    kr_a = p1[:, COL_CKV:COL_CKV + HEAD_SLOT]
    kr_b = p1[:, COL_CKV + HEAD_SLOT:COL_CKV + 2 * HEAD_SLOT]
    k_rot = kr_a * cos_t + kr_b * sin_t

    scale = math.log2(math.e) / math.sqrt(QK_NOPE_DIM + QK_ROPE_DIM)
    q_cos = jnp.where(lane < ROPE_LO, scale, cos_t * scale)
    q_sin = sin_t * scale
    cqn = _rms(c_q, g_cq_ref[...]).astype(BF16)
    qa = jnp.dot(cqn, wqa_ref[...], preferred_element_type=F32)
    qb = jnp.dot(cqn, wqb_ref[...], preferred_element_type=F32)
    ckvn = _rms(c_kv, g_ckv_ref[...]).astype(BF16)
    kk = jnp.dot(ckvn, wkk_ref[...], preferred_element_type=F32)
    vv = jnp.dot(ckvn, wkv_ref[...], preferred_element_type=F32)
    ones_lane = jnp.where(lane == L_LANE, 1.0, 0.0)
    for hd in range(MLA_HEADS):
        sl = slice(hd * HEAD_SLOT, (hd + 1) * HEAD_SLOT)
        q_ref[:, sl] = (qa[:, sl] * q_cos + qb[:, sl] * q_sin).astype(BF16)
        k_ref[:, sl] = (kk[:, sl] + k_rot).astype(BF16)
        v_ref[:, sl] = (vv[:, sl] + ones_lane).astype(BF16)

    p2 = jnp.dot(h, w2_ref[...], preferred_element_type=F32)
    b_gate = p2[:, :CONV_WIDTH]
    u = p2[:, CONV_WIDTH:2 * CONV_WIDTH] * p2[:, 2 * CONV_WIDTH:]

    @pl.when(step % tiles_per_seq == 0)
    def _():
        carry_ref[...] = jnp.zeros_like(carry_ref)

    prev1 = carry_ref[7:8, :]
    prev2 = carry_ref[6:7, :]
    row = lax.broadcasted_iota(I32, (tm, CONV_WIDTH), 0)
    u1 = jnp.where(row == 0, prev1, pltpu.roll(u, 1, 0))
    u2 = jnp.where(row == 0, prev2, jnp.where(row == 1, prev1, pltpu.roll(u, 2, 0)))
    carry_ref[...] = u[tm - 8:, :]
    cw = conv_w_ref[...]
    conv = b_gate * (cw[0:1, :] * u2 + cw[1:2, :] * u1 + cw[2:3, :] * u)
    branch_b = jnp.dot(conv.astype(BF16), wbb_ref[...], preferred_element_type=F32)

    gates = jax.nn.sigmoid(jnp.dot(h, w3_ref[...], preferred_element_type=F32) + b_gate_ref[...])
    ga_ref[...] = gates[:, :D_MODEL].astype(BF16)
    mixb_ref[...] = (gates[:, D_MODEL:] * branch_b).astype(BF16)


def _in_proj(x2d, pos2d, invf, g_mix, w1, w2, w3, b_gate, g_cq, wqa, wqb, g_ckv, wkk, wkv,
             conv_w, wbb, seq):
    n = x2d.shape[0]
    row_spec = lambda c: pl.BlockSpec((TM, c), lambda i: (i, 0))
    consts = [invf, g_mix, w1, w2, w3, b_gate, g_cq, wqa, wqb, g_ckv, wkk, wkv, conv_w, wbb]
    kv_cols = MLA_HEADS * HEAD_SLOT
    return pl.pallas_call(
        functools.partial(_in_proj_kernel, tiles_per_seq=seq // TM),
        grid=(n // TM,),
        in_specs=[row_spec(D_MODEL), row_spec(1)] + [_const_spec(c.shape) for c in consts],
        out_specs=[row_spec(kv_cols), row_spec(kv_cols), row_spec(kv_cols),
                   row_spec(D_MODEL), row_spec(D_MODEL)],
        out_shape=[jax.ShapeDtypeStruct((n, kv_cols), BF16),
                   jax.ShapeDtypeStruct((n, kv_cols), BF16),
                   jax.ShapeDtypeStruct((n, kv_cols), BF16),
                   jax.ShapeDtypeStruct((n, D_MODEL), BF16),
                   jax.ShapeDtypeStruct((n, D_MODEL), BF16)],
        scratch_shapes=[pltpu.VMEM((8, CONV_WIDTH), F32)],
        compiler_params=_params(("arbitrary",)),
        name="in_proj",
    )(x2d, pos2d, *consts)


def _attn_kernel(q_ref, k_ref, v_ref, o_ref):
    qi = pl.program_id(1)
    tq = q_ref.shape[0]

    def block(carry, heads, col_start, ncols, row_lo, masked):
        out = []
        for c, hd in enumerate(heads):
            m, acc = carry[c]
            hsl = slice(hd * HEAD_SLOT, (hd + 1) * HEAD_SLOT)
            kb = k_ref[pl.ds(col_start, ncols), hsl]
            vb = v_ref[pl.ds(col_start, ncols), hsl]
            s = lax.dot_general(q_ref[row_lo:, hsl], kb, (((1,), (1,)), ((), ())),
                                preferred_element_type=F32)
            if masked:
                r = qi * tq + row_lo + lax.broadcasted_iota(I32, s.shape, 0)
                s = jnp.where(col_start + lax.broadcasted_iota(I32, s.shape, 1) <= r, s, NEG_BIG)
            m_hi, acc_hi = m[row_lo:], acc[row_lo:]
            m_new = jnp.maximum(m_hi, jnp.max(s, axis=1, keepdims=True))
            p = jnp.exp2(s - m_new).astype(BF16)
            acc_new = jnp.exp2(m_hi - m_new) * acc_hi + jnp.dot(p, vb, preferred_element_type=F32)
            if row_lo:
                m_new = jnp.concatenate([m[:row_lo], m_new], axis=0)
                acc_new = jnp.concatenate([acc[:row_lo], acc_new], axis=0)
            out.append((m_new, acc_new))
        return tuple(out)

    def full_step(j, carry, heads):
        return block(carry, heads, pl.multiple_of(j * TK, TK), TK, 0, False)

    lane = lax.broadcasted_iota(I32, (tq, HEAD_SLOT), 1)
    for g in range(MLA_HEADS // ATTN_CHAINS):
        heads = tuple(range(g * ATTN_CHAINS, (g + 1) * ATTN_CHAINS))
        carry = tuple((jnp.full((tq, 1), NEG_BIG, F32), jnp.zeros((tq, HEAD_SLOT), F32))
                      for _ in heads)
        carry = lax.fori_loop(0, qi, functools.partial(full_step, heads=heads), carry)
        for d in range(DIAG_STRIPS):
            col = pl.multiple_of(qi * TK + d * (TK // DIAG_STRIPS), TK // DIAG_STRIPS)
            carry = block(carry, heads, col, TK // DIAG_STRIPS, d * (tq // DIAG_STRIPS), True)
        outs = []
        for _, acc in carry:
            denom = jnp.sum(jnp.where(lane == L_LANE, acc, 0.0), axis=1, keepdims=True)
            outs.append(acc / denom)
        for c in range(0, ATTN_CHAINS, 2):
            hp = heads[c] // 2
            pair = jnp.where(lane < V_HEAD_DIM, outs[c], pltpu.roll(outs[c + 1], V_HEAD_DIM, 1))
            o_ref[:, hp * HEAD_SLOT:(hp + 1) * HEAD_SLOT] = pair.astype(BF16)


def _mla_attn(q, k, v, batch, seq):
    n = q.shape[0]
    nq = seq // TQ
    out_cols = MLA_HEADS * V_HEAD_DIM
    return pl.pallas_call(
        _attn_kernel,
        grid=(batch, nq),
        in_specs=[pl.BlockSpec((TQ, q.shape[1]), lambda b, i: (b * nq + i, 0)),
                  pl.BlockSpec((seq, k.shape[1]), lambda b, i: (b, 0)),
                  pl.BlockSpec((seq, v.shape[1]), lambda b, i: (b, 0))],
        out_specs=pl.BlockSpec((TQ, out_cols), lambda b, i: (b * nq + i, 0)),
        out_shape=jax.ShapeDtypeStruct((n, out_cols), BF16),
        compiler_params=_params(("arbitrary", "arbitrary")),
        name="mla_attn",
    )(q, k, v)


def _post_attn_kernel(x_ref, attn_ref, ga_ref, mixb_ref, wba_ref, wout_ref, g_x_ref, wq_ref,
                      km_ref, vm_ref, wo_ref, g_moe_ref, wr_ref, br_ref,
                      x2_ref, h3_ref, meta_ref, wts_ref, cnt_ref, run_ref):
    step = pl.program_id(0)
    tm = x_ref.shape[0]

    @pl.when(step == 0)
    def _():
        run_ref[...] = jnp.zeros_like(run_ref)

    a = jnp.dot(attn_ref[...], wba_ref[...], preferred_element_type=F32)
    mixed = ga_ref[...].astype(F32) * a + mixb_ref[...].astype(F32)
    x1 = x_ref[...] + jnp.dot(mixed.astype(BF16), wout_ref[...], preferred_element_type=F32)

    h2 = _rms(x1, g_x_ref[...]).astype(BF16)
    qm = jnp.dot(h2, wq_ref[...], preferred_element_type=F32) * (1.0 / math.sqrt(XATTN_HEAD_DIM))
    qm = qm.astype(BF16)
    heads = []
    for hd in range(XATTN_HEADS):
        sl = slice(hd * XATTN_HEAD_DIM, (hd + 1) * XATTN_HEAD_DIM)
        s = lax.dot_general(qm[:, sl], km_ref[:, sl], (((1,), (1,)), ((), ())),
                            preferred_element_type=F32)
        p = jnp.exp(s - jnp.max(s, axis=1, keepdims=True))
        o = jnp.dot(p.astype(BF16), vm_ref[:, sl], preferred_element_type=F32)
        heads.append((o / jnp.sum(p, axis=1, keepdims=True)).astype(BF16))
    o_all = jnp.concatenate(heads, axis=1)
    x2 = x1 + jnp.dot(o_all, wo_ref[...], preferred_element_type=F32)
    x2_ref[...] = x2

    h3 = _rms(x2, g_moe_ref[...])
    h3_ref[...] = h3.reshape(h3_ref.shape)
    logits = jnp.dot(h3.astype(BF16), wr_ref[...], preferred_element_type=F32) + br_ref[...]
    lane = lax.broadcasted_iota(I32, (tm, LANES), 1)
    lane_f = lane.astype(F32)

    def first_lane(hit):
        return jnp.min(jnp.where(hit, lane_f, float(LANES)), axis=1, keepdims=True).astype(I32)

    gl = jnp.where(lane < N_GROUPS, logits, -jnp.inf)
    gmax = jnp.max(gl, axis=1, keepdims=True)
    g_idx = first_lane(gl == gmax)
    p_group = 1.0 / jnp.sum(jnp.exp(gl - gmax), axis=1, keepdims=True)
    e_lo = ROUTE_LANE0 + g_idx * EXPERTS_PER_GROUP
    el = jnp.where((lane >= e_lo) & (lane < e_lo + EXPERTS_PER_GROUP), logits, -jnp.inf)
    v1 = jnp.max(el, axis=1, keepdims=True)
    i1 = first_lane(el == v1)
    el2 = jnp.where(lane == i1, -jnp.inf, el)
    v2 = jnp.max(el2, axis=1, keepdims=True)
    i2 = first_lane(el2 == v2)
    t = jnp.exp(v2 - v1)
    w0 = p_group / (1.0 + t)
    w1 = p_group * t / (1.0 + t)

    oh0 = lane == i1
    oh1 = lane == i2
    oh = (oh0 | oh1).astype(BF16)
    r_i = lax.broadcasted_iota(I32, (tm, tm), 0)
    c_i = lax.broadcasted_iota(I32, (tm, tm), 1)
    lower = (c_i < r_i).astype(BF16)
    before = jnp.dot(lower, oh, preferred_element_type=F32) + run_ref[...]
    rank0 = jnp.sum(jnp.where(oh0, before, 0.0), axis=1, keepdims=True).astype(I32)
    rank1 = jnp.sum(jnp.where(oh1, before, 0.0), axis=1, keepdims=True).astype(I32)
    run_ref[...] = run_ref[...] + jnp.sum(oh.astype(F32), axis=0, keepdims=True)
    cnt_ref[...] = jnp.broadcast_to(run_ref[...], cnt_ref.shape).astype(I32)

    meta = jnp.where(lane == 0, i1 - ROUTE_LANE0, 0)
    meta = jnp.where(lane == 1, i2 - ROUTE_LANE0, meta)
    meta = jnp.where(lane == 2, rank0, meta)
    meta = jnp.where(lane == 3, rank1, meta)
    meta_ref[...] = meta
    wts_ref[...] = jnp.where(lane == 0, w0, jnp.where(lane == 1, w1, 0.0))


def _post_attn(x2d, attn, ga, mixb, wba, wout, g_x, wq, km, vm, wo, g_moe, wr, br, seq):
    n = x2d.shape[0]
    tps = seq // TM
    row_spec = lambda c: pl.BlockSpec((TM, c), lambda i: (i, 0))
    mem_spec = pl.BlockSpec((MEM_LEN, XATTN_HEADS * XATTN_HEAD_DIM), lambda i: (i // tps, 0))
    return pl.pallas_call(
        _post_attn_kernel,
        grid=(n // TM,),
        in_specs=[row_spec(D_MODEL), row_spec(MLA_HEADS * V_HEAD_DIM), row_spec(D_MODEL),
                  row_spec(D_MODEL), _const_spec(wba.shape), _const_spec(wout.shape),
                  _const_spec(g_x.shape), _const_spec(wq.shape), mem_spec, mem_spec,
                  _const_spec(wo.shape), _const_spec(g_moe.shape), _const_spec(wr.shape),
                  _const_spec(br.shape)],
        out_specs=[row_spec(D_MODEL),
                   pl.BlockSpec((TM // ROW_GROUP, ROW_GROUP, D_MODEL), lambda i: (i, 0, 0)),
                   row_spec(LANES), row_spec(LANES),
                   _const_spec((8, LANES))],
        out_shape=[jax.ShapeDtypeStruct((n, D_MODEL), F32),
                   jax.ShapeDtypeStruct((n // ROW_GROUP, ROW_GROUP, D_MODEL), F32),
                   jax.ShapeDtypeStruct((n, LANES), I32),
                   jax.ShapeDtypeStruct((n, LANES), F32),
                   jax.ShapeDtypeStruct((8, LANES), I32)],
        scratch_shapes=[pltpu.VMEM((1, LANES), F32)],
        compiler_params=_params(("arbitrary",)),
        name="post_attn",
    )(x2d, attn, ga, mixb, wba, wout, g_x, wq, km, vm, wo, g_moe, wr, br)


def _tile_row(ref, group, k):
    return ref.at[group, pl.ds(k, 1), :]


def _hbm_row(ref, row):
    return ref.at[pl.ds(row, 1), :]


def _dispatch_kernel(fill_ref, dest_ref, zeros_ref, h3_ref, xd_ref, row_sem, fill_sem):
    i = pl.program_id(0)

    def fill_copy(b):
        return pltpu.make_async_copy(
            zeros_ref, xd_ref.at[pl.ds(pl.multiple_of(b * SLOT_BLOCK, SLOT_BLOCK), SLOT_BLOCK)],
            fill_sem)

    @pl.when(i == 0)
    def _():
        def start(b, c):
            @pl.when(fill_ref[b] > 0)
            def _():
                fill_copy(b).start()
            return c

        def wait(b, c):
            @pl.when(fill_ref[b] > 0)
            def _():
                fill_copy(b).wait()
            return c

        lax.fori_loop(0, fill_ref.shape[0], start, 0)
        lax.fori_loop(0, fill_ref.shape[0], wait, 0)

    def copies(g, k):
        r = g * ROW_GROUP + k
        src = _tile_row(h3_ref, g, k)
        return (pltpu.make_async_copy(src, _hbm_row(xd_ref, dest_ref[0, 0, 2 * r]), row_sem),
                pltpu.make_async_copy(src, _hbm_row(xd_ref, dest_ref[0, 0, 2 * r + 1]), row_sem))

    def issue(g, c):
        for k in range(ROW_GROUP):
            for cp in copies(g, k):
                cp.start()
        return c

    def drain(g, c):
        for k in range(ROW_GROUP):
            for cp in copies(g, k):
                cp.wait()
        return c

    lax.fori_loop(0, TROW // ROW_GROUP, issue, 0)
    lax.fori_loop(0, TROW // ROW_GROUP, drain, 0)


def _dispatch(fill_blk, dest3, h3, n_slots):
    groups, _, d = h3.shape
    n = groups * ROW_GROUP
    grid_spec = pltpu.PrefetchScalarGridSpec(
        num_scalar_prefetch=1,
        grid=(n // TROW,),
        in_specs=[pl.BlockSpec((1, 1, 2 * TROW), lambda i, fb: (i, 0, 0), memory_space=pltpu.SMEM),
                  pl.BlockSpec((SLOT_BLOCK, d), lambda i, fb: (0, 0)),
                  pl.BlockSpec((TROW // ROW_GROUP, ROW_GROUP, d), lambda i, fb: (i, 0, 0))],
        out_specs=pl.BlockSpec(memory_space=pl.ANY),
        scratch_shapes=[pltpu.SemaphoreType.DMA, pltpu.SemaphoreType.DMA],
    )
    return pl.pallas_call(
        _dispatch_kernel,
        grid_spec=grid_spec,
        out_shape=jax.ShapeDtypeStruct((n_slots, d), F32),
        compiler_params=_params(("arbitrary",)),
        name="dispatch",
    )(fill_blk, dest3, jnp.zeros((SLOT_BLOCK, d), F32), h3)


def _experts_kernel(be_ref, nb_ref, nxt_ref, xd_ref, wgu_hbm, wdn_hbm, yd_ref,
                    wgu_f32, wdn_f32, wgu_bf, wdn_bf, slot_ref, sem):
    i = pl.program_id(0)

    def weight_copies(e, slot):
        return (pltpu.make_async_copy(wgu_hbm.at[e], wgu_f32.at[slot], sem.at[slot]),
                pltpu.make_async_copy(wdn_hbm.at[e], wdn_f32.at[slot], sem.at[slot]))

    @pl.when(i == 0)
    def _():
        slot_ref[0] = 0
        for cp in weight_copies(be_ref[0], 0):
            cp.start()

    @pl.when(i < nb_ref[0])
    def _():
        @pl.when((i == 0) | (be_ref[i] != be_ref[jnp.maximum(i - 1, 0)]))
        def _():
            slot = slot_ref[0]
            for cp in weight_copies(be_ref[i], slot):
                cp.wait()
            wgu_bf[...] = wgu_f32[slot].astype(BF16)
            wdn_bf[...] = wdn_f32[slot].astype(BF16)

            @pl.when(nxt_ref[i] >= 0)
            def _():
                for cp in weight_copies(nxt_ref[i], 1 - slot):
                    cp.start()

            slot_ref[0] = 1 - slot

        gu = jnp.dot(xd_ref[...].astype(BF16), wgu_bf[...], preferred_element_type=F32)
        gate = gu[:, :EXPERT_FF]
        up = gu[:, EXPERT_FF:]
        act = (gate * jax.nn.sigmoid(gate) * up).astype(BF16)
        yd_ref[...] = jnp.dot(act, wdn_bf[...], preferred_element_type=F32)


def _experts(block_expert, n_blocks_used, next_expert, xd, w_gate_up, w_down):
    n_slots, d = xd.shape
    slot_map = lambda i, be, nb, nx: (jnp.minimum(i, nb[0] - 1), 0)
    grid_spec = pltpu.PrefetchScalarGridSpec(
        num_scalar_prefetch=3,
        grid=(n_slots // SLOT_BLOCK,),
        in_specs=[pl.BlockSpec((SLOT_BLOCK, d), slot_map),
                  pl.BlockSpec(memory_space=pl.ANY),
                  pl.BlockSpec(memory_space=pl.ANY)],
        out_specs=pl.BlockSpec((SLOT_BLOCK, d), slot_map),
        scratch_shapes=[pltpu.VMEM((2, D_MODEL, 2 * EXPERT_FF), F32),
                        pltpu.VMEM((2, EXPERT_FF, D_MODEL), F32),
                        pltpu.VMEM((D_MODEL, 2 * EXPERT_FF), BF16),
                        pltpu.VMEM((EXPERT_FF, D_MODEL), BF16),
                        pltpu.SMEM((1,), I32),
                        pltpu.SemaphoreType.DMA((2,))],
    )
    return pl.pallas_call(
        _experts_kernel,
        grid_spec=grid_spec,
        out_shape=jax.ShapeDtypeStruct(xd.shape, F32),
        input_output_aliases={3: 0},
        compiler_params=_params(("arbitrary",)),
        name="experts",
    )(block_expert, n_blocks_used, next_expert, xd, w_gate_up, w_down)


def _combine_kernel(dest_ref, dest_next_ref, x2_ref, wts_ref, g_ref, yd_ref, o_ref, buf, sem):
    i = pl.program_id(0)
    last = pl.num_programs(0) - 1
    rows = x2_ref.shape[0]
    slot = i % 2

    def copies(d_ref, slt, g, k):
        r = g * ROW_GROUP + k
        return (pltpu.make_async_copy(_hbm_row(yd_ref, d_ref[0, 0, 2 * r]),
                                      _tile_row(buf.at[slt, 0], g, k), sem.at[slt]),
                pltpu.make_async_copy(_hbm_row(yd_ref, d_ref[0, 0, 2 * r + 1]),
                                      _tile_row(buf.at[slt, 1], g, k), sem.at[slt]))

    def issue(d_ref, slt):
        def body(g, c):
            for k in range(ROW_GROUP):
                for cp in copies(d_ref, slt, g, k):
                    cp.start()
            return c
        lax.fori_loop(0, rows // ROW_GROUP, body, 0)

    @pl.when(i == 0)
    def _():
        issue(dest_ref, 0)

    @pl.when(i < last)
    def _():
        issue(dest_next_ref, 1 - slot)

    def drain(g, c):
        for k in range(ROW_GROUP):
            for cp in copies(dest_ref, slot, g, k):
                cp.wait()
        return c

    lax.fori_loop(0, rows // ROW_GROUP, drain, 0)

    lane = lax.broadcasted_iota(I32, (rows, LANES), 1)
    wts = wts_ref[...]
    w0 = jnp.sum(jnp.where(lane == 0, wts, 0.0), axis=1, keepdims=True)
    w1 = jnp.sum(jnp.where(lane == 1, wts, 0.0), axis=1, keepdims=True)
    y0 = buf[slot, 0].reshape(rows, D_MODEL)
    y1 = buf[slot, 1].reshape(rows, D_MODEL)
    o_ref[...] = _rms(x2_ref[...] + y0 * w0 + y1 * w1, g_ref[...])


def _combine(dest3, x2, wts, g_final, yd):
    n = x2.shape[0]
    steps = n // TROW
    dest_spec = lambda f: pl.BlockSpec((1, 1, 2 * TROW), f, memory_space=pltpu.SMEM)
    return pl.pallas_call(
        _combine_kernel,
        grid=(steps,),
        in_specs=[dest_spec(lambda i: (i, 0, 0)),
                  dest_spec(lambda i: (jnp.minimum(i + 1, steps - 1), 0, 0)),
                  pl.BlockSpec((TROW, D_MODEL), lambda i: (i, 0)),
                  pl.BlockSpec((TROW, LANES), lambda i: (i, 0)),
                  _const_spec((1, D_MODEL)),
                  pl.BlockSpec(memory_space=pl.ANY)],
        out_specs=pl.BlockSpec((TROW, D_MODEL), lambda i: (i, 0)),
        out_shape=jax.ShapeDtypeStruct((n, D_MODEL), F32),
        scratch_shapes=[pltpu.VMEM((2, 2, TROW // ROW_GROUP, ROW_GROUP, D_MODEL), F32),
                        pltpu.SemaphoreType.DMA((2,))],
        compiler_params=_params(("arbitrary",)),
        name="combine",
    )(dest3, dest3, x2, wts, g_final, yd)


def _lanes(parts, width):
    ref = next(p for p in parts if not isinstance(p, int))
    cols = [jnp.zeros(ref.shape[:-1] + (p,), ref.dtype) if isinstance(p, int) else p for p in parts]
    used = sum(c.shape[-1] for c in cols)
    assert used <= width
    if used < width:
        cols.append(jnp.zeros(ref.shape[:-1] + (width - used,), ref.dtype))
    return jnp.concatenate(cols, axis=-1)


def _head_slots(w, per_head, pieces):
    rows = w.shape[0]
    wh = w.reshape(rows, MLA_HEADS, per_head)
    return _lanes(pieces(wh), HEAD_SLOT).reshape(rows, MLA_HEADS * HEAD_SLOT)


def _layer(x2d, pos2d, invf, km, vm, batch, seq, g_mix, w_in, b_gate, g_cq, w_uq, g_ckv, w_ukv,
           conv_w, w_branch_a, w_branch_b, w_out, g_xattn, w_q_mem, w_o_mem, g_moe,
           w_router_group, b_router_group, w_router_expert, b_router_expert, w_gate_up, w_down,
           g_final):
    n = x2d.shape[0]
    row = lambda v: v.reshape(1, -1)
    lo, mid, hi = ROPE_LO, ROPE_LO + ROPE_HALF, ROPE_LO + QK_ROPE_DIM

    w_kr = w_in[:, COL_CKV:COL_KROPE]
    kr_a = _lanes([lo, w_kr], HEAD_SLOT)
    kr_b = _lanes([lo, -w_kr[:, ROPE_HALF:], w_kr[:, :ROPE_HALF]], HEAD_SLOT)
    w1 = jnp.concatenate([w_in[:, :COL_CKV], kr_a, kr_b], axis=1).astype(BF16)
    w2 = w_in[:, COL_KROPE:COL_CONV].astype(BF16)
    w3 = w_in[:, COL_CONV:].astype(BF16)
    qd = QK_NOPE_DIM + QK_ROPE_DIM
    wqa = _head_slots(w_uq, qd, lambda wh: [wh]).astype(BF16)
    wqb = _head_slots(w_uq, qd, lambda wh: [lo, -wh[..., mid:hi], wh[..., lo:mid]]).astype(BF16)
    kvd = QK_NOPE_DIM + V_HEAD_DIM
    wkk = _head_slots(w_ukv, kvd, lambda wh: [wh[..., :QK_NOPE_DIM]]).astype(BF16)
    wkv = _head_slots(w_ukv, kvd, lambda wh: [wh[..., QK_NOPE_DIM:]]).astype(BF16)
    conv_w8 = jnp.concatenate([conv_w, jnp.zeros((8 - conv_w.shape[0], CONV_WIDTH), F32)], axis=0)

    q, k, v, ga, mixb = _in_proj(x2d, pos2d, invf, row(g_mix), w1, w2, w3, row(b_gate), row(g_cq),
                                 wqa, wqb, row(g_ckv), wkk, wkv, conv_w8,
                                 w_branch_b.astype(BF16), seq)
    attn = _mla_attn(q, k, v, batch, seq)

    wr = _lanes([w_router_group, w_router_expert], LANES).astype(BF16)
    br = _lanes([row(b_router_group), row(b_router_expert)], LANES)
    x2, h3, meta, wts, cnt = _post_attn(x2d, attn, ga, mixb, w_branch_a.astype(BF16),
                                   w_out.astype(BF16), row(g_xattn), w_q_mem.astype(BF16), km, vm,
                                   w_o_mem.astype(BF16), row(g_moe), wr, br, seq)

    counts = cnt[0, ROUTE_LANE0:ROUTE_LANE0 + N_EXPERTS]
    padded = (counts + SLOT_BLOCK - 1) // SLOT_BLOCK * SLOT_BLOCK
    pad_end = jnp.cumsum(padded)
    pad_start = pad_end - padded
    n_slots = (2 * n + SLOT_BLOCK - 1) // SLOT_BLOCK * SLOT_BLOCK + N_EXPERTS * SLOT_BLOCK
    expert_ids = meta[:, 0:2]
    onehot = expert_ids[:, :, None] == jnp.arange(N_EXPERTS, dtype=I32)[None, None, :]
    dest = jnp.sum(jnp.where(onehot, pad_start[None, None, :], 0), axis=-1) + meta[:, 2:4]
    dest3 = dest.astype(I32).reshape(n // TROW, 1, 2 * TROW)
    block_lo = jnp.arange(n_slots // SLOT_BLOCK, dtype=I32) * SLOT_BLOCK
    block_expert = jnp.minimum(jnp.sum(pad_end[None, :] <= block_lo[:, None], axis=1),
                               N_EXPERTS - 1).astype(I32)
    n_blocks_used = (pad_end[-1:] // SLOT_BLOCK).astype(I32)
    fill_blk = ((block_lo + SLOT_BLOCK == pad_end[block_expert])
                | (block_lo >= pad_end[-1])).astype(I32)

    cand = jnp.where(padded > 0, jnp.arange(N_EXPERTS, dtype=I32), N_EXPERTS)
    later = jnp.concatenate([lax.cummin(cand, reverse=True)[1:],
                             jnp.full((1,), N_EXPERTS, I32)])
    next_expert = jnp.where(later < N_EXPERTS, later, -1)[block_expert].astype(I32)

    xd = _dispatch(fill_blk, dest3, h3, n_slots)
    yd = _experts(block_expert, n_blocks_used, next_expert, xd, w_gate_up, w_down)
    return _combine(dest3, x2, wts, row(g_final), yd)


def kernel(x, mem, positions, g_mix, w_in, b_gate, g_cq, w_uq, g_ckv, w_ukv, conv_w, w_branch_a,
           w_branch_b, w_out, g_xattn, g_mem, w_q_mem, w_kv_mem, w_o_mem, g_moe, w_router_group,
           b_router_group, w_router_expert, b_router_expert, w_gate_up, w_down, g_final):
    batch, seq, d = x.shape
    depth = g_mix.shape[0]
    assert d == D_MODEL and depth == 1 and seq % TM == 0 and seq % TQ == 0 and TQ == TK
    n = batch * seq
    x2d = x.reshape(n, d)
    pos2d = positions.reshape(n, 1).astype(I32)
    inv_freq = ROPE_THETA ** (-jnp.arange(0, QK_ROPE_DIM, 2, dtype=F32) / QK_ROPE_DIM)
    invf = _lanes([ROPE_LO, inv_freq.reshape(1, -1), inv_freq.reshape(1, -1)], HEAD_SLOT)
    l = 0
    km, vm = _mem_kv(mem.reshape(batch * MEM_LEN, d), g_mem[l].reshape(1, -1),
                     w_kv_mem[l].astype(BF16))
    out = _layer(x2d, pos2d, invf, km, vm, batch, seq, g_mix[l], w_in[l], b_gate[l], g_cq[l],
                 w_uq[l], g_ckv[l], w_ukv[l], conv_w[l], w_branch_a[l], w_branch_b[l], w_out[l],
                 g_xattn[l], w_q_mem[l], w_o_mem[l], g_moe[l], w_router_group[l],
                 b_router_group[l], w_router_expert[l], b_router_expert[l], w_gate_up[l],
                 w_down[l], g_final)
    return out.reshape(batch, seq, d)
```

```python
import functools
import math

import jax
import jax.numpy as jnp
from jax import lax
from jax.experimental import pallas as pl
from jax.experimental.pallas import tpu as pltpu

F32 = jnp.float32
BF16 = jnp.bfloat16
I32 = jnp.int32

D_MODEL = 1024
MEM_LEN = 256
MLA_HEADS = 8
QK_NOPE_DIM = 64
QK_ROPE_DIM = 32
V_HEAD_DIM = 64
Q_LORA_RANK = 256
KV_LORA_RANK = 128
ROPE_THETA = 10000.0
CONV_WIDTH = 512
XATTN_HEADS = 4
XATTN_HEAD_DIM = 128
N_GROUPS = 8
EXPERTS_PER_GROUP = 8
N_EXPERTS = N_GROUPS * EXPERTS_PER_GROUP
EXPERT_FF = 256
EPS = 1e-6

COL_CQ = Q_LORA_RANK
COL_CKV = COL_CQ + KV_LORA_RANK
COL_KROPE = COL_CKV + QK_ROPE_DIM
COL_CONV = COL_KROPE + 3 * CONV_WIDTH

LANES = 128
HEAD_SLOT = LANES
ROPE_LO = QK_NOPE_DIM
ROPE_HALF = QK_ROPE_DIM // 2
L_LANE = V_HEAD_DIM
SLOT_BLOCK = 256
ROUTE_LANE0 = N_GROUPS
NEG_BIG = -1e30

TM = 512
TQ = 512
TK = 512
META_ROWS = 8
POST_PARTS = 1
DIAG_STRIPS = 2
ATTN_CHAINS = 8
TROW = 256
ROW_GROUP = 8
VMEM_LIMIT = 56 * 1024 * 1024


def _rms(x, g):
    return x * lax.rsqrt(jnp.mean(x * x, axis=-1, keepdims=True) + EPS) * g


def _const_spec(shape):
    nd = len(shape)
    return pl.BlockSpec(shape, lambda *_: (0,) * nd)


def _params(sem):
    return pltpu.CompilerParams(dimension_semantics=sem, vmem_limit_bytes=VMEM_LIMIT)


def _in_proj_kernel(x_ref, pos_ref, invf_ref, g_mix_ref, w1_ref, w2_ref, w3_ref, b_gate_ref,
                    g_cq_ref, wqa_ref, wqb_ref, g_ckv_ref, wkk_ref, wkv_ref, conv_w_ref, wbb_ref,
                    q_ref, k_ref, v_ref, ga_ref, mixb_ref, carry_ref, *, tiles_per_seq):
    step = pl.program_id(0)
    tm = x_ref.shape[0]
    h = _rms(x_ref[...], g_mix_ref[...]).astype(BF16)

    lane = lax.broadcasted_iota(I32, (tm, HEAD_SLOT), 1)
    rope_lane = (lane >= ROPE_LO) & (lane < ROPE_LO + QK_ROPE_DIM)
    pos_rows = pos_ref[0].astype(F32)
    pos_tile = jnp.concatenate([jnp.broadcast_to(pos_rows[g:g + 1, :], (LANES, LANES))
                                for g in range(tm // LANES)], axis=0)
    row_lane = lax.broadcasted_iota(I32, (tm, LANES), 0) & (LANES - 1)
    pos_col = jnp.sum(jnp.where(lane == row_lane, pos_tile, 0.0), axis=1, keepdims=True)
    ang = pos_col * invf_ref[...]
    cos_t = jnp.where(rope_lane, jnp.cos(ang), 0.0)
    sin_t = jnp.where(rope_lane, jnp.sin(ang), 0.0)

    p1 = jnp.dot(h, w1_ref[...], preferred_element_type=F32)
    c_q = p1[:, :COL_CQ]
    c_kv = p1[:, COL_CQ:COL_CKV]
    kr_a = p1[:, COL_CKV:COL_CKV + HEAD_SLOT]
    kr_b = p1[:, COL_CKV + HEAD_SLOT:COL_CKV + 2 * HEAD_SLOT]
    k_rot = kr_a * cos_t + kr_b * sin_t

    scale = math.log2(math.e) / math.sqrt(QK_NOPE_DIM + QK_ROPE_DIM)
    q_cos = jnp.where(lane < ROPE_LO, scale, cos_t * scale)
    q_sin = sin_t * scale
    cqn = _rms(c_q, g_cq_ref[...]).astype(BF16)
    qa = jnp.dot(cqn, wqa_ref[...], preferred_element_type=F32)
    qb = jnp.dot(cqn, wqb_ref[...], preferred_element_type=F32)
    ckvn = _rms(c_kv, g_ckv_ref[...]).astype(BF16)
    kk = jnp.dot(ckvn, wkk_ref[...], preferred_element_type=F32)
    vv = jnp.dot(ckvn, wkv_ref[...], preferred_element_type=F32)
    ones_lane = jnp.where(lane == L_LANE, 1.0, 0.0)
    for hd in range(MLA_HEADS):
        sl = slice(hd * HEAD_SLOT, (hd + 1) * HEAD_SLOT)
        q_ref[:, sl] = (qa[:, sl] * q_cos + qb[:, sl] * q_sin).astype(BF16)
        k_ref[:, sl] = (kk[:, sl] + k_rot).astype(BF16)
        v_ref[:, sl] = (vv[:, sl] + ones_lane).astype(BF16)

    p2 = jnp.dot(h, w2_ref[...], preferred_element_type=F32)
    b_gate = p2[:, :CONV_WIDTH]
    u = p2[:, CONV_WIDTH:2 * CONV_WIDTH] * p2[:, 2 * CONV_WIDTH:]

    @pl.when(step % tiles_per_seq == 0)
    def _():
        carry_ref[...] = jnp.zeros_like(carry_ref)

    prev1 = carry_ref[7:8, :]
    prev2 = carry_ref[6:7, :]
    row = lax.broadcasted_iota(I32, (tm, CONV_WIDTH), 0)
    u1 = jnp.where(row == 0, prev1, pltpu.roll(u, 1, 0))
    u2 = jnp.where(row == 0, prev2, jnp.where(row == 1, prev1, pltpu.roll(u, 2, 0)))
    carry_ref[...] = u[tm - 8:, :]
    cw = conv_w_ref[...]
    conv = b_gate * (cw[0:1, :] * u2 + cw[1:2, :] * u1 + cw[2:3, :] * u)
    branch_b = jnp.dot(conv.astype(BF16), wbb_ref[...], preferred_element_type=F32)

    gates = jax.nn.sigmoid(jnp.dot(h, w3_ref[...], preferred_element_type=F32) + b_gate_ref[...])
    ga_ref[...] = gates[:, :D_MODEL].astype(BF16)
    mixb_ref[...] = (gates[:, D_MODEL:] * branch_b).astype(BF16)


def _in_proj(x2d, pos3d, invf, g_mix, w1, w2, w3, b_gate, g_cq, wqa, wqb, g_ckv, wkk, wkv,
             conv_w, wbb, seq):
    n = x2d.shape[0]
    row_spec = lambda c: pl.BlockSpec((TM, c), lambda i: (i, 0))
    consts = [invf, g_mix, w1, w2, w3, b_gate, g_cq, wqa, wqb, g_ckv, wkk, wkv, conv_w, wbb]
    kv_cols = MLA_HEADS * HEAD_SLOT
    return pl.pallas_call(
        functools.partial(_in_proj_kernel, tiles_per_seq=seq // TM),
        grid=(n // TM,),
        in_specs=[row_spec(D_MODEL), pl.BlockSpec((1, TM // LANES, LANES), lambda i: (i, 0, 0))]
        + [_const_spec(c.shape) for c in consts],
        out_specs=[row_spec(kv_cols), row_spec(kv_cols), row_spec(kv_cols),
                   row_spec(D_MODEL), row_spec(D_MODEL)],
        out_shape=[jax.ShapeDtypeStruct((n, kv_cols), BF16),
                   jax.ShapeDtypeStruct((n, kv_cols), BF16),
                   jax.ShapeDtypeStruct((n, kv_cols), BF16),
                   jax.ShapeDtypeStruct((n, D_MODEL), BF16),
                   jax.ShapeDtypeStruct((n, D_MODEL), BF16)],
        scratch_shapes=[pltpu.VMEM((8, CONV_WIDTH), F32)],
        compiler_params=_params(("arbitrary",)),
        name="in_proj",
    )(x2d, pos3d, *consts)


def _attn_kernel(q_ref, k_ref, v_ref, o_ref):
    qi = pl.program_id(1)
    tq = q_ref.shape[0]

    def block(carry, heads, col_start, ncols, row_lo, masked):
        out = []
        for c, hd in enumerate(heads):
            m, acc = carry[c]
            hsl = slice(hd * HEAD_SLOT, (hd + 1) * HEAD_SLOT)
            kb = k_ref[pl.ds(col_start, ncols), hsl]
            vb = v_ref[pl.ds(col_start, ncols), hsl]
            s = lax.dot_general(q_ref[row_lo:, hsl], kb, (((1,), (1,)), ((), ())),
                                preferred_element_type=F32)
            if masked:
                r = qi * tq + row_lo + lax.broadcasted_iota(I32, s.shape, 0)
                s = jnp.where(col_start + lax.broadcasted_iota(I32, s.shape, 1) <= r, s, NEG_BIG)
            m_hi, acc_hi = m[row_lo:], acc[row_lo:]
            m_new = jnp.maximum(m_hi, jnp.max(s, axis=1, keepdims=True))
            p = jnp.exp2(s - m_new).astype(BF16)
            acc_new = jnp.exp2(m_hi - m_new) * acc_hi + jnp.dot(p, vb, preferred_element_type=F32)
            if row_lo:
                m_new = jnp.concatenate([m[:row_lo], m_new], axis=0)
                acc_new = jnp.concatenate([acc[:row_lo], acc_new], axis=0)
            out.append((m_new, acc_new))
        return tuple(out)

    def full_step(j, carry, heads):
        return block(carry, heads, pl.multiple_of(j * TK, TK), TK, 0, False)

    lane = lax.broadcasted_iota(I32, (tq, HEAD_SLOT), 1)
    for g in range(MLA_HEADS // ATTN_CHAINS):
        heads = tuple(range(g * ATTN_CHAINS, (g + 1) * ATTN_CHAINS))
        carry = tuple((jnp.full((tq, 1), NEG_BIG, F32), jnp.zeros((tq, HEAD_SLOT), F32))
                      for _ in heads)
        carry = lax.fori_loop(0, qi, functools.partial(full_step, heads=heads), carry)
        for d in range(DIAG_STRIPS):
            col = pl.multiple_of(qi * TK + d * (TK // DIAG_STRIPS), TK // DIAG_STRIPS)
            carry = block(carry, heads, col, TK // DIAG_STRIPS, d * (tq // DIAG_STRIPS), True)
        outs = []
        for _, acc in carry:
            denom = jnp.sum(jnp.where(lane == L_LANE, acc, 0.0), axis=1, keepdims=True)
            outs.append(acc / denom)
        for c in range(0, ATTN_CHAINS, 2):
            hp = heads[c] // 2
            pair = jnp.where(lane < V_HEAD_DIM, outs[c], pltpu.roll(outs[c + 1], V_HEAD_DIM, 1))
            o_ref[:, hp * HEAD_SLOT:(hp + 1) * HEAD_SLOT] = pair.astype(BF16)


def _mla_attn(q, k, v, batch, seq):
    n = q.shape[0]
    nq = seq // TQ
    out_cols = MLA_HEADS * V_HEAD_DIM
    return pl.pallas_call(
        _attn_kernel,
        grid=(batch, nq),
        in_specs=[pl.BlockSpec((TQ, q.shape[1]), lambda b, i: (b * nq + i, 0)),
                  pl.BlockSpec((seq, k.shape[1]), lambda b, i: (b, 0)),
                  pl.BlockSpec((seq, v.shape[1]), lambda b, i: (b, 0))],
        out_specs=pl.BlockSpec((TQ, out_cols), lambda b, i: (b * nq + i, 0)),
        out_shape=jax.ShapeDtypeStruct((n, out_cols), BF16),
        compiler_params=_params(("arbitrary", "arbitrary")),
        name="mla_attn",
    )(q, k, v)


def _post_attn_kernel(x_ref, attn_ref, ga_ref, mixb_ref, wba_ref, wout_ref, g_x_ref, wq_ref,
                      mem_ref, g_mem_ref, wkv_ref, wo_ref, g_moe_ref, wr_ref, br_ref,
                      x2_ref, h3_ref, meta_ref, wts_ref, cnt_ref, run_ref, km_ref, vm_ref,
                      *, tiles_per_seq):
    step = pl.program_id(0)
    tm = x_ref.shape[0]
    xdim = XATTN_HEADS * XATTN_HEAD_DIM

    @pl.when(step == 0)
    def _():
        run_ref[...] = jnp.zeros_like(run_ref)

    @pl.when(step % tiles_per_seq == 0)
    def _():
        mem_n = _rms(mem_ref[...], g_mem_ref[...]).astype(BF16)
        kv = jnp.dot(mem_n, wkv_ref[...], preferred_element_type=F32)
        km_ref[...] = kv[:, :xdim].astype(BF16)
        vm_ref[...] = kv[:, xdim:].astype(BF16)

    def mixer_and_xattn(rows):
        a = jnp.dot(attn_ref[rows, :], wba_ref[...], preferred_element_type=F32)
        mixed = ga_ref[rows, :].astype(F32) * a + mixb_ref[rows, :].astype(F32)
        x1 = x_ref[rows, :] + jnp.dot(mixed.astype(BF16), wout_ref[...], preferred_element_type=F32)

        h2 = _rms(x1, g_x_ref[...]).astype(BF16)
        qm = jnp.dot(h2, wq_ref[...], preferred_element_type=F32) * (1.0 / math.sqrt(XATTN_HEAD_DIM))
        qm = qm.astype(BF16)
        heads = []
        for hd in range(XATTN_HEADS):
            sl = slice(hd * XATTN_HEAD_DIM, (hd + 1) * XATTN_HEAD_DIM)
            s = lax.dot_general(qm[:, sl], km_ref[:, sl], (((1,), (1,)), ((), ())),
                                preferred_element_type=F32)
            p = jnp.exp(s - jnp.max(s, axis=1, keepdims=True))
            o = jnp.dot(p.astype(BF16), vm_ref[:, sl], preferred_element_type=F32)
            heads.append((o / jnp.sum(p, axis=1, keepdims=True)).astype(BF16))
        o_all = jnp.concatenate(heads, axis=1)
        x2 = x1 + jnp.dot(o_all, wo_ref[...], preferred_element_type=F32)
        x2_ref[rows, :] = x2
        h3 = _rms(x2, g_moe_ref[...])
        g0, g1 = rows.start // ROW_GROUP, rows.stop // ROW_GROUP
        h3_ref[g0:g1] = h3.reshape(g1 - g0, ROW_GROUP, D_MODEL)
        return jnp.dot(h3.astype(BF16), wr_ref[...], preferred_element_type=F32) + br_ref[...]

    part = tm // POST_PARTS
    logits = jnp.concatenate([mixer_and_xattn(slice(c * part, (c + 1) * part))
                              for c in range(POST_PARTS)], axis=0)

    lane = lax.broadcasted_iota(I32, (tm, LANES), 1)
    lane_f = lane.astype(F32)

    def first_lane(hit):
        return jnp.min(jnp.where(hit, lane_f, float(LANES)), axis=1, keepdims=True)

    gl = jnp.where(lane < N_GROUPS, logits, -jnp.inf)
    gmax = jnp.max(gl, axis=1, keepdims=True)
    g_idx = first_lane(gl == gmax).astype(I32)
    p_group = 1.0 / jnp.sum(jnp.exp(gl - gmax), axis=1, keepdims=True)
    e_lo = ROUTE_LANE0 + g_idx * EXPERTS_PER_GROUP
    el = jnp.where((lane >= e_lo) & (lane < e_lo + EXPERTS_PER_GROUP), logits, -jnp.inf)
    v1 = jnp.max(el, axis=1, keepdims=True)
    i1 = first_lane(el == v1)
    el2 = jnp.where(lane_f == i1, -jnp.inf, el)
    v2 = jnp.max(el2, axis=1, keepdims=True)
    i2 = first_lane(el2 == v2)
    t = jnp.exp(v2 - v1)
    w0 = p_group / (1.0 + t)
    w1 = p_group * t / (1.0 + t)

    oh0 = lane_f == i1
    oh1 = lane_f == i2
    oh = (oh0 | oh1).astype(BF16)
    r_i = lax.broadcasted_iota(I32, (tm, tm), 0)
    c_i = lax.broadcasted_iota(I32, (tm, tm), 1)
    lower = (c_i < r_i).astype(BF16)
    before = jnp.dot(lower, oh, preferred_element_type=F32) + run_ref[...]
    rank0 = jnp.sum(jnp.where(oh0, before, 0.0), axis=1, keepdims=True)
    rank1 = jnp.sum(jnp.where(oh1, before, 0.0), axis=1, keepdims=True)
    run_ref[...] = run_ref[...] + jnp.sum(oh.astype(F32), axis=0, keepdims=True)
    cnt_ref[...] = jnp.broadcast_to(run_ref[...], cnt_ref.shape).astype(I32)

    meta = jnp.where(lane == 0, i1 - ROUTE_LANE0, 0.0)
    meta = jnp.where(lane == 1, i2 - ROUTE_LANE0, meta)
    meta = jnp.where(lane == 2, rank0, meta)
    meta = jnp.where(lane == 3, rank1, meta)
    meta_ref[...] = meta.T[:META_ROWS, :]
    wts_ref[...] = jnp.where(lane == 0, w0, jnp.where(lane == 1, w1, 0.0))


def _post_attn(x2d, attn, ga, mixb, wba, wout, g_x, wq, mem2d, g_mem, wkv, wo, g_moe, wr, br, seq):
    n = x2d.shape[0]
    tps = seq // TM
    xdim = XATTN_HEADS * XATTN_HEAD_DIM
    row_spec = lambda c: pl.BlockSpec((TM, c), lambda i: (i, 0))
    consts = lambda *arrs: [_const_spec(a.shape) for a in arrs]
    return pl.pallas_call(
        functools.partial(_post_attn_kernel, tiles_per_seq=tps),
        grid=(n // TM,),
        in_specs=([row_spec(D_MODEL), row_spec(MLA_HEADS * V_HEAD_DIM), row_spec(D_MODEL),
                   row_spec(D_MODEL)] + consts(wba, wout, g_x, wq)
                  + [pl.BlockSpec((MEM_LEN, D_MODEL), lambda i: (i // tps, 0))]
                  + consts(g_mem, wkv, wo, g_moe, wr, br)),
        out_specs=[row_spec(D_MODEL),
                   pl.BlockSpec((TM // ROW_GROUP, ROW_GROUP, D_MODEL), lambda i: (i, 0, 0)),
                   pl.BlockSpec((META_ROWS, TM), lambda i: (0, i)), row_spec(LANES),
                   _const_spec((8, LANES))],
        out_shape=[jax.ShapeDtypeStruct((n, D_MODEL), F32),
                   jax.ShapeDtypeStruct((n // ROW_GROUP, ROW_GROUP, D_MODEL), F32),
                   jax.ShapeDtypeStruct((META_ROWS, n), F32),
                   jax.ShapeDtypeStruct((n, LANES), F32),
                   jax.ShapeDtypeStruct((8, LANES), I32)],
        scratch_shapes=[pltpu.VMEM((1, LANES), F32), pltpu.VMEM((MEM_LEN, xdim), BF16),
                        pltpu.VMEM((MEM_LEN, xdim), BF16)],
        compiler_params=_params(("arbitrary",)),
        name="post_attn",
    )(x2d, attn, ga, mixb, wba, wout, g_x, wq, mem2d, g_mem, wkv, wo, g_moe, wr, br)


def _tile_row(ref, group, k):
    return ref.at[group, pl.ds(k, 1), :]


def _hbm_row(ref, row):
    return ref.at[pl.ds(row, 1), :]


def _dispatch_kernel(fill_ref, dest0_ref, dest1_ref, zeros_ref, h3_ref, xd_ref, row_sem, fill_sem):
    i = pl.program_id(0)

    def fill_copy(b):
        return pltpu.make_async_copy(
            zeros_ref, xd_ref.at[pl.ds(pl.multiple_of(b * SLOT_BLOCK, SLOT_BLOCK), SLOT_BLOCK)],
            fill_sem)

    @pl.when(i == 0)
    def _():
        def start(b, c):
            @pl.when(fill_ref[b] > 0)
            def _():
                fill_copy(b).start()
            return c

        def wait(b, c):
            @pl.when(fill_ref[b] > 0)
            def _():
                fill_copy(b).wait()
            return c

        lax.fori_loop(0, fill_ref.shape[0], start, 0)
        lax.fori_loop(0, fill_ref.shape[0], wait, 0)

    def copies(g, k):
        r = g * ROW_GROUP + k
        src = _tile_row(h3_ref, g, k)
        return (pltpu.make_async_copy(src, _hbm_row(xd_ref, dest0_ref[0, 0, r]), row_sem),
                pltpu.make_async_copy(src, _hbm_row(xd_ref, dest1_ref[0, 0, r]), row_sem))

    def issue(g, c):
        for k in range(ROW_GROUP):
            for cp in copies(g, k):
                cp.start()
        return c

    def drain(g, c):
        for k in range(ROW_GROUP):
            for cp in copies(g, k):
                cp.wait()
        return c

    lax.fori_loop(0, TROW // ROW_GROUP, issue, 0)
    lax.fori_loop(0, TROW // ROW_GROUP, drain, 0)


def _dispatch(fill_blk, dest0, dest1, h3, n_slots):
    groups, _, d = h3.shape
    n = groups * ROW_GROUP
    grid_spec = pltpu.PrefetchScalarGridSpec(
        num_scalar_prefetch=1,
        grid=(n // TROW,),
        in_specs=[pl.BlockSpec((1, 1, TROW), lambda i, fb: (i, 0, 0), memory_space=pltpu.SMEM),
                  pl.BlockSpec((1, 1, TROW), lambda i, fb: (i, 0, 0), memory_space=pltpu.SMEM),
                  pl.BlockSpec((SLOT_BLOCK, d), lambda i, fb: (0, 0)),
                  pl.BlockSpec((TROW // ROW_GROUP, ROW_GROUP, d), lambda i, fb: (i, 0, 0))],
        out_specs=pl.BlockSpec(memory_space=pl.ANY),
        scratch_shapes=[pltpu.SemaphoreType.DMA, pltpu.SemaphoreType.DMA],
    )
    return pl.pallas_call(
        _dispatch_kernel,
        grid_spec=grid_spec,
        out_shape=jax.ShapeDtypeStruct((n_slots, d), F32),
        compiler_params=_params(("arbitrary",)),
        name="dispatch",
    )(fill_blk, dest0, dest1, jnp.zeros((SLOT_BLOCK, d), F32), h3)


def _experts_kernel(be_ref, nb_ref, nxt_ref, xd_ref, wgu_hbm, wdn_hbm, yd_ref,
                    wgu_f32, wdn_f32, wgu_bf, wdn_bf, slot_ref, sem):
    i = pl.program_id(0)

    def weight_copies(e, slot):
        return (pltpu.make_async_copy(wgu_hbm.at[e], wgu_f32.at[slot], sem.at[slot]),
                pltpu.make_async_copy(wdn_hbm.at[e], wdn_f32.at[slot], sem.at[slot]))

    @pl.when(i == 0)
    def _():
        slot_ref[0] = 0
        for cp in weight_copies(be_ref[0], 0):
            cp.start()

    @pl.when(i < nb_ref[0])
    def _():
        @pl.when((i == 0) | (be_ref[i] != be_ref[jnp.maximum(i - 1, 0)]))
        def _():
            slot = slot_ref[0]
            for cp in weight_copies(be_ref[i], slot):
                cp.wait()
            wgu_bf[...] = wgu_f32[slot].astype(BF16)
            wdn_bf[...] = wdn_f32[slot].astype(BF16)

            @pl.when(nxt_ref[i] >= 0)
            def _():
                for cp in weight_copies(nxt_ref[i], 1 - slot):
                    cp.start()

            slot_ref[0] = 1 - slot

        gu = jnp.dot(xd_ref[...].astype(BF16), wgu_bf[...], preferred_element_type=F32)
        gate = gu[:, :EXPERT_FF]
        up = gu[:, EXPERT_FF:]
        act = (gate * jax.nn.sigmoid(gate) * up).astype(BF16)
        yd_ref[...] = jnp.dot(act, wdn_bf[...], preferred_element_type=F32)


def _experts(block_expert, n_blocks_used, next_expert, xd, w_gate_up, w_down):
    n_slots, d = xd.shape
    slot_map = lambda i, be, nb, nx: (jnp.minimum(i, nb[0] - 1), 0)
    grid_spec = pltpu.PrefetchScalarGridSpec(
        num_scalar_prefetch=3,
        grid=(n_slots // SLOT_BLOCK,),
        in_specs=[pl.BlockSpec((SLOT_BLOCK, d), slot_map),
                  pl.BlockSpec(memory_space=pl.ANY),
                  pl.BlockSpec(memory_space=pl.ANY)],
        out_specs=pl.BlockSpec((SLOT_BLOCK, d), slot_map),
        scratch_shapes=[pltpu.VMEM((2, D_MODEL, 2 * EXPERT_FF), F32),
                        pltpu.VMEM((2, EXPERT_FF, D_MODEL), F32),
                        pltpu.VMEM((D_MODEL, 2 * EXPERT_FF), BF16),
                        pltpu.VMEM((EXPERT_FF, D_MODEL), BF16),
                        pltpu.SMEM((1,), I32),
                        pltpu.SemaphoreType.DMA((2,))],
    )
    return pl.pallas_call(
        _experts_kernel,
        grid_spec=grid_spec,
        out_shape=jax.ShapeDtypeStruct(xd.shape, F32),
        input_output_aliases={3: 0},
        compiler_params=_params(("arbitrary",)),
        name="experts",
    )(block_expert, n_blocks_used, next_expert, xd, w_gate_up, w_down)


def _combine_kernel(dest0_ref, dest1_ref, next0_ref, next1_ref, x2_ref, wts_ref, g_ref, yd_ref, o_ref,
                    buf, sem):
    i = pl.program_id(0)
    last = pl.num_programs(0) - 1
    rows = x2_ref.shape[0]
    slot = i % 2

    def copies(d_refs, slt, g, k):
        r = g * ROW_GROUP + k
        return tuple(pltpu.make_async_copy(_hbm_row(yd_ref, d_ref[0, 0, r]),
                                           _tile_row(buf.at[slt, j], g, k), sem.at[slt])
                     for j, d_ref in enumerate(d_refs))

    def issue(d_ref, slt):
        def body(g, c):
            for k in range(ROW_GROUP):
                for cp in copies(d_ref, slt, g, k):
                    cp.start()
            return c
        lax.fori_loop(0, rows // ROW_GROUP, body, 0)

    @pl.when(i == 0)
    def _():
        issue((dest0_ref, dest1_ref), 0)

    @pl.when(i < last)
    def _():
        issue((next0_ref, next1_ref), 1 - slot)

    def drain(g, c):
        for k in range(ROW_GROUP):
            for cp in copies((dest0_ref, dest1_ref), slot, g, k):
                cp.wait()
        return c

    lax.fori_loop(0, rows // ROW_GROUP, drain, 0)

    lane = lax.broadcasted_iota(I32, (rows, LANES), 1)
    wts = wts_ref[...]
    w0 = jnp.sum(jnp.where(lane == 0, wts, 0.0), axis=1, keepdims=True)
    w1 = jnp.sum(jnp.where(lane == 1, wts, 0.0), axis=1, keepdims=True)
    y0 = buf[slot, 0].reshape(rows, D_MODEL)
    y1 = buf[slot, 1].reshape(rows, D_MODEL)
    o_ref[...] = _rms(x2_ref[...] + y0 * w0 + y1 * w1, g_ref[...])


def _combine(dest0, dest1, x2, wts, g_final, yd):
    n = x2.shape[0]
    steps = n // TROW
    dest_spec = lambda f: pl.BlockSpec((1, 1, TROW), f, memory_space=pltpu.SMEM)
    cur = lambda i: (i, 0, 0)
    nxt = lambda i: (jnp.minimum(i + 1, steps - 1), 0, 0)
    return pl.pallas_call(
        _combine_kernel,
        grid=(steps,),
        in_specs=[dest_spec(cur), dest_spec(cur), dest_spec(nxt), dest_spec(nxt),
                  pl.BlockSpec((TROW, D_MODEL), lambda i: (i, 0)),
                  pl.BlockSpec((TROW, LANES), lambda i: (i, 0)),
                  _const_spec((1, D_MODEL)),
                  pl.BlockSpec(memory_space=pl.ANY)],
        out_specs=pl.BlockSpec((TROW, D_MODEL), lambda i: (i, 0)),
        out_shape=jax.ShapeDtypeStruct((n, D_MODEL), F32),
        scratch_shapes=[pltpu.VMEM((2, 2, TROW // ROW_GROUP, ROW_GROUP, D_MODEL), F32),
                        pltpu.SemaphoreType.DMA((2,))],
        compiler_params=_params(("arbitrary",)),
        name="combine",
    )(dest0, dest1, dest0, dest1, x2, wts, g_final, yd)


def _lanes(parts, width):
    ref = next(p for p in parts if not isinstance(p, int))
    cols = [jnp.zeros(ref.shape[:-1] + (p,), ref.dtype) if isinstance(p, int) else p for p in parts]
    used = sum(c.shape[-1] for c in cols)
    assert used <= width
    if used < width:
        cols.append(jnp.zeros(ref.shape[:-1] + (width - used,), ref.dtype))
    return jnp.concatenate(cols, axis=-1)


def _head_slots(w, per_head, pieces):
    rows = w.shape[0]
    wh = w.reshape(rows, MLA_HEADS, per_head)
    return _lanes(pieces(wh), HEAD_SLOT).reshape(rows, MLA_HEADS * HEAD_SLOT)


def _layer(x2d, pos3d, invf, mem2d, g_mem, w_kv_mem, batch, seq, g_mix, w_in, b_gate, g_cq, w_uq, g_ckv, w_ukv,
           conv_w, w_branch_a, w_branch_b, w_out, g_xattn, w_q_mem, w_o_mem, g_moe,
           w_router_group, b_router_group, w_router_expert, b_router_expert, w_gate_up, w_down,
           g_final):
    n = x2d.shape[0]
    row = lambda v: v.reshape(1, -1)
    lo, mid, hi = ROPE_LO, ROPE_LO + ROPE_HALF, ROPE_LO + QK_ROPE_DIM

    w_kr = w_in[:, COL_CKV:COL_KROPE]
    kr_a = _lanes([lo, w_kr], HEAD_SLOT)
    kr_b = _lanes([lo, -w_kr[:, ROPE_HALF:], w_kr[:, :ROPE_HALF]], HEAD_SLOT)
    w1 = jnp.concatenate([w_in[:, :COL_CKV], kr_a, kr_b], axis=1).astype(BF16)
    w2 = w_in[:, COL_KROPE:COL_CONV].astype(BF16)
    w3 = w_in[:, COL_CONV:].astype(BF16)
    qd = QK_NOPE_DIM + QK_ROPE_DIM
    wqa = _head_slots(w_uq, qd, lambda wh: [wh]).astype(BF16)
    wqb = _head_slots(w_uq, qd, lambda wh: [lo, -wh[..., mid:hi], wh[..., lo:mid]]).astype(BF16)
    kvd = QK_NOPE_DIM + V_HEAD_DIM
    wkk = _head_slots(w_ukv, kvd, lambda wh: [wh[..., :QK_NOPE_DIM]]).astype(BF16)
    wkv = _head_slots(w_ukv, kvd, lambda wh: [wh[..., QK_NOPE_DIM:]]).astype(BF16)
    conv_w8 = jnp.concatenate([conv_w, jnp.zeros((8 - conv_w.shape[0], CONV_WIDTH), F32)], axis=0)

    q, k, v, ga, mixb = _in_proj(x2d, pos3d, invf, row(g_mix), w1, w2, w3, row(b_gate), row(g_cq),
                                 wqa, wqb, row(g_ckv), wkk, wkv, conv_w8,
                                 w_branch_b.astype(BF16), seq)
    attn = _mla_attn(q, k, v, batch, seq)

    wr = _lanes([w_router_group, w_router_expert], LANES).astype(BF16)
    br = _lanes([row(b_router_group), row(b_router_expert)], LANES)
    x2, h3, meta, wts, cnt = _post_attn(x2d, attn, ga, mixb, w_branch_a.astype(BF16),
                                   w_out.astype(BF16), row(g_xattn), w_q_mem.astype(BF16), mem2d, row(g_mem),
                                   w_kv_mem.astype(BF16),
                                   w_o_mem.astype(BF16), row(g_moe), wr, br, seq)

    counts = cnt[0, ROUTE_LANE0:ROUTE_LANE0 + N_EXPERTS]
    padded = (counts + SLOT_BLOCK - 1) // SLOT_BLOCK * SLOT_BLOCK
    pad_end = jnp.cumsum(padded)
    pad_start = pad_end - padded
    n_slots = (2 * n + SLOT_BLOCK - 1) // SLOT_BLOCK * SLOT_BLOCK + N_EXPERTS * SLOT_BLOCK
    routed = meta[0:4].astype(I32)
    dest = pad_start[routed[0:2]] + routed[2:4]
    dest0 = dest[0].reshape(n // TROW, 1, TROW)
    dest1 = dest[1].reshape(n // TROW, 1, TROW)
    block_lo = jnp.arange(n_slots // SLOT_BLOCK, dtype=I32) * SLOT_BLOCK
    block_expert = jnp.minimum(jnp.sum(pad_end[None, :] <= block_lo[:, None], axis=1),
                               N_EXPERTS - 1).astype(I32)
    n_blocks_used = (pad_end[-1:] // SLOT_BLOCK).astype(I32)
    fill_blk = ((block_lo + SLOT_BLOCK == pad_end[block_expert])
                | (block_lo >= pad_end[-1])).astype(I32)

    cand = jnp.where(padded > 0, jnp.arange(N_EXPERTS, dtype=I32), N_EXPERTS)
    later = jnp.concatenate([lax.cummin(cand, reverse=True)[1:],
                             jnp.full((1,), N_EXPERTS, I32)])
    next_expert = jnp.where(later < N_EXPERTS, later, -1)[block_expert].astype(I32)

    xd = _dispatch(fill_blk, dest0, dest1, h3, n_slots)
    yd = _experts(block_expert, n_blocks_used, next_expert, xd, w_gate_up, w_down)
    return _combine(dest0, dest1, x2, wts, row(g_final), yd)


def kernel(x, mem, positions, g_mix, w_in, b_gate, g_cq, w_uq, g_ckv, w_ukv, conv_w, w_branch_a,
           w_branch_b, w_out, g_xattn, g_mem, w_q_mem, w_kv_mem, w_o_mem, g_moe, w_router_group,
           b_router_group, w_router_expert, b_router_expert, w_gate_up, w_down, g_final):
    batch, seq, d = x.shape
    depth = g_mix.shape[0]
    assert d == D_MODEL and depth == 1 and seq % TM == 0 and seq % TQ == 0 and TQ == TK
    n = batch * seq
    x2d = x.reshape(n, d)
    pos3d = positions.astype(I32).reshape(n // TM, TM // LANES, LANES)
    inv_freq = ROPE_THETA ** (-jnp.arange(0, QK_ROPE_DIM, 2, dtype=F32) / QK_ROPE_DIM)
    invf = _lanes([ROPE_LO, inv_freq.reshape(1, -1), inv_freq.reshape(1, -1)], HEAD_SLOT)
    l = 0
    out = _layer(x2d, pos3d, invf, mem.reshape(batch * MEM_LEN, d), g_mem[l], w_kv_mem[l], batch,
                 seq, g_mix[l], w_in[l], b_gate[l], g_cq[l],
                 w_uq[l], g_ckv[l], w_ukv[l], conv_w[l], w_branch_a[l], w_branch_b[l], w_out[l],
                 g_xattn[l], w_q_mem[l], w_o_mem[l], g_moe[l], w_router_group[l],
                 b_router_group[l], w_router_expert[l], b_router_expert[l], w_gate_up[l],
                 w_down[l], g_final)
    return out.reshape(batch, seq, d)
```

```python
import functools
import math

import jax
import jax.numpy as jnp
from jax import lax
from jax.experimental import pallas as pl
from jax.experimental.pallas import tpu as pltpu

F32 = jnp.float32
BF16 = jnp.bfloat16
I32 = jnp.int32

D_MODEL = 1024
MEM_LEN = 256
MLA_HEADS = 8
QK_NOPE_DIM = 64
QK_ROPE_DIM = 32
V_HEAD_DIM = 64
Q_LORA_RANK = 256
KV_LORA_RANK = 128
ROPE_THETA = 10000.0
CONV_WIDTH = 512
XATTN_HEADS = 4
XATTN_HEAD_DIM = 128
N_GROUPS = 8
EXPERTS_PER_GROUP = 8
N_EXPERTS = N_GROUPS * EXPERTS_PER_GROUP
EXPERT_FF = 256
EPS = 1e-6

COL_CQ = Q_LORA_RANK
COL_CKV = COL_CQ + KV_LORA_RANK
COL_KROPE = COL_CKV + QK_ROPE_DIM
COL_CONV = COL_KROPE + 3 * CONV_WIDTH

LANES = 128
HEAD_SLOT = LANES
ROPE_LO = QK_NOPE_DIM
ROPE_HALF = QK_ROPE_DIM // 2
L_LANE = V_HEAD_DIM
SLOT_BLOCK = 256
ROUTE_LANE0 = N_GROUPS
NEG_BIG = -1e30

TM = 512
TQ = 512
TK = 512
META_ROWS = 8
POST_PARTS = 1
DIAG_STRIPS = 2
ATTN_CHAINS = 8
TROW = 256
ROW_GROUP = 8
VMEM_LIMIT = 56 * 1024 * 1024


def _rms(x, g):
    return x * lax.rsqrt(jnp.mean(x * x, axis=-1, keepdims=True) + EPS) * g


def _const_spec(shape):
    nd = len(shape)
    return pl.BlockSpec(shape, lambda *_: (0,) * nd)


def _params(sem):
    return pltpu.CompilerParams(dimension_semantics=sem, vmem_limit_bytes=VMEM_LIMIT)


def _in_proj_kernel(x_ref, pos_ref, invf_ref, g_mix_ref, w1_ref, w2_ref, w3_ref, b_gate_ref,
                    g_cq_ref, wq_ref, g_ckv_ref, wkk_ref, wkv_ref, conv_w_ref, wbb_ref,
                    q_ref, k_ref, v_ref, ga_ref, mixb_ref, carry_ref, *, tiles_per_seq):
    step = pl.program_id(0)
    tm = x_ref.shape[0]
    h = _rms(x_ref[...], g_mix_ref[...]).astype(BF16)

    lane = lax.broadcasted_iota(I32, (tm, HEAD_SLOT), 1)
    rope_lane = (lane >= ROPE_LO) & (lane < ROPE_LO + QK_ROPE_DIM)
    pos_rows = pos_ref[0].astype(F32)
    pos_tile = jnp.concatenate([jnp.broadcast_to(pos_rows[g:g + 1, :], (LANES, LANES))
                                for g in range(tm // LANES)], axis=0)
    row_lane = lax.broadcasted_iota(I32, (tm, LANES), 0) & (LANES - 1)
    pos_col = jnp.sum(jnp.where(lane == row_lane, pos_tile, 0.0), axis=1, keepdims=True)
    ang = pos_col * invf_ref[...]
    first_half = lane < ROPE_LO + ROPE_HALF
    cos_t = jnp.where(rope_lane, jnp.cos(ang), 0.0)
    sin_t = jnp.where(rope_lane, jnp.sin(ang), 0.0)
    sin_t = jnp.where(first_half, -sin_t, sin_t)

    def swap_halves(t):
        return jnp.where(first_half, pltpu.roll(t, HEAD_SLOT - ROPE_HALF, 1),
                         pltpu.roll(t, ROPE_HALF, 1))

    p1 = jnp.dot(h, w1_ref[...], preferred_element_type=F32)
    c_q = p1[:, :COL_CQ]
    c_kv = p1[:, COL_CQ:COL_CKV]
    kr = p1[:, COL_CKV:COL_CKV + HEAD_SLOT]
    k_rot = kr * cos_t + swap_halves(kr) * sin_t

    scale = math.log2(math.e) / math.sqrt(QK_NOPE_DIM + QK_ROPE_DIM)
    q_cos = jnp.where(lane < ROPE_LO, scale, cos_t * scale)
    q_sin = sin_t * scale
    cqn = _rms(c_q, g_cq_ref[...]).astype(BF16)
    qq = jnp.dot(cqn, wq_ref[...], preferred_element_type=F32)
    ckvn = _rms(c_kv, g_ckv_ref[...]).astype(BF16)
    kk = jnp.dot(ckvn, wkk_ref[...], preferred_element_type=F32)
    vv = jnp.dot(ckvn, wkv_ref[...], preferred_element_type=F32)
    ones_lane = jnp.where(lane == L_LANE, 1.0, 0.0)
    for hd in range(MLA_HEADS):
        sl = slice(hd * HEAD_SLOT, (hd + 1) * HEAD_SLOT)
        q_ref[:, sl] = (qq[:, sl] * q_cos + swap_halves(qq[:, sl]) * q_sin).astype(BF16)
        k_ref[:, sl] = (kk[:, sl] + k_rot).astype(BF16)
        v_ref[:, sl] = (vv[:, sl] + ones_lane).astype(BF16)

    p2 = jnp.dot(h, w2_ref[...], preferred_element_type=F32)
    b_gate = p2[:, :CONV_WIDTH]
    u = p2[:, CONV_WIDTH:2 * CONV_WIDTH] * p2[:, 2 * CONV_WIDTH:]

    @pl.when(step % tiles_per_seq == 0)
    def _():
        carry_ref[...] = jnp.zeros_like(carry_ref)

    prev1 = carry_ref[7:8, :]
    prev2 = carry_ref[6:7, :]
    row = lax.broadcasted_iota(I32, (tm, CONV_WIDTH), 0)
    u1 = jnp.where(row == 0, prev1, pltpu.roll(u, 1, 0))
    u2 = jnp.where(row == 0, prev2, jnp.where(row == 1, prev1, pltpu.roll(u, 2, 0)))
    carry_ref[...] = u[tm - 8:, :]
    cw = conv_w_ref[...]
    conv = b_gate * (cw[0:1, :] * u2 + cw[1:2, :] * u1 + cw[2:3, :] * u)
    branch_b = jnp.dot(conv.astype(BF16), wbb_ref[...], preferred_element_type=F32)

    gates = jax.nn.sigmoid(jnp.dot(h, w3_ref[...], preferred_element_type=F32) + b_gate_ref[...])
    ga_ref[...] = gates[:, :D_MODEL].astype(BF16)
    mixb_ref[...] = (gates[:, D_MODEL:] * branch_b).astype(BF16)


def _in_proj(x2d, pos3d, invf, g_mix, w1, w2, w3, b_gate, g_cq, wq, g_ckv, wkk, wkv,
             conv_w, wbb, seq):
    n = x2d.shape[0]
    row_spec = lambda c: pl.BlockSpec((TM, c), lambda i: (i, 0))
    consts = [invf, g_mix, w1, w2, w3, b_gate, g_cq, wq, g_ckv, wkk, wkv, conv_w, wbb]
    kv_cols = MLA_HEADS * HEAD_SLOT
    return pl.pallas_call(
        functools.partial(_in_proj_kernel, tiles_per_seq=seq // TM),
        grid=(n // TM,),
        in_specs=[row_spec(D_MODEL), pl.BlockSpec((1, TM // LANES, LANES), lambda i: (i, 0, 0))]
        + [_const_spec(c.shape) for c in consts],
        out_specs=[row_spec(kv_cols), row_spec(kv_cols), row_spec(kv_cols),
                   row_spec(D_MODEL), row_spec(D_MODEL)],
        out_shape=[jax.ShapeDtypeStruct((n, kv_cols), BF16),
                   jax.ShapeDtypeStruct((n, kv_cols), BF16),
                   jax.ShapeDtypeStruct((n, kv_cols), BF16),
                   jax.ShapeDtypeStruct((n, D_MODEL), BF16),
                   jax.ShapeDtypeStruct((n, D_MODEL), BF16)],
        scratch_shapes=[pltpu.VMEM((8, CONV_WIDTH), F32)],
        compiler_params=_params(("arbitrary",)),
        name="in_proj",
    )(x2d, pos3d, *consts)


def _attn_kernel(q_ref, k_ref, v_ref, o_ref):
    qi = pl.program_id(1)
    tq = q_ref.shape[0]

    def block(carry, heads, col_start, ncols, row_lo, masked):
        out = []
        for c, hd in enumerate(heads):
            m, acc = carry[c]
            hsl = slice(hd * HEAD_SLOT, (hd + 1) * HEAD_SLOT)
            kb = k_ref[pl.ds(col_start, ncols), hsl]
            vb = v_ref[pl.ds(col_start, ncols), hsl]
            s = lax.dot_general(q_ref[row_lo:, hsl], kb, (((1,), (1,)), ((), ())),
                                preferred_element_type=F32)
            if masked:
                r = qi * tq + row_lo + lax.broadcasted_iota(I32, s.shape, 0)
                s = jnp.where(col_start + lax.broadcasted_iota(I32, s.shape, 1) <= r, s, NEG_BIG)
            m_hi, acc_hi = m[row_lo:], acc[row_lo:]
            m_new = jnp.maximum(m_hi, jnp.max(s, axis=1, keepdims=True))
            p = jnp.exp2(s - m_new).astype(BF16)
            acc_new = jnp.exp2(m_hi - m_new) * acc_hi + jnp.dot(p, vb, preferred_element_type=F32)
            if row_lo:
                m_new = jnp.concatenate([m[:row_lo], m_new], axis=0)
                acc_new = jnp.concatenate([acc[:row_lo], acc_new], axis=0)
            out.append((m_new, acc_new))
        return tuple(out)

    def full_step(j, carry, heads):
        return block(carry, heads, pl.multiple_of(j * TK, TK), TK, 0, False)

    lane = lax.broadcasted_iota(I32, (tq, HEAD_SLOT), 1)
    for g in range(MLA_HEADS // ATTN_CHAINS):
        heads = tuple(range(g * ATTN_CHAINS, (g + 1) * ATTN_CHAINS))
        carry = tuple((jnp.full((tq, 1), NEG_BIG, F32), jnp.zeros((tq, HEAD_SLOT), F32))
                      for _ in heads)
        carry = lax.fori_loop(0, qi, functools.partial(full_step, heads=heads), carry)
        for d in range(DIAG_STRIPS):
            col = pl.multiple_of(qi * TK + d * (TK // DIAG_STRIPS), TK // DIAG_STRIPS)
            carry = block(carry, heads, col, TK // DIAG_STRIPS, d * (tq // DIAG_STRIPS), True)
        outs = []
        for _, acc in carry:
            denom = jnp.sum(jnp.where(lane == L_LANE, acc, 0.0), axis=1, keepdims=True)
            outs.append(acc / denom)
        for c in range(0, ATTN_CHAINS, 2):
            hp = heads[c] // 2
            pair = jnp.where(lane < V_HEAD_DIM, outs[c], pltpu.roll(outs[c + 1], V_HEAD_DIM, 1))
            o_ref[:, hp * HEAD_SLOT:(hp + 1) * HEAD_SLOT] = pair.astype(BF16)


def _mla_attn(q, k, v, batch, seq):
    n = q.shape[0]
    nq = seq // TQ
    out_cols = MLA_HEADS * V_HEAD_DIM
    return pl.pallas_call(
        _attn_kernel,
        grid=(batch, nq),
        in_specs=[pl.BlockSpec((TQ, q.shape[1]), lambda b, i: (b * nq + i, 0)),
                  pl.BlockSpec((seq, k.shape[1]), lambda b, i: (b, 0)),
                  pl.BlockSpec((seq, v.shape[1]), lambda b, i: (b, 0))],
        out_specs=pl.BlockSpec((TQ, out_cols), lambda b, i: (b * nq + i, 0)),
        out_shape=jax.ShapeDtypeStruct((n, out_cols), BF16),
        compiler_params=_params(("arbitrary", "arbitrary")),
        name="mla_attn",
    )(q, k, v)


def _post_attn_kernel(x_ref, attn_ref, ga_ref, mixb_ref, wba_ref, wout_ref, g_x_ref, wq_ref,
                      mem_ref, g_mem_ref, wkv_ref, wo_ref, g_moe_ref, wr_ref, br_ref,
                      x2_ref, h3_ref, meta_ref, wts_ref, cnt_ref, run_ref, km_ref, vm_ref,
                      *, tiles_per_seq):
    step = pl.program_id(0)
    tm = x_ref.shape[0]
    xdim = XATTN_HEADS * XATTN_HEAD_DIM

    @pl.when(step == 0)
    def _():
        run_ref[...] = jnp.zeros_like(run_ref)

    @pl.when(step % tiles_per_seq == 0)
    def _():
        mem_n = _rms(mem_ref[...], g_mem_ref[...]).astype(BF16)
        kv = jnp.dot(mem_n, wkv_ref[...], preferred_element_type=F32)
        km_ref[...] = kv[:, :xdim].astype(BF16)
        vm_ref[...] = kv[:, xdim:].astype(BF16)

    def mixer_and_xattn(rows):
        a = jnp.dot(attn_ref[rows, :], wba_ref[...], preferred_element_type=F32)
        mixed = ga_ref[rows, :].astype(F32) * a + mixb_ref[rows, :].astype(F32)
        x1 = x_ref[rows, :] + jnp.dot(mixed.astype(BF16), wout_ref[...], preferred_element_type=F32)

        h2 = _rms(x1, g_x_ref[...]).astype(BF16)
        qm = jnp.dot(h2, wq_ref[...], preferred_element_type=F32) * (1.0 / math.sqrt(XATTN_HEAD_DIM))
        qm = qm.astype(BF16)
        heads = []
        for hd in range(XATTN_HEADS):
            sl = slice(hd * XATTN_HEAD_DIM, (hd + 1) * XATTN_HEAD_DIM)
            s = lax.dot_general(qm[:, sl], km_ref[:, sl], (((1,), (1,)), ((), ())),
                                preferred_element_type=F32)
            p = jnp.exp(s - jnp.max(s, axis=1, keepdims=True))
            o = jnp.dot(p.astype(BF16), vm_ref[:, sl], preferred_element_type=F32)
            heads.append((o / jnp.sum(p, axis=1, keepdims=True)).astype(BF16))
        o_all = jnp.concatenate(heads, axis=1)
        x2 = x1 + jnp.dot(o_all, wo_ref[...], preferred_element_type=F32)
        x2_ref[rows, :] = x2
        h3 = _rms(x2, g_moe_ref[...])
        g0, g1 = rows.start // ROW_GROUP, rows.stop // ROW_GROUP
        h3_ref[g0:g1] = h3.reshape(g1 - g0, ROW_GROUP, D_MODEL)
        return jnp.dot(h3.astype(BF16), wr_ref[...], preferred_element_type=F32) + br_ref[...]

    part = tm // POST_PARTS
    logits = jnp.concatenate([mixer_and_xattn(slice(c * part, (c + 1) * part))
                              for c in range(POST_PARTS)], axis=0)

    lane = lax.broadcasted_iota(I32, (tm, LANES), 1)
    lane_f = lane.astype(F32)

    def first_lane(hit):
        return jnp.min(jnp.where(hit, lane_f, float(LANES)), axis=1, keepdims=True)

    gl = jnp.where(lane < N_GROUPS, logits, -jnp.inf)
    gmax = jnp.max(gl, axis=1, keepdims=True)
    g_idx = first_lane(gl == gmax).astype(I32)
    p_group = 1.0 / jnp.sum(jnp.exp(gl - gmax), axis=1, keepdims=True)
    e_lo = ROUTE_LANE0 + g_idx * EXPERTS_PER_GROUP
    el = jnp.where((lane >= e_lo) & (lane < e_lo + EXPERTS_PER_GROUP), logits, -jnp.inf)
    v1 = jnp.max(el, axis=1, keepdims=True)
    i1 = first_lane(el == v1)
    el2 = jnp.where(lane_f == i1, -jnp.inf, el)
    v2 = jnp.max(el2, axis=1, keepdims=True)
    i2 = first_lane(el2 == v2)
    t = jnp.exp(v2 - v1)
    w0 = p_group / (1.0 + t)
    w1 = p_group * t / (1.0 + t)

    oh0 = lane_f == i1
    oh1 = lane_f == i2
    oh = (oh0 | oh1).astype(BF16)
    r_i = lax.broadcasted_iota(I32, (tm, tm), 0)
    c_i = lax.broadcasted_iota(I32, (tm, tm), 1)
    lower = (c_i < r_i).astype(BF16)
    before = jnp.dot(lower, oh, preferred_element_type=F32) + run_ref[...]
    rank0 = jnp.sum(jnp.where(oh0, before, 0.0), axis=1, keepdims=True)
    rank1 = jnp.sum(jnp.where(oh1, before, 0.0), axis=1, keepdims=True)
    run_ref[...] = run_ref[...] + jnp.sum(oh.astype(F32), axis=0, keepdims=True)
    cnt_ref[...] = jnp.broadcast_to(run_ref[...], cnt_ref.shape).astype(I32)

    meta = jnp.where(lane == 0, i1 - ROUTE_LANE0, 0.0)
    meta = jnp.where(lane == 1, i2 - ROUTE_LANE0, meta)
    meta = jnp.where(lane == 2, rank0, meta)
    meta = jnp.where(lane == 3, rank1, meta)
    meta_ref[...] = meta.T[:META_ROWS, :]
    wts_ref[...] = jnp.where(lane == 0, w0, jnp.where(lane == 1, w1, 0.0))


def _post_attn(x2d, attn, ga, mixb, wba, wout, g_x, wq, mem2d, g_mem, wkv, wo, g_moe, wr, br, seq):
    n = x2d.shape[0]
    tps = seq // TM
    xdim = XATTN_HEADS * XATTN_HEAD_DIM
    row_spec = lambda c: pl.BlockSpec((TM, c), lambda i: (i, 0))
    consts = lambda *arrs: [_const_spec(a.shape) for a in arrs]
    return pl.pallas_call(
        functools.partial(_post_attn_kernel, tiles_per_seq=tps),
        grid=(n // TM,),
        in_specs=([row_spec(D_MODEL), row_spec(MLA_HEADS * V_HEAD_DIM), row_spec(D_MODEL),
                   row_spec(D_MODEL)] + consts(wba, wout, g_x, wq)
                  + [pl.BlockSpec((MEM_LEN, D_MODEL), lambda i: (i // tps, 0))]
                  + consts(g_mem, wkv, wo, g_moe, wr, br)),
        out_specs=[row_spec(D_MODEL),
                   pl.BlockSpec((TM // ROW_GROUP, ROW_GROUP, D_MODEL), lambda i: (i, 0, 0)),
                   pl.BlockSpec((META_ROWS, TM), lambda i: (0, i)), row_spec(LANES),
                   _const_spec((8, LANES))],
        out_shape=[jax.ShapeDtypeStruct((n, D_MODEL), F32),
                   jax.ShapeDtypeStruct((n // ROW_GROUP, ROW_GROUP, D_MODEL), F32),
                   jax.ShapeDtypeStruct((META_ROWS, n), F32),
                   jax.ShapeDtypeStruct((n, LANES), F32),
                   jax.ShapeDtypeStruct((8, LANES), I32)],
        scratch_shapes=[pltpu.VMEM((1, LANES), F32), pltpu.VMEM((MEM_LEN, xdim), BF16),
                        pltpu.VMEM((MEM_LEN, xdim), BF16)],
        compiler_params=_params(("arbitrary",)),
        name="post_attn",
    )(x2d, attn, ga, mixb, wba, wout, g_x, wq, mem2d, g_mem, wkv, wo, g_moe, wr, br)


def _tile_row(ref, group, k):
    return ref.at[group, pl.ds(k, 1), :]


def _hbm_row(ref, row):
    return ref.at[pl.ds(row, 1), :]


def _dispatch_kernel(fill_ref, dest0_ref, dest1_ref, zeros_ref, h3_ref, xd_ref, row_sem, fill_sem):
    i = pl.program_id(0)

    def fill_copy(b):
        return pltpu.make_async_copy(
            zeros_ref, xd_ref.at[pl.ds(pl.multiple_of(b * SLOT_BLOCK, SLOT_BLOCK), SLOT_BLOCK)],
            fill_sem)

    @pl.when(i == 0)
    def _():
        def start(b, c):
            @pl.when(fill_ref[b] > 0)
            def _():
                fill_copy(b).start()
            return c

        def wait(b, c):
            @pl.when(fill_ref[b] > 0)
            def _():
                fill_copy(b).wait()
            return c

        lax.fori_loop(0, fill_ref.shape[0], start, 0)
        lax.fori_loop(0, fill_ref.shape[0], wait, 0)

    def copies(g, k):
        r = g * ROW_GROUP + k
        src = _tile_row(h3_ref, g, k)
        return (pltpu.make_async_copy(src, _hbm_row(xd_ref, dest0_ref[0, 0, r]), row_sem),
                pltpu.make_async_copy(src, _hbm_row(xd_ref, dest1_ref[0, 0, r]), row_sem))

    def issue(g, c):
        for k in range(ROW_GROUP):
            for cp in copies(g, k):
                cp.start()
        return c

    def drain(g, c):
        for k in range(ROW_GROUP):
            for cp in copies(g, k):
                cp.wait()
        return c

    lax.fori_loop(0, TROW // ROW_GROUP, issue, 0)
    lax.fori_loop(0, TROW // ROW_GROUP, drain, 0)


def _dispatch(fill_blk, dest0, dest1, h3, n_slots):
    groups, _, d = h3.shape
    n = groups * ROW_GROUP
    grid_spec = pltpu.PrefetchScalarGridSpec(
        num_scalar_prefetch=1,
        grid=(n // TROW,),
        in_specs=[pl.BlockSpec((1, 1, TROW), lambda i, fb: (i, 0, 0), memory_space=pltpu.SMEM),
                  pl.BlockSpec((1, 1, TROW), lambda i, fb: (i, 0, 0), memory_space=pltpu.SMEM),
                  pl.BlockSpec((SLOT_BLOCK, d), lambda i, fb: (0, 0)),
                  pl.BlockSpec((TROW // ROW_GROUP, ROW_GROUP, d), lambda i, fb: (i, 0, 0))],
        out_specs=pl.BlockSpec(memory_space=pl.ANY),
        scratch_shapes=[pltpu.SemaphoreType.DMA, pltpu.SemaphoreType.DMA],
    )
    return pl.pallas_call(
        _dispatch_kernel,
        grid_spec=grid_spec,
        out_shape=jax.ShapeDtypeStruct((n_slots, d), F32),
        compiler_params=_params(("arbitrary",)),
        name="dispatch",
    )(fill_blk, dest0, dest1, jnp.zeros((SLOT_BLOCK, d), F32), h3)


def _experts_kernel(be_ref, nb_ref, nxt_ref, xd_ref, wgu_hbm, wdn_hbm, yd_ref,
                    wgu_f32, wdn_f32, wgu_bf, wdn_bf, slot_ref, sem):
    i = pl.program_id(0)

    def weight_copies(e, slot):
        return (pltpu.make_async_copy(wgu_hbm.at[e], wgu_f32.at[slot], sem.at[slot]),
                pltpu.make_async_copy(wdn_hbm.at[e], wdn_f32.at[slot], sem.at[slot]))

    @pl.when(i == 0)
    def _():
        slot_ref[0] = 0
        for cp in weight_copies(be_ref[0], 0):
            cp.start()

    @pl.when(i < nb_ref[0])
    def _():
        @pl.when((i == 0) | (be_ref[i] != be_ref[jnp.maximum(i - 1, 0)]))
        def _():
            slot = slot_ref[0]
            for cp in weight_copies(be_ref[i], slot):
                cp.wait()
            wgu_bf[...] = wgu_f32[slot].astype(BF16)
            wdn_bf[...] = wdn_f32[slot].astype(BF16)

            @pl.when(nxt_ref[i] >= 0)
            def _():
                for cp in weight_copies(nxt_ref[i], 1 - slot):
                    cp.start()

            slot_ref[0] = 1 - slot

        gu = jnp.dot(xd_ref[...].astype(BF16), wgu_bf[...], preferred_element_type=F32)
        gate = gu[:, :EXPERT_FF]
        up = gu[:, EXPERT_FF:]
        act = (gate * jax.nn.sigmoid(gate) * up).astype(BF16)
        yd_ref[...] = jnp.dot(act, wdn_bf[...], preferred_element_type=F32)


def _experts(block_expert, n_blocks_used, next_expert, xd, w_gate_up, w_down):
    n_slots, d = xd.shape
    slot_map = lambda i, be, nb, nx: (jnp.minimum(i, nb[0] - 1), 0)
    grid_spec = pltpu.PrefetchScalarGridSpec(
        num_scalar_prefetch=3,
        grid=(n_slots // SLOT_BLOCK,),
        in_specs=[pl.BlockSpec((SLOT_BLOCK, d), slot_map),
                  pl.BlockSpec(memory_space=pl.ANY),
                  pl.BlockSpec(memory_space=pl.ANY)],
        out_specs=pl.BlockSpec((SLOT_BLOCK, d), slot_map),
        scratch_shapes=[pltpu.VMEM((2, D_MODEL, 2 * EXPERT_FF), F32),
                        pltpu.VMEM((2, EXPERT_FF, D_MODEL), F32),
                        pltpu.VMEM((D_MODEL, 2 * EXPERT_FF), BF16),
                        pltpu.VMEM((EXPERT_FF, D_MODEL), BF16),
                        pltpu.SMEM((1,), I32),
                        pltpu.SemaphoreType.DMA((2,))],
    )
    return pl.pallas_call(
        _experts_kernel,
        grid_spec=grid_spec,
        out_shape=jax.ShapeDtypeStruct(xd.shape, F32),
        input_output_aliases={3: 0},
        compiler_params=_params(("arbitrary",)),
        name="experts",
    )(block_expert, n_blocks_used, next_expert, xd, w_gate_up, w_down)


def _combine_kernel(dest0_ref, dest1_ref, next0_ref, next1_ref, x2_ref, wts_ref, g_ref, yd_ref, o_ref,
                    buf, sem):
    i = pl.program_id(0)
    last = pl.num_programs(0) - 1
    rows = x2_ref.shape[0]
    slot = i % 2

    def copies(d_refs, slt, g, k):
        r = g * ROW_GROUP + k
        return tuple(pltpu.make_async_copy(_hbm_row(yd_ref, d_ref[0, 0, r]),
                                           _tile_row(buf.at[slt, j], g, k), sem.at[slt])
                     for j, d_ref in enumerate(d_refs))

    def issue(d_ref, slt):
        def body(g, c):
            for k in range(ROW_GROUP):
                for cp in copies(d_ref, slt, g, k):
                    cp.start()
            return c
        lax.fori_loop(0, rows // ROW_GROUP, body, 0)

    @pl.when(i == 0)
    def _():
        issue((dest0_ref, dest1_ref), 0)

    @pl.when(i < last)
    def _():
        issue((next0_ref, next1_ref), 1 - slot)

    def drain(g, c):
        for k in range(ROW_GROUP):
            for cp in copies((dest0_ref, dest1_ref), slot, g, k):
                cp.wait()
        return c

    lax.fori_loop(0, rows // ROW_GROUP, drain, 0)

    lane = lax.broadcasted_iota(I32, (rows, LANES), 1)
    wts = wts_ref[...]
    w0 = jnp.sum(jnp.where(lane == 0, wts, 0.0), axis=1, keepdims=True)
    w1 = jnp.sum(jnp.where(lane == 1, wts, 0.0), axis=1, keepdims=True)
    y0 = buf[slot, 0].reshape(rows, D_MODEL)
    y1 = buf[slot, 1].reshape(rows, D_MODEL)
    o_ref[...] = _rms(x2_ref[...] + y0 * w0 + y1 * w1, g_ref[...])


def _combine(dest0, dest1, x2, wts, g_final, yd):
    n = x2.shape[0]
    steps = n // TROW
    dest_spec = lambda f: pl.BlockSpec((1, 1, TROW), f, memory_space=pltpu.SMEM)
    cur = lambda i: (i, 0, 0)
    nxt = lambda i: (jnp.minimum(i + 1, steps - 1), 0, 0)
    return pl.pallas_call(
        _combine_kernel,
        grid=(steps,),
        in_specs=[dest_spec(cur), dest_spec(cur), dest_spec(nxt), dest_spec(nxt),
                  pl.BlockSpec((TROW, D_MODEL), lambda i: (i, 0)),
                  pl.BlockSpec((TROW, LANES), lambda i: (i, 0)),
                  _const_spec((1, D_MODEL)),
                  pl.BlockSpec(memory_space=pl.ANY)],
        out_specs=pl.BlockSpec((TROW, D_MODEL), lambda i: (i, 0)),
        out_shape=jax.ShapeDtypeStruct((n, D_MODEL), F32),
        scratch_shapes=[pltpu.VMEM((2, 2, TROW // ROW_GROUP, ROW_GROUP, D_MODEL), F32),
                        pltpu.SemaphoreType.DMA((2,))],
        compiler_params=_params(("arbitrary",)),
        name="combine",
    )(dest0, dest1, dest0, dest1, x2, wts, g_final, yd)


def _lanes(parts, width):
    ref = next(p for p in parts if not isinstance(p, int))
    cols = [jnp.zeros(ref.shape[:-1] + (p,), ref.dtype) if isinstance(p, int) else p for p in parts]
    used = sum(c.shape[-1] for c in cols)
    assert used <= width
    if used < width:
        cols.append(jnp.zeros(ref.shape[:-1] + (width - used,), ref.dtype))
    return jnp.concatenate(cols, axis=-1)


def _head_slots(w, per_head, pieces):
    rows = w.shape[0]
    wh = w.reshape(rows, MLA_HEADS, per_head)
    return _lanes(pieces(wh), HEAD_SLOT).reshape(rows, MLA_HEADS * HEAD_SLOT)


def _layer(x2d, pos3d, invf, mem2d, g_mem, w_kv_mem, batch, seq, g_mix, w_in, b_gate, g_cq, w_uq, g_ckv, w_ukv,
           conv_w, w_branch_a, w_branch_b, w_out, g_xattn, w_q_mem, w_o_mem, g_moe,
           w_router_group, b_router_group, w_router_expert, b_router_expert, w_gate_up, w_down,
           g_final):
    n = x2d.shape[0]
    row = lambda v: v.reshape(1, -1)
    lo, mid, hi = ROPE_LO, ROPE_LO + ROPE_HALF, ROPE_LO + QK_ROPE_DIM

    w_kr = w_in[:, COL_CKV:COL_KROPE]
    w1 = _lanes([w_in[:, :COL_CKV], ROPE_LO, w_kr], COL_CKV + HEAD_SLOT).astype(BF16)
    w2 = w_in[:, COL_KROPE:COL_CONV].astype(BF16)
    w3 = w_in[:, COL_CONV:].astype(BF16)
    qd = QK_NOPE_DIM + QK_ROPE_DIM
    wq = _head_slots(w_uq, qd, lambda wh: [wh]).astype(BF16)
    kvd = QK_NOPE_DIM + V_HEAD_DIM
    wkk = _head_slots(w_ukv, kvd, lambda wh: [wh[..., :QK_NOPE_DIM]]).astype(BF16)
    wkv = _head_slots(w_ukv, kvd, lambda wh: [wh[..., QK_NOPE_DIM:]]).astype(BF16)
    conv_w8 = jnp.concatenate([conv_w, jnp.zeros((8 - conv_w.shape[0], CONV_WIDTH), F32)], axis=0)

    q, k, v, ga, mixb = _in_proj(x2d, pos3d, invf, row(g_mix), w1, w2, w3, row(b_gate), row(g_cq),
                                 wq, row(g_ckv), wkk, wkv, conv_w8,
                                 w_branch_b.astype(BF16), seq)
    attn = _mla_attn(q, k, v, batch, seq)

    wr = _lanes([w_router_group, w_router_expert], LANES).astype(BF16)
    br = _lanes([row(b_router_group), row(b_router_expert)], LANES)
    x2, h3, meta, wts, cnt = _post_attn(x2d, attn, ga, mixb, w_branch_a.astype(BF16),
                                   w_out.astype(BF16), row(g_xattn), w_q_mem.astype(BF16), mem2d, row(g_mem),
                                   w_kv_mem.astype(BF16),
                                   w_o_mem.astype(BF16), row(g_moe), wr, br, seq)

    counts = cnt[0, ROUTE_LANE0:ROUTE_LANE0 + N_EXPERTS]
    padded = (counts + SLOT_BLOCK - 1) // SLOT_BLOCK * SLOT_BLOCK
    pad_end = jnp.cumsum(padded)
    pad_start = pad_end - padded
    n_slots = (2 * n + SLOT_BLOCK - 1) // SLOT_BLOCK * SLOT_BLOCK + N_EXPERTS * SLOT_BLOCK
    routed = meta[0:4].astype(I32)
    expert_col = jnp.arange(N_EXPERTS, dtype=I32)[:, None, None]
    dest = jnp.sum(jnp.where(routed[None, 0:2] == expert_col, pad_start[:, None, None], 0),
                   axis=0) + routed[2:4]
    dest0 = dest[0].reshape(n // TROW, 1, TROW)
    dest1 = dest[1].reshape(n // TROW, 1, TROW)
    block_lo = jnp.arange(n_slots // SLOT_BLOCK, dtype=I32) * SLOT_BLOCK
    block_expert = jnp.minimum(jnp.sum(pad_end[None, :] <= block_lo[:, None], axis=1),
                               N_EXPERTS - 1).astype(I32)
    n_blocks_used = (pad_end[-1:] // SLOT_BLOCK).astype(I32)
    fill_blk = ((block_lo + SLOT_BLOCK == pad_end[block_expert])
                | (block_lo >= pad_end[-1])).astype(I32)

    cand = jnp.where(padded > 0, jnp.arange(N_EXPERTS, dtype=I32), N_EXPERTS)
    later = jnp.concatenate([lax.cummin(cand, reverse=True)[1:],
                             jnp.full((1,), N_EXPERTS, I32)])
    next_expert = jnp.where(later < N_EXPERTS, later, -1)[block_expert].astype(I32)

    xd = _dispatch(fill_blk, dest0, dest1, h3, n_slots)
    yd = _experts(block_expert, n_blocks_used, next_expert, xd, w_gate_up, w_down)
    return _combine(dest0, dest1, x2, wts, row(g_final), yd)


def kernel(x, mem, positions, g_mix, w_in, b_gate, g_cq, w_uq, g_ckv, w_ukv, conv_w, w_branch_a,
           w_branch_b, w_out, g_xattn, g_mem, w_q_mem, w_kv_mem, w_o_mem, g_moe, w_router_group,
           b_router_group, w_router_expert, b_router_expert, w_gate_up, w_down, g_final):
    batch, seq, d = x.shape
    depth = g_mix.shape[0]
    assert d == D_MODEL and depth == 1 and seq % TM == 0 and seq % TQ == 0 and TQ == TK
    n = batch * seq
    x2d = x.reshape(n, d)
    pos3d = positions.astype(I32).reshape(n // TM, TM // LANES, LANES)
    inv_freq = ROPE_THETA ** (-jnp.arange(0, QK_ROPE_DIM, 2, dtype=F32) / QK_ROPE_DIM)
    invf = _lanes([ROPE_LO, inv_freq.reshape(1, -1), inv_freq.reshape(1, -1)], HEAD_SLOT)
    l = 0
    out = _layer(x2d, pos3d, invf, mem.reshape(batch * MEM_LEN, d), g_mem[l], w_kv_mem[l], batch,
                 seq, g_mix[l], w_in[l], b_gate[l], g_cq[l],
                 w_uq[l], g_ckv[l], w_ukv[l], conv_w[l], w_branch_a[l], w_branch_b[l], w_out[l],
                 g_xattn[l], w_q_mem[l], w_o_mem[l], g_moe[l], w_router_group[l],
                 b_router_group[l], w_router_expert[l], b_router_expert[l], w_gate_up[l],
                 w_down[l], g_final)
    return out.reshape(batch, seq, d)
```

```python
import functools
import math

import jax
import jax.numpy as jnp
from jax import lax
from jax.experimental import pallas as pl
from jax.experimental.pallas import tpu as pltpu

F32 = jnp.float32
BF16 = jnp.bfloat16
I32 = jnp.int32

D_MODEL = 1024
MEM_LEN = 256
MLA_HEADS = 8
QK_NOPE_DIM = 64
QK_ROPE_DIM = 32
V_HEAD_DIM = 64
Q_LORA_RANK = 256
KV_LORA_RANK = 128
ROPE_THETA = 10000.0
CONV_WIDTH = 512
XATTN_HEADS = 4
XATTN_HEAD_DIM = 128
N_GROUPS = 8
EXPERTS_PER_GROUP = 8
N_EXPERTS = N_GROUPS * EXPERTS_PER_GROUP
EXPERT_FF = 256
EPS = 1e-6

COL_CQ = Q_LORA_RANK
COL_CKV = COL_CQ + KV_LORA_RANK
COL_KROPE = COL_CKV + QK_ROPE_DIM
COL_CONV = COL_KROPE + 3 * CONV_WIDTH

LANES = 128
HEAD_SLOT = LANES
ROPE_LO = QK_NOPE_DIM
ROPE_HALF = QK_ROPE_DIM // 2
L_LANE = V_HEAD_DIM
SLOT_BLOCK = 256
ROUTE_LANE0 = N_GROUPS
NEG_BIG = -1e30

TM = 512
TQ = 512
TK = 512
META_ROWS = 8
POST_PARTS = 1
DIAG_STRIPS = 2
ATTN_CHAINS = 8
TROW = 256
DMA_THREADS = 2
ROW_GROUP = 8
VMEM_LIMIT = 56 * 1024 * 1024


def _rms(x, g):
    return x * lax.rsqrt(jnp.mean(x * x, axis=-1, keepdims=True) + EPS) * g


def _const_spec(shape):
    nd = len(shape)
    return pl.BlockSpec(shape, lambda *_: (0,) * nd)


def _params(sem):
    return pltpu.CompilerParams(dimension_semantics=sem, vmem_limit_bytes=VMEM_LIMIT)


def _in_proj_kernel(x_ref, pos_ref, invf_ref, g_mix_ref, w1_ref, w2_ref, w3_ref, b_gate_ref,
                    g_cq_ref, wq_ref, g_ckv_ref, wkk_ref, wkv_ref, conv_w_ref, wbb_ref,
                    q_ref, k_ref, v_ref, ga_ref, mixb_ref, carry_ref, *, tiles_per_seq):
    step = pl.program_id(0)
    tm = x_ref.shape[0]
    h = _rms(x_ref[...], g_mix_ref[...]).astype(BF16)

    lane = lax.broadcasted_iota(I32, (tm, HEAD_SLOT), 1)
    rope_lane = (lane >= ROPE_LO) & (lane < ROPE_LO + QK_ROPE_DIM)
    pos_rows = pos_ref[0].astype(F32)
    pos_tile = jnp.concatenate([jnp.broadcast_to(pos_rows[g:g + 1, :], (LANES, LANES))
                                for g in range(tm // LANES)], axis=0)
    row_lane = lax.broadcasted_iota(I32, (tm, LANES), 0) & (LANES - 1)
    pos_col = jnp.sum(jnp.where(lane == row_lane, pos_tile, 0.0), axis=1, keepdims=True)
    ang = pos_col * invf_ref[...]
    first_half = lane < ROPE_LO + ROPE_HALF
    cos_t = jnp.where(rope_lane, jnp.cos(ang), 0.0)
    sin_t = jnp.where(rope_lane, jnp.sin(ang), 0.0)
    sin_t = jnp.where(first_half, -sin_t, sin_t)

    def swap_halves(t):
        return jnp.where(first_half, pltpu.roll(t, HEAD_SLOT - ROPE_HALF, 1),
                         pltpu.roll(t, ROPE_HALF, 1))

    p1 = jnp.dot(h, w1_ref[...], preferred_element_type=F32)
    c_q = p1[:, :COL_CQ]
    c_kv = p1[:, COL_CQ:COL_CKV]
    kr = p1[:, COL_CKV:COL_CKV + HEAD_SLOT]
    k_rot = kr * cos_t + swap_halves(kr) * sin_t

    scale = math.log2(math.e) / math.sqrt(QK_NOPE_DIM + QK_ROPE_DIM)
    q_cos = jnp.where(lane < ROPE_LO, scale, cos_t * scale)
    q_sin = sin_t * scale
    cqn = _rms(c_q, g_cq_ref[...]).astype(BF16)
    qq = jnp.dot(cqn, wq_ref[...], preferred_element_type=F32)
    ckvn = _rms(c_kv, g_ckv_ref[...]).astype(BF16)
    kk = jnp.dot(ckvn, wkk_ref[...], preferred_element_type=F32)
    vv = jnp.dot(ckvn, wkv_ref[...], preferred_element_type=F32)
    ones_lane = jnp.where(lane == L_LANE, 1.0, 0.0)
    for hd in range(MLA_HEADS):
        sl = slice(hd * HEAD_SLOT, (hd + 1) * HEAD_SLOT)
        q_ref[:, sl] = (qq[:, sl] * q_cos + swap_halves(qq[:, sl]) * q_sin).astype(BF16)
        k_ref[:, sl] = (kk[:, sl] + k_rot).astype(BF16)
        v_ref[:, sl] = (vv[:, sl] + ones_lane).astype(BF16)

    p2 = jnp.dot(h, w2_ref[...], preferred_element_type=F32)
    b_gate = p2[:, :CONV_WIDTH]
    u = p2[:, CONV_WIDTH:2 * CONV_WIDTH] * p2[:, 2 * CONV_WIDTH:]

    @pl.when(step % tiles_per_seq == 0)
    def _():
        carry_ref[...] = jnp.zeros_like(carry_ref)

    prev1 = carry_ref[7:8, :]
    prev2 = carry_ref[6:7, :]
    row = lax.broadcasted_iota(I32, (tm, CONV_WIDTH), 0)
    u1 = jnp.where(row == 0, prev1, pltpu.roll(u, 1, 0))
    u2 = jnp.where(row == 0, prev2, jnp.where(row == 1, prev1, pltpu.roll(u, 2, 0)))
    carry_ref[...] = u[tm - 8:, :]
    cw = conv_w_ref[...]
    conv = b_gate * (cw[0:1, :] * u2 + cw[1:2, :] * u1 + cw[2:3, :] * u)
    branch_b = jnp.dot(conv.astype(BF16), wbb_ref[...], preferred_element_type=F32)

    gates = jax.nn.sigmoid(jnp.dot(h, w3_ref[...], preferred_element_type=F32) + b_gate_ref[...])
    ga_ref[...] = gates[:, :D_MODEL].astype(BF16)
    mixb_ref[...] = (gates[:, D_MODEL:] * branch_b).astype(BF16)


def _in_proj(x2d, pos3d, invf, g_mix, w1, w2, w3, b_gate, g_cq, wq, g_ckv, wkk, wkv,
             conv_w, wbb, seq):
    n = x2d.shape[0]
    row_spec = lambda c: pl.BlockSpec((TM, c), lambda i: (i, 0))
    consts = [invf, g_mix, w1, w2, w3, b_gate, g_cq, wq, g_ckv, wkk, wkv, conv_w, wbb]
    kv_cols = MLA_HEADS * HEAD_SLOT
    return pl.pallas_call(
        functools.partial(_in_proj_kernel, tiles_per_seq=seq // TM),
        grid=(n // TM,),
        in_specs=[row_spec(D_MODEL), pl.BlockSpec((1, TM // LANES, LANES), lambda i: (i, 0, 0))]
        + [_const_spec(c.shape) for c in consts],
        out_specs=[row_spec(kv_cols), row_spec(kv_cols), row_spec(kv_cols),
                   row_spec(D_MODEL), row_spec(D_MODEL)],
        out_shape=[jax.ShapeDtypeStruct((n, kv_cols), BF16),
                   jax.ShapeDtypeStruct((n, kv_cols), BF16),
                   jax.ShapeDtypeStruct((n, kv_cols), BF16),
                   jax.ShapeDtypeStruct((n, D_MODEL), BF16),
                   jax.ShapeDtypeStruct((n, D_MODEL), BF16)],
        scratch_shapes=[pltpu.VMEM((8, CONV_WIDTH), F32)],
        compiler_params=_params(("arbitrary",)),
        name="in_proj",
    )(x2d, pos3d, *consts)


def _attn_kernel(q_ref, k_ref, v_ref, o_ref):
    qi = pl.program_id(1)
    tq = q_ref.shape[0]

    def block(carry, heads, col_start, ncols, row_lo, masked):
        out = []
        for c, hd in enumerate(heads):
            m, acc = carry[c]
            hsl = slice(hd * HEAD_SLOT, (hd + 1) * HEAD_SLOT)
            kb = k_ref[pl.ds(col_start, ncols), hsl]
            vb = v_ref[pl.ds(col_start, ncols), hsl]
            s = lax.dot_general(q_ref[row_lo:, hsl], kb, (((1,), (1,)), ((), ())),
                                preferred_element_type=F32)
            if masked:
                r = qi * tq + row_lo + lax.broadcasted_iota(I32, s.shape, 0)
                s = jnp.where(col_start + lax.broadcasted_iota(I32, s.shape, 1) <= r, s, NEG_BIG)
            m_hi, acc_hi = m[row_lo:], acc[row_lo:]
            m_new = jnp.maximum(m_hi, jnp.max(s, axis=1, keepdims=True))
            p = jnp.exp2(s - m_new).astype(BF16)
            acc_new = jnp.exp2(m_hi - m_new) * acc_hi + jnp.dot(p, vb, preferred_element_type=F32)
            if row_lo:
                m_new = jnp.concatenate([m[:row_lo], m_new], axis=0)
                acc_new = jnp.concatenate([acc[:row_lo], acc_new], axis=0)
            out.append((m_new, acc_new))
        return tuple(out)

    def full_step(j, carry, heads):
        return block(carry, heads, pl.multiple_of(j * TK, TK), TK, 0, False)

    lane = lax.broadcasted_iota(I32, (tq, HEAD_SLOT), 1)
    for g in range(MLA_HEADS // ATTN_CHAINS):
        heads = tuple(range(g * ATTN_CHAINS, (g + 1) * ATTN_CHAINS))
        carry = tuple((jnp.full((tq, 1), NEG_BIG, F32), jnp.zeros((tq, HEAD_SLOT), F32))
                      for _ in heads)
        carry = lax.fori_loop(0, qi, functools.partial(full_step, heads=heads), carry)
        for d in range(DIAG_STRIPS):
            col = pl.multiple_of(qi * TK + d * (TK // DIAG_STRIPS), TK // DIAG_STRIPS)
            carry = block(carry, heads, col, TK // DIAG_STRIPS, d * (tq // DIAG_STRIPS), True)
        outs = []
        for _, acc in carry:
            denom = jnp.sum(jnp.where(lane == L_LANE, acc, 0.0), axis=1, keepdims=True)
            outs.append(acc / denom)
        for c in range(0, ATTN_CHAINS, 2):
            hp = heads[c] // 2
            pair = jnp.where(lane < V_HEAD_DIM, outs[c], pltpu.roll(outs[c + 1], V_HEAD_DIM, 1))
            o_ref[:, hp * HEAD_SLOT:(hp + 1) * HEAD_SLOT] = pair.astype(BF16)


def _mla_attn(q, k, v, batch, seq):
    n = q.shape[0]
    nq = seq // TQ
    out_cols = MLA_HEADS * V_HEAD_DIM
    return pl.pallas_call(
        _attn_kernel,
        grid=(batch, nq),
        in_specs=[pl.BlockSpec((TQ, q.shape[1]), lambda b, i: (b * nq + i, 0)),
                  pl.BlockSpec((seq, k.shape[1]), lambda b, i: (b, 0)),
                  pl.BlockSpec((seq, v.shape[1]), lambda b, i: (b, 0))],
        out_specs=pl.BlockSpec((TQ, out_cols), lambda b, i: (b * nq + i, 0)),
        out_shape=jax.ShapeDtypeStruct((n, out_cols), BF16),
        compiler_params=_params(("arbitrary", "arbitrary")),
        name="mla_attn",
    )(q, k, v)


def _post_attn_kernel(x_ref, attn_ref, ga_ref, mixb_ref, wba_ref, wout_ref, g_x_ref, wq_ref,
                      mem_ref, g_mem_ref, wkv_ref, wo_ref, g_moe_ref, wr_ref, br_ref,
                      x2_ref, h3_ref, meta_ref, wts_ref, cnt_ref, run_ref, km_ref, vm_ref,
                      *, tiles_per_seq):
    step = pl.program_id(0)
    tm = x_ref.shape[0]
    xdim = XATTN_HEADS * XATTN_HEAD_DIM

    @pl.when(step == 0)
    def _():
        run_ref[...] = jnp.zeros_like(run_ref)

    @pl.when(step % tiles_per_seq == 0)
    def _():
        mem_n = _rms(mem_ref[...], g_mem_ref[...]).astype(BF16)
        kv = jnp.dot(mem_n, wkv_ref[...], preferred_element_type=F32)
        km_ref[...] = kv[:, :xdim].astype(BF16)
        vm_ref[...] = kv[:, xdim:].astype(BF16)

    def mixer_and_xattn(rows):
        a = jnp.dot(attn_ref[rows, :], wba_ref[...], preferred_element_type=F32)
        mixed = ga_ref[rows, :].astype(F32) * a + mixb_ref[rows, :].astype(F32)
        x1 = x_ref[rows, :] + jnp.dot(mixed.astype(BF16), wout_ref[...], preferred_element_type=F32)

        h2 = _rms(x1, g_x_ref[...]).astype(BF16)
        qm = jnp.dot(h2, wq_ref[...], preferred_element_type=F32) * (1.0 / math.sqrt(XATTN_HEAD_DIM))
        qm = qm.astype(BF16)
        heads = []
        for hd in range(XATTN_HEADS):
            sl = slice(hd * XATTN_HEAD_DIM, (hd + 1) * XATTN_HEAD_DIM)
            s = lax.dot_general(qm[:, sl], km_ref[:, sl], (((1,), (1,)), ((), ())),
                                preferred_element_type=F32)
            p = jnp.exp(s - jnp.max(s, axis=1, keepdims=True))
            o = jnp.dot(p.astype(BF16), vm_ref[:, sl], preferred_element_type=F32)
            heads.append((o / jnp.sum(p, axis=1, keepdims=True)).astype(BF16))
        o_all = jnp.concatenate(heads, axis=1)
        x2 = x1 + jnp.dot(o_all, wo_ref[...], preferred_element_type=F32)
        x2_ref[rows, :] = x2
        h3 = _rms(x2, g_moe_ref[...])
        g0, g1 = rows.start // ROW_GROUP, rows.stop // ROW_GROUP
        h3_ref[g0:g1] = h3.reshape(g1 - g0, ROW_GROUP, D_MODEL)
        return jnp.dot(h3.astype(BF16), wr_ref[...], preferred_element_type=F32) + br_ref[...]

    part = tm // POST_PARTS
    logits = jnp.concatenate([mixer_and_xattn(slice(c * part, (c + 1) * part))
                              for c in range(POST_PARTS)], axis=0)

    lane = lax.broadcasted_iota(I32, (tm, LANES), 1)
    lane_f = lane.astype(F32)

    def first_lane(hit):
        return jnp.min(jnp.where(hit, lane_f, float(LANES)), axis=1, keepdims=True)

    gl = jnp.where(lane < N_GROUPS, logits, -jnp.inf)
    gmax = jnp.max(gl, axis=1, keepdims=True)
    g_idx = first_lane(gl == gmax).astype(I32)
    p_group = 1.0 / jnp.sum(jnp.exp(gl - gmax), axis=1, keepdims=True)
    e_lo = ROUTE_LANE0 + g_idx * EXPERTS_PER_GROUP
    el = jnp.where((lane >= e_lo) & (lane < e_lo + EXPERTS_PER_GROUP), logits, -jnp.inf)
    v1 = jnp.max(el, axis=1, keepdims=True)
    i1 = first_lane(el == v1)
    el2 = jnp.where(lane_f == i1, -jnp.inf, el)
    v2 = jnp.max(el2, axis=1, keepdims=True)
    i2 = first_lane(el2 == v2)
    t = jnp.exp(v2 - v1)
    w0 = p_group / (1.0 + t)
    w1 = p_group * t / (1.0 + t)

    oh0 = lane_f == i1
    oh1 = lane_f == i2
    oh = (oh0 | oh1).astype(BF16)
    r_i = lax.broadcasted_iota(I32, (tm, tm), 0)
    c_i = lax.broadcasted_iota(I32, (tm, tm), 1)
    lower = (c_i < r_i).astype(BF16)
    before = jnp.dot(lower, oh, preferred_element_type=F32) + run_ref[...]
    rank0 = jnp.sum(jnp.where(oh0, before, 0.0), axis=1, keepdims=True)
    rank1 = jnp.sum(jnp.where(oh1, before, 0.0), axis=1, keepdims=True)
    run_ref[...] = run_ref[...] + jnp.sum(oh.astype(F32), axis=0, keepdims=True)
    cnt_ref[...] = jnp.broadcast_to(run_ref[...], cnt_ref.shape).astype(I32)

    meta = jnp.where(lane == 0, i1 - ROUTE_LANE0, 0.0)
    meta = jnp.where(lane == 1, i2 - ROUTE_LANE0, meta)
    meta = jnp.where(lane == 2, rank0, meta)
    meta = jnp.where(lane == 3, rank1, meta)
    meta_ref[...] = meta.T[:META_ROWS, :]
    wts_ref[...] = jnp.where(lane == 0, w0, jnp.where(lane == 1, w1, 0.0))


def _post_attn(x2d, attn, ga, mixb, wba, wout, g_x, wq, mem2d, g_mem, wkv, wo, g_moe, wr, br, seq):
    n = x2d.shape[0]
    tps = seq // TM
    xdim = XATTN_HEADS * XATTN_HEAD_DIM
    row_spec = lambda c: pl.BlockSpec((TM, c), lambda i: (i, 0))
    consts = lambda *arrs: [_const_spec(a.shape) for a in arrs]
    return pl.pallas_call(
        functools.partial(_post_attn_kernel, tiles_per_seq=tps),
        grid=(n // TM,),
        in_specs=([row_spec(D_MODEL), row_spec(MLA_HEADS * V_HEAD_DIM), row_spec(D_MODEL),
                   row_spec(D_MODEL)] + consts(wba, wout, g_x, wq)
                  + [pl.BlockSpec((MEM_LEN, D_MODEL), lambda i: (i // tps, 0))]
                  + consts(g_mem, wkv, wo, g_moe, wr, br)),
        out_specs=[row_spec(D_MODEL),
                   pl.BlockSpec((TM // ROW_GROUP, ROW_GROUP, D_MODEL), lambda i: (i, 0, 0)),
                   pl.BlockSpec((META_ROWS, TM), lambda i: (0, i)), row_spec(LANES),
                   _const_spec((8, LANES))],
        out_shape=[jax.ShapeDtypeStruct((n, D_MODEL), F32),
                   jax.ShapeDtypeStruct((n // ROW_GROUP, ROW_GROUP, D_MODEL), F32),
                   jax.ShapeDtypeStruct((META_ROWS, n), F32),
                   jax.ShapeDtypeStruct((n, LANES), F32),
                   jax.ShapeDtypeStruct((8, LANES), I32)],
        scratch_shapes=[pltpu.VMEM((1, LANES), F32), pltpu.VMEM((MEM_LEN, xdim), BF16),
                        pltpu.VMEM((MEM_LEN, xdim), BF16)],
        compiler_params=_params(("arbitrary",)),
        name="post_attn",
    )(x2d, attn, ga, mixb, wba, wout, g_x, wq, mem2d, g_mem, wkv, wo, g_moe, wr, br)


def _tile_row(ref, group, k):
    return ref.at[group, pl.ds(k, 1), :]


def _hbm_row(ref, row):
    return ref.at[pl.ds(row, 1), :]


def _dispatch_kernel(fill_ref, dest0_ref, dest1_ref, zeros_ref, h3_ref, xd_ref, row_sem, fill_sem):
    i = pl.program_id(0)

    def fill_copy(b):
        return pltpu.make_async_copy(
            zeros_ref, xd_ref.at[pl.ds(pl.multiple_of(b * SLOT_BLOCK, SLOT_BLOCK), SLOT_BLOCK)],
            fill_sem)

    @pl.when(i == 0)
    def _():
        def start(b, c):
            @pl.when(fill_ref[b] > 0)
            def _():
                fill_copy(b).start()
            return c

        def wait(b, c):
            @pl.when(fill_ref[b] > 0)
            def _():
                fill_copy(b).wait()
            return c

        lax.fori_loop(0, fill_ref.shape[0], start, 0)
        lax.fori_loop(0, fill_ref.shape[0], wait, 0)

    def copies(g, k):
        r = g * ROW_GROUP + k
        src = _tile_row(h3_ref, g, k)
        return (pltpu.make_async_copy(src, _hbm_row(xd_ref, dest0_ref[0, 0, r]), row_sem),
                pltpu.make_async_copy(src, _hbm_row(xd_ref, dest1_ref[0, 0, r]), row_sem))

    def issue(g, c):
        for k in range(ROW_GROUP):
            for j, cp in enumerate(copies(g, k)):
                cp.start(priority=j % DMA_THREADS)
        return c

    def drain(g, c):
        for k in range(ROW_GROUP):
            for cp in copies(g, k):
                cp.wait()
        return c

    lax.fori_loop(0, TROW // ROW_GROUP, issue, 0)
    lax.fori_loop(0, TROW // ROW_GROUP, drain, 0)


def _dispatch(fill_blk, dest0, dest1, h3, n_slots):
    groups, _, d = h3.shape
    n = groups * ROW_GROUP
    grid_spec = pltpu.PrefetchScalarGridSpec(
        num_scalar_prefetch=1,
        grid=(n // TROW,),
        in_specs=[pl.BlockSpec((1, 1, TROW), lambda i, fb: (i, 0, 0), memory_space=pltpu.SMEM),
                  pl.BlockSpec((1, 1, TROW), lambda i, fb: (i, 0, 0), memory_space=pltpu.SMEM),
                  pl.BlockSpec((SLOT_BLOCK, d), lambda i, fb: (0, 0)),
                  pl.BlockSpec((TROW // ROW_GROUP, ROW_GROUP, d), lambda i, fb: (i, 0, 0))],
        out_specs=pl.BlockSpec(memory_space=pl.ANY),
        scratch_shapes=[pltpu.SemaphoreType.DMA, pltpu.SemaphoreType.DMA],
    )
    return pl.pallas_call(
        _dispatch_kernel,
        grid_spec=grid_spec,
        out_shape=jax.ShapeDtypeStruct((n_slots, d), F32),
        compiler_params=_params(("arbitrary",)),
        name="dispatch",
    )(fill_blk, dest0, dest1, jnp.zeros((SLOT_BLOCK, d), F32), h3)


def _experts_kernel(be_ref, nb_ref, nxt_ref, xd_ref, wgu_hbm, wdn_hbm, yd_ref,
                    wgu_f32, wdn_f32, wgu_bf, wdn_bf, slot_ref, sem):
    i = pl.program_id(0)

    def weight_copies(e, slot):
        return (pltpu.make_async_copy(wgu_hbm.at[e], wgu_f32.at[slot], sem.at[slot]),
                pltpu.make_async_copy(wdn_hbm.at[e], wdn_f32.at[slot], sem.at[slot]))

    @pl.when(i == 0)
    def _():
        slot_ref[0] = 0
        for cp in weight_copies(be_ref[0], 0):
            cp.start()

    @pl.when(i < nb_ref[0])
    def _():
        @pl.when((i == 0) | (be_ref[i] != be_ref[jnp.maximum(i - 1, 0)]))
        def _():
            slot = slot_ref[0]
            for cp in weight_copies(be_ref[i], slot):
                cp.wait()
            wgu_bf[...] = wgu_f32[slot].astype(BF16)
            wdn_bf[...] = wdn_f32[slot].astype(BF16)

            @pl.when(nxt_ref[i] >= 0)
            def _():
                for cp in weight_copies(nxt_ref[i], 1 - slot):
                    cp.start()

            slot_ref[0] = 1 - slot

        gu = jnp.dot(xd_ref[...].astype(BF16), wgu_bf[...], preferred_element_type=F32)
        gate = gu[:, :EXPERT_FF]
        up = gu[:, EXPERT_FF:]
        act = (gate * jax.nn.sigmoid(gate) * up).astype(BF16)
        yd_ref[...] = jnp.dot(act, wdn_bf[...], preferred_element_type=F32)


def _experts(block_expert, n_blocks_used, next_expert, xd, w_gate_up, w_down):
    n_slots, d = xd.shape
    slot_map = lambda i, be, nb, nx: (jnp.minimum(i, nb[0] - 1), 0)
    grid_spec = pltpu.PrefetchScalarGridSpec(
        num_scalar_prefetch=3,
        grid=(n_slots // SLOT_BLOCK,),
        in_specs=[pl.BlockSpec((SLOT_BLOCK, d), slot_map),
                  pl.BlockSpec(memory_space=pl.ANY),
                  pl.BlockSpec(memory_space=pl.ANY)],
        out_specs=pl.BlockSpec((SLOT_BLOCK, d), slot_map),
        scratch_shapes=[pltpu.VMEM((2, D_MODEL, 2 * EXPERT_FF), F32),
                        pltpu.VMEM((2, EXPERT_FF, D_MODEL), F32),
                        pltpu.VMEM((D_MODEL, 2 * EXPERT_FF), BF16),
                        pltpu.VMEM((EXPERT_FF, D_MODEL), BF16),
                        pltpu.SMEM((1,), I32),
                        pltpu.SemaphoreType.DMA((2,))],
    )
    return pl.pallas_call(
        _experts_kernel,
        grid_spec=grid_spec,
        out_shape=jax.ShapeDtypeStruct(xd.shape, F32),
        input_output_aliases={3: 0},
        compiler_params=_params(("arbitrary",)),
        name="experts",
    )(block_expert, n_blocks_used, next_expert, xd, w_gate_up, w_down)


def _combine_kernel(dest0_ref, dest1_ref, next0_ref, next1_ref, x2_ref, wts_ref, g_ref, yd_ref, o_ref,
                    buf, sem):
    i = pl.program_id(0)
    last = pl.num_programs(0) - 1
    rows = x2_ref.shape[0]
    slot = i % 2

    def copies(d_refs, slt, g, k):
        r = g * ROW_GROUP + k
        return tuple(pltpu.make_async_copy(_hbm_row(yd_ref, d_ref[0, 0, r]),
                                           _tile_row(buf.at[slt, j], g, k), sem.at[slt])
                     for j, d_ref in enumerate(d_refs))

    def issue(d_ref, slt):
        def body(g, c):
            for k in range(ROW_GROUP):
                for j, cp in enumerate(copies(d_ref, slt, g, k)):
                    cp.start(priority=j % DMA_THREADS)
            return c
        lax.fori_loop(0, rows // ROW_GROUP, body, 0)

    @pl.when(i == 0)
    def _():
        issue((dest0_ref, dest1_ref), 0)

    @pl.when(i < last)
    def _():
        issue((next0_ref, next1_ref), 1 - slot)

    def drain(g, c):
        for k in range(ROW_GROUP):
            for cp in copies((dest0_ref, dest1_ref), slot, g, k):
                cp.wait()
        return c

    lax.fori_loop(0, rows // ROW_GROUP, drain, 0)

    lane = lax.broadcasted_iota(I32, (rows, LANES), 1)
    wts = wts_ref[...]
    w0 = jnp.sum(jnp.where(lane == 0, wts, 0.0), axis=1, keepdims=True)
    w1 = jnp.sum(jnp.where(lane == 1, wts, 0.0), axis=1, keepdims=True)
    y0 = buf[slot, 0].reshape(rows, D_MODEL)
    y1 = buf[slot, 1].reshape(rows, D_MODEL)
    o_ref[...] = _rms(x2_ref[...] + y0 * w0 + y1 * w1, g_ref[...])


def _combine(dest0, dest1, x2, wts, g_final, yd):
    n = x2.shape[0]
    steps = n // TROW
    dest_spec = lambda f: pl.BlockSpec((1, 1, TROW), f, memory_space=pltpu.SMEM)
    cur = lambda i: (i, 0, 0)
    nxt = lambda i: (jnp.minimum(i + 1, steps - 1), 0, 0)
    return pl.pallas_call(
        _combine_kernel,
        grid=(steps,),
        in_specs=[dest_spec(cur), dest_spec(cur), dest_spec(nxt), dest_spec(nxt),
                  pl.BlockSpec((TROW, D_MODEL), lambda i: (i, 0)),
                  pl.BlockSpec((TROW, LANES), lambda i: (i, 0)),
                  _const_spec((1, D_MODEL)),
                  pl.BlockSpec(memory_space=pl.ANY)],
        out_specs=pl.BlockSpec((TROW, D_MODEL), lambda i: (i, 0)),
        out_shape=jax.ShapeDtypeStruct((n, D_MODEL), F32),
        scratch_shapes=[pltpu.VMEM((2, 2, TROW // ROW_GROUP, ROW_GROUP, D_MODEL), F32),
                        pltpu.SemaphoreType.DMA((2,))],
        compiler_params=_params(("arbitrary",)),
        name="combine",
    )(dest0, dest1, dest0, dest1, x2, wts, g_final, yd)


def _lanes(parts, width):
    ref = next(p for p in parts if not isinstance(p, int))
    cols = [jnp.zeros(ref.shape[:-1] + (p,), ref.dtype) if isinstance(p, int) else p for p in parts]
    used = sum(c.shape[-1] for c in cols)
    assert used <= width
    if used < width:
        cols.append(jnp.zeros(ref.shape[:-1] + (width - used,), ref.dtype))
    return jnp.concatenate(cols, axis=-1)


def _head_slots(w, per_head, pieces):
    rows = w.shape[0]
    wh = w.reshape(rows, MLA_HEADS, per_head)
    return _lanes(pieces(wh), HEAD_SLOT).reshape(rows, MLA_HEADS * HEAD_SLOT)


def _layer(x2d, pos3d, invf, mem2d, g_mem, w_kv_mem, batch, seq, g_mix, w_in, b_gate, g_cq, w_uq, g_ckv, w_ukv,
           conv_w, w_branch_a, w_branch_b, w_out, g_xattn, w_q_mem, w_o_mem, g_moe,
           w_router_group, b_router_group, w_router_expert, b_router_expert, w_gate_up, w_down,
           g_final):
    n = x2d.shape[0]
    row = lambda v: v.reshape(1, -1)
    lo, mid, hi = ROPE_LO, ROPE_LO + ROPE_HALF, ROPE_LO + QK_ROPE_DIM

    w_kr = w_in[:, COL_CKV:COL_KROPE]
    w1 = _lanes([w_in[:, :COL_CKV], ROPE_LO, w_kr], COL_CKV + HEAD_SLOT).astype(BF16)
    w2 = w_in[:, COL_KROPE:COL_CONV].astype(BF16)
    w3 = w_in[:, COL_CONV:].astype(BF16)
    qd = QK_NOPE_DIM + QK_ROPE_DIM
    wq = _head_slots(w_uq, qd, lambda wh: [wh]).astype(BF16)
    kvd = QK_NOPE_DIM + V_HEAD_DIM
    wkk = _head_slots(w_ukv, kvd, lambda wh: [wh[..., :QK_NOPE_DIM]]).astype(BF16)
    wkv = _head_slots(w_ukv, kvd, lambda wh: [wh[..., QK_NOPE_DIM:]]).astype(BF16)
    conv_w8 = jnp.concatenate([conv_w, jnp.zeros((8 - conv_w.shape[0], CONV_WIDTH), F32)], axis=0)

    q, k, v, ga, mixb = _in_proj(x2d, pos3d, invf, row(g_mix), w1, w2, w3, row(b_gate), row(g_cq),
                                 wq, row(g_ckv), wkk, wkv, conv_w8,
                                 w_branch_b.astype(BF16), seq)
    attn = _mla_attn(q, k, v, batch, seq)

    wr = _lanes([w_router_group, w_router_expert], LANES).astype(BF16)
    br = _lanes([row(b_router_group), row(b_router_expert)], LANES)
    x2, h3, meta, wts, cnt = _post_attn(x2d, attn, ga, mixb, w_branch_a.astype(BF16),
                                   w_out.astype(BF16), row(g_xattn), w_q_mem.astype(BF16), mem2d, row(g_mem),
                                   w_kv_mem.astype(BF16),
                                   w_o_mem.astype(BF16), row(g_moe), wr, br, seq)

    counts = cnt[0, ROUTE_LANE0:ROUTE_LANE0 + N_EXPERTS]
    padded = (counts + SLOT_BLOCK - 1) // SLOT_BLOCK * SLOT_BLOCK
    pad_end = jnp.cumsum(padded)
    pad_start = pad_end - padded
    n_slots = (2 * n + SLOT_BLOCK - 1) // SLOT_BLOCK * SLOT_BLOCK + N_EXPERTS * SLOT_BLOCK
    routed = meta[0:4].astype(I32)
    expert_col = jnp.arange(N_EXPERTS, dtype=I32)[:, None, None]
    dest = jnp.sum(jnp.where(routed[None, 0:2] == expert_col, pad_start[:, None, None], 0),
                   axis=0) + routed[2:4]
    dest0 = dest[0].reshape(n // TROW, 1, TROW)
    dest1 = dest[1].reshape(n // TROW, 1, TROW)
    block_lo = jnp.arange(n_slots // SLOT_BLOCK, dtype=I32) * SLOT_BLOCK
    block_expert = jnp.minimum(jnp.sum(pad_end[None, :] <= block_lo[:, None], axis=1),
                               N_EXPERTS - 1).astype(I32)
    n_blocks_used = (pad_end[-1:] // SLOT_BLOCK).astype(I32)
    fill_blk = ((block_lo + SLOT_BLOCK == pad_end[block_expert])
                | (block_lo >= pad_end[-1])).astype(I32)

    cand = jnp.where(padded > 0, jnp.arange(N_EXPERTS, dtype=I32), N_EXPERTS)
    later = jnp.concatenate([lax.cummin(cand, reverse=True)[1:],
                             jnp.full((1,), N_EXPERTS, I32)])
    next_expert = jnp.where(later < N_EXPERTS, later, -1)[block_expert].astype(I32)

    xd = _dispatch(fill_blk, dest0, dest1, h3, n_slots)
    yd = _experts(block_expert, n_blocks_used, next_expert, xd, w_gate_up, w_down)
    return _combine(dest0, dest1, x2, wts, row(g_final), yd)


def kernel(x, mem, positions, g_mix, w_in, b_gate, g_cq, w_uq, g_ckv, w_ukv, conv_w, w_branch_a,
           w_branch_b, w_out, g_xattn, g_mem, w_q_mem, w_kv_mem, w_o_mem, g_moe, w_router_group,
           b_router_group, w_router_expert, b_router_expert, w_gate_up, w_down, g_final):
    batch, seq, d = x.shape
    depth = g_mix.shape[0]
    assert d == D_MODEL and depth == 1 and seq % TM == 0 and seq % TQ == 0 and TQ == TK
    n = batch * seq
    x2d = x.reshape(n, d)
    pos3d = positions.astype(I32).reshape(n // TM, TM // LANES, LANES)
    inv_freq = ROPE_THETA ** (-jnp.arange(0, QK_ROPE_DIM, 2, dtype=F32) / QK_ROPE_DIM)
    invf = _lanes([ROPE_LO, inv_freq.reshape(1, -1), inv_freq.reshape(1, -1)], HEAD_SLOT)
    l = 0
    out = _layer(x2d, pos3d, invf, mem.reshape(batch * MEM_LEN, d), g_mem[l], w_kv_mem[l], batch,
                 seq, g_mix[l], w_in[l], b_gate[l], g_cq[l],
                 w_uq[l], g_ckv[l], w_ukv[l], conv_w[l], w_branch_a[l], w_branch_b[l], w_out[l],
                 g_xattn[l], w_q_mem[l], w_o_mem[l], g_moe[l], w_router_group[l],
                 b_router_group[l], w_router_expert[l], b_router_expert[l], w_gate_up[l],
                 w_down[l], g_final)
    return out.reshape(batch, seq, d)
```

```python
import functools
import math

import jax
import jax.numpy as jnp
from jax import lax
from jax.experimental import pallas as pl
from jax.experimental.pallas import tpu as pltpu

F32 = jnp.float32
BF16 = jnp.bfloat16
I32 = jnp.int32

D_MODEL = 1024
MEM_LEN = 256
MLA_HEADS = 8
QK_NOPE_DIM = 64
QK_ROPE_DIM = 32
V_HEAD_DIM = 64
Q_LORA_RANK = 256
KV_LORA_RANK = 128
ROPE_THETA = 10000.0
CONV_WIDTH = 512
XATTN_HEADS = 4
XATTN_HEAD_DIM = 128
N_GROUPS = 8
EXPERTS_PER_GROUP = 8
N_EXPERTS = N_GROUPS * EXPERTS_PER_GROUP
EXPERT_FF = 256
EPS = 1e-6

COL_CQ = Q_LORA_RANK
COL_CKV = COL_CQ + KV_LORA_RANK
COL_KROPE = COL_CKV + QK_ROPE_DIM
COL_CONV = COL_KROPE + 3 * CONV_WIDTH

LANES = 128
HEAD_SLOT = LANES
ROPE_LO = QK_NOPE_DIM
ROPE_HALF = QK_ROPE_DIM // 2
L_LANE = V_HEAD_DIM
SLOT_BLOCK = 256
ROUTE_LANE0 = N_GROUPS
NEG_BIG = -1e30

TM = 512
TQ = 512
TK = 512
META_ROWS = 8
POST_PARTS = 1
DIAG_STRIPS = 2
ATTN_CHAINS = 8
TROW = 256
DMA_THREADS = 2
ROW_GROUP = 8
VMEM_LIMIT = 56 * 1024 * 1024


def _rms(x, g):
    return x * lax.rsqrt(jnp.mean(x * x, axis=-1, keepdims=True) + EPS) * g


def _const_spec(shape):
    nd = len(shape)
    return pl.BlockSpec(shape, lambda *_: (0,) * nd)


def _params(sem):
    return pltpu.CompilerParams(dimension_semantics=sem, vmem_limit_bytes=VMEM_LIMIT)


def _in_proj_kernel(x_ref, pos_ref, invf_ref, g_mix_ref, w1_ref, w2_ref, w3_ref, b_gate_ref,
                    g_cq_ref, wq_ref, g_ckv_ref, wkk_ref, wkv_ref, conv_w_ref, wbb_ref,
                    q_ref, k_ref, v_ref, ga_ref, mixb_ref, carry_ref, *, tiles_per_seq):
    step = pl.program_id(0)
    tm = x_ref.shape[0]
    h = _rms(x_ref[...], g_mix_ref[...]).astype(BF16)

    lane = lax.broadcasted_iota(I32, (tm, HEAD_SLOT), 1)
    rope_lane = (lane >= ROPE_LO) & (lane < ROPE_LO + QK_ROPE_DIM)
    pos_rows = pos_ref[0].astype(F32)
    pos_tile = jnp.concatenate([jnp.broadcast_to(pos_rows[g:g + 1, :], (LANES, LANES))
                                for g in range(tm // LANES)], axis=0)
    row_lane = lax.broadcasted_iota(I32, (tm, LANES), 0) & (LANES - 1)
    pos_col = jnp.sum(jnp.where(lane == row_lane, pos_tile, 0.0), axis=1, keepdims=True)
    ang = pos_col * invf_ref[...]
    first_half = lane < ROPE_LO + ROPE_HALF
    cos_t = jnp.where(rope_lane, jnp.cos(ang), 0.0)
    sin_t = jnp.where(rope_lane, jnp.sin(ang), 0.0)
    sin_t = jnp.where(first_half, -sin_t, sin_t)

    def swap_halves(t):
        return jnp.where(first_half, pltpu.roll(t, HEAD_SLOT - ROPE_HALF, 1),
                         pltpu.roll(t, ROPE_HALF, 1))

    p1 = jnp.dot(h, w1_ref[...], preferred_element_type=F32)
    c_q = p1[:, :COL_CQ]
    c_kv = p1[:, COL_CQ:COL_CKV]
    kr = p1[:, COL_CKV:COL_CKV + HEAD_SLOT]
    k_rot = kr * cos_t + swap_halves(kr) * sin_t

    scale = math.log2(math.e) / math.sqrt(QK_NOPE_DIM + QK_ROPE_DIM)
    q_cos = jnp.where(lane < ROPE_LO, scale, cos_t * scale)
    q_sin = sin_t * scale
    cqn = _rms(c_q, g_cq_ref[...]).astype(BF16)
    qq = jnp.dot(cqn, wq_ref[...], preferred_element_type=F32)
    ckvn = _rms(c_kv, g_ckv_ref[...]).astype(BF16)
    kk = jnp.dot(ckvn, wkk_ref[...], preferred_element_type=F32)
    vv = jnp.dot(ckvn, wkv_ref[...], preferred_element_type=F32)
    ones_lane = jnp.where(lane == L_LANE, 1.0, 0.0)
    for hd in range(MLA_HEADS):
        sl = slice(hd * HEAD_SLOT, (hd + 1) * HEAD_SLOT)
        q_ref[:, sl] = (qq[:, sl] * q_cos + swap_halves(qq[:, sl]) * q_sin).astype(BF16)
        k_ref[:, sl] = (kk[:, sl] + k_rot).astype(BF16)
        v_ref[:, sl] = (vv[:, sl] + ones_lane).astype(BF16)

    p2 = jnp.dot(h, w2_ref[...], preferred_element_type=F32)
    b_gate = p2[:, :CONV_WIDTH]
    u = p2[:, CONV_WIDTH:2 * CONV_WIDTH] * p2[:, 2 * CONV_WIDTH:]

    @pl.when(step % tiles_per_seq == 0)
    def _():
        carry_ref[...] = jnp.zeros_like(carry_ref)

    prev1 = carry_ref[ROW_GROUP - 1:ROW_GROUP, :]
    prev2 = carry_ref[ROW_GROUP - 2:ROW_GROUP - 1, :]
    row = lax.broadcasted_iota(I32, (tm, CONV_WIDTH), 0)
    u1 = jnp.where(row == 0, prev1, pltpu.roll(u, 1, 0))
    u2 = jnp.where(row == 0, prev2, jnp.where(row == 1, prev1, pltpu.roll(u, 2, 0)))
    carry_ref[...] = u[tm - ROW_GROUP:, :]
    cw = conv_w_ref[...]
    conv = b_gate * (cw[0:1, :] * u2 + cw[1:2, :] * u1 + cw[2:3, :] * u)
    branch_b = jnp.dot(conv.astype(BF16), wbb_ref[...], preferred_element_type=F32)

    gates = jax.nn.sigmoid(jnp.dot(h, w3_ref[...], preferred_element_type=F32) + b_gate_ref[...])
    ga_ref[...] = gates[:, :D_MODEL].astype(BF16)
    mixb_ref[...] = (gates[:, D_MODEL:] * branch_b).astype(BF16)


def _in_proj(x2d, pos3d, invf, g_mix, w1, w2, w3, b_gate, g_cq, wq, g_ckv, wkk, wkv,
             conv_w, wbb, seq):
    n = x2d.shape[0]
    row_spec = lambda c: pl.BlockSpec((TM, c), lambda i: (i, 0))
    consts = [invf, g_mix, w1, w2, w3, b_gate, g_cq, wq, g_ckv, wkk, wkv, conv_w, wbb]
    kv_cols = MLA_HEADS * HEAD_SLOT
    return pl.pallas_call(
        functools.partial(_in_proj_kernel, tiles_per_seq=seq // TM),
        grid=(n // TM,),
        in_specs=[row_spec(D_MODEL), pl.BlockSpec((1, TM // LANES, LANES), lambda i: (i, 0, 0))]
        + [_const_spec(c.shape) for c in consts],
        out_specs=[row_spec(kv_cols), row_spec(kv_cols), row_spec(kv_cols),
                   row_spec(D_MODEL), row_spec(D_MODEL)],
        out_shape=[jax.ShapeDtypeStruct((n, kv_cols), BF16),
                   jax.ShapeDtypeStruct((n, kv_cols), BF16),
                   jax.ShapeDtypeStruct((n, kv_cols), BF16),
                   jax.ShapeDtypeStruct((n, D_MODEL), BF16),
                   jax.ShapeDtypeStruct((n, D_MODEL), BF16)],
        scratch_shapes=[pltpu.VMEM((ROW_GROUP, CONV_WIDTH), F32)],
        compiler_params=_params(("arbitrary",)),
        name="in_proj",
    )(x2d, pos3d, *consts)


def _attn_kernel(q_ref, k_ref, v_ref, o_ref):
    qi = pl.program_id(1)
    tq = q_ref.shape[0]

    def block(carry, heads, col_start, ncols, row_lo, visible):
        out = []
        for c, hd in enumerate(heads):
            m, acc = carry[c]
            hsl = slice(hd * HEAD_SLOT, (hd + 1) * HEAD_SLOT)
            kb = k_ref[pl.ds(col_start, ncols), hsl]
            vb = v_ref[pl.ds(col_start, ncols), hsl]
            s = lax.dot_general(q_ref[row_lo:, hsl], kb, (((1,), (1,)), ((), ())),
                                preferred_element_type=F32)
            if visible is not None:
                s = jnp.where(visible, s, NEG_BIG)
            m_hi, acc_hi = m[row_lo:], acc[row_lo:]
            m_new = jnp.maximum(m_hi, jnp.max(s, axis=1, keepdims=True))
            p = jnp.exp2(s - m_new).astype(BF16)
            acc_new = jnp.exp2(m_hi - m_new) * acc_hi + jnp.dot(p, vb, preferred_element_type=F32)
            if row_lo:
                m_new = jnp.concatenate([m[:row_lo], m_new], axis=0)
                acc_new = jnp.concatenate([acc[:row_lo], acc_new], axis=0)
            out.append((m_new, acc_new))
        return tuple(out)

    def full_step(j, carry, heads):
        return block(carry, heads, pl.multiple_of(j * TK, TK), TK, 0, None)

    lane = lax.broadcasted_iota(I32, (tq, HEAD_SLOT), 1)
    strip_masks = []
    for d in range(DIAG_STRIPS):
        shape = (tq - d * (tq // DIAG_STRIPS), TK // DIAG_STRIPS)
        strip_masks.append(lax.broadcasted_iota(I32, shape, 1) <= lax.broadcasted_iota(I32, shape, 0))
    for g in range(MLA_HEADS // ATTN_CHAINS):
        heads = tuple(range(g * ATTN_CHAINS, (g + 1) * ATTN_CHAINS))
        carry = tuple((jnp.full((tq, 1), NEG_BIG, F32), jnp.zeros((tq, HEAD_SLOT), F32))
                      for _ in heads)
        carry = lax.fori_loop(0, qi, functools.partial(full_step, heads=heads), carry)
        for d in range(DIAG_STRIPS):
            width, row_lo = TK // DIAG_STRIPS, d * (tq // DIAG_STRIPS)
            col = pl.multiple_of(qi * TK + d * width, width)
            carry = block(carry, heads, col, width, row_lo, strip_masks[d])
        outs = []
        for _, acc in carry:
            denom = jnp.sum(jnp.where(lane == L_LANE, acc, 0.0), axis=1, keepdims=True)
            outs.append(acc / denom)
        for c in range(0, ATTN_CHAINS, 2):
            hp = heads[c] // 2
            pair = jnp.where(lane < V_HEAD_DIM, outs[c], pltpu.roll(outs[c + 1], V_HEAD_DIM, 1))
            o_ref[:, hp * HEAD_SLOT:(hp + 1) * HEAD_SLOT] = pair.astype(BF16)


def _mla_attn(q, k, v, batch, seq):
    n = q.shape[0]
    nq = seq // TQ
    out_cols = MLA_HEADS * V_HEAD_DIM
    return pl.pallas_call(
        _attn_kernel,
        grid=(batch, nq),
        in_specs=[pl.BlockSpec((TQ, q.shape[1]), lambda b, i: (b * nq + i, 0)),
                  pl.BlockSpec((seq, k.shape[1]), lambda b, i: (b, 0)),
                  pl.BlockSpec((seq, v.shape[1]), lambda b, i: (b, 0))],
        out_specs=pl.BlockSpec((TQ, out_cols), lambda b, i: (b * nq + i, 0)),
        out_shape=jax.ShapeDtypeStruct((n, out_cols), BF16),
        compiler_params=_params(("arbitrary", "arbitrary")),
        name="mla_attn",
    )(q, k, v)


def _post_attn_kernel(x_ref, attn_ref, ga_ref, mixb_ref, wba_ref, wout_ref, g_x_ref, wq_ref,
                      mem_ref, g_mem_ref, wkv_ref, wo_ref, g_moe_ref, wr_ref, br_ref,
                      x2_ref, h3_ref, meta_ref, wts_ref, cnt_ref, run_ref, km_ref, vm_ref,
                      *, tiles_per_seq):
    step = pl.program_id(0)
    tm = x_ref.shape[0]
    xdim = XATTN_HEADS * XATTN_HEAD_DIM

    @pl.when(step == 0)
    def _():
        run_ref[...] = jnp.zeros_like(run_ref)

    @pl.when(step % tiles_per_seq == 0)
    def _():
        mem_n = _rms(mem_ref[...], g_mem_ref[...]).astype(BF16)
        kv = jnp.dot(mem_n, wkv_ref[...], preferred_element_type=F32)
        km_ref[...] = kv[:, :xdim].astype(BF16)
        vm_ref[...] = kv[:, xdim:].astype(BF16)

    def mixer_and_xattn(rows):
        a = jnp.dot(attn_ref[rows, :], wba_ref[...], preferred_element_type=F32)
        mixed = ga_ref[rows, :].astype(F32) * a + mixb_ref[rows, :].astype(F32)
        x1 = x_ref[rows, :] + jnp.dot(mixed.astype(BF16), wout_ref[...], preferred_element_type=F32)

        h2 = _rms(x1, g_x_ref[...]).astype(BF16)
        qm = jnp.dot(h2, wq_ref[...], preferred_element_type=F32) * (1.0 / math.sqrt(XATTN_HEAD_DIM))
        qm = qm.astype(BF16)
        heads = []
        for hd in range(XATTN_HEADS):
            sl = slice(hd * XATTN_HEAD_DIM, (hd + 1) * XATTN_HEAD_DIM)
            s = lax.dot_general(qm[:, sl], km_ref[:, sl], (((1,), (1,)), ((), ())),
                                preferred_element_type=F32)
            p = jnp.exp(s - jnp.max(s, axis=1, keepdims=True))
            o = jnp.dot(p.astype(BF16), vm_ref[:, sl], preferred_element_type=F32)
            heads.append((o / jnp.sum(p, axis=1, keepdims=True)).astype(BF16))
        o_all = jnp.concatenate(heads, axis=1)
        x2 = x1 + jnp.dot(o_all, wo_ref[...], preferred_element_type=F32)
        x2_ref[rows, :] = x2
        h3 = _rms(x2, g_moe_ref[...])
        g0, g1 = rows.start // ROW_GROUP, rows.stop // ROW_GROUP
        h3_ref[g0:g1] = h3.reshape(g1 - g0, ROW_GROUP, D_MODEL)
        return jnp.dot(h3.astype(BF16), wr_ref[...], preferred_element_type=F32) + br_ref[...]

    part = tm // POST_PARTS
    logits = jnp.concatenate([mixer_and_xattn(slice(c * part, (c + 1) * part))
                              for c in range(POST_PARTS)], axis=0)

    lane = lax.broadcasted_iota(I32, (tm, LANES), 1)
    lane_f = lane.astype(F32)

    def first_lane(hit):
        return jnp.min(jnp.where(hit, lane_f, float(LANES)), axis=1, keepdims=True)

    gl = jnp.where(lane < N_GROUPS, logits, -jnp.inf)
    gmax = jnp.max(gl, axis=1, keepdims=True)
    g_idx = first_lane(gl == gmax).astype(I32)
    p_group = 1.0 / jnp.sum(jnp.exp(gl - gmax), axis=1, keepdims=True)
    e_lo = ROUTE_LANE0 + g_idx * EXPERTS_PER_GROUP
    el = jnp.where((lane >= e_lo) & (lane < e_lo + EXPERTS_PER_GROUP), logits, -jnp.inf)
    v1 = jnp.max(el, axis=1, keepdims=True)
    i1 = first_lane(el == v1)
    el2 = jnp.where(lane_f == i1, -jnp.inf, el)
    v2 = jnp.max(el2, axis=1, keepdims=True)
    i2 = first_lane(el2 == v2)
    t = jnp.exp(v2 - v1)
    w0 = p_group / (1.0 + t)
    w1 = p_group * t / (1.0 + t)

    oh0 = lane_f == i1
    oh1 = lane_f == i2
    oh = (oh0 | oh1).astype(BF16)
    r_i = lax.broadcasted_iota(I32, (tm, tm), 0)
    c_i = lax.broadcasted_iota(I32, (tm, tm), 1)
    lower = (c_i < r_i).astype(BF16)
    before = jnp.dot(lower, oh, preferred_element_type=F32) + run_ref[...]
    rank0 = jnp.sum(jnp.where(oh0, before, 0.0), axis=1, keepdims=True)
    rank1 = jnp.sum(jnp.where(oh1, before, 0.0), axis=1, keepdims=True)
    run_ref[...] = run_ref[...] + jnp.sum(oh.astype(F32), axis=0, keepdims=True)
    cnt_ref[...] = jnp.broadcast_to(run_ref[...], cnt_ref.shape).astype(I32)

    meta = jnp.where(lane == 0, i1 - ROUTE_LANE0, 0.0)
    meta = jnp.where(lane == 1, i2 - ROUTE_LANE0, meta)
    meta = jnp.where(lane == 2, rank0, meta)
    meta = jnp.where(lane == 3, rank1, meta)
    meta_ref[...] = meta.T[:META_ROWS, :]
    wts_ref[...] = jnp.where(lane == 0, w0, jnp.where(lane == 1, w1, 0.0))


def _post_attn(x2d, attn, ga, mixb, wba, wout, g_x, wq, mem2d, g_mem, wkv, wo, g_moe, wr, br, seq):
    n = x2d.shape[0]
    tps = seq // TM
    xdim = XATTN_HEADS * XATTN_HEAD_DIM
    row_spec = lambda c: pl.BlockSpec((TM, c), lambda i: (i, 0))
    consts = lambda *arrs: [_const_spec(a.shape) for a in arrs]
    return pl.pallas_call(
        functools.partial(_post_attn_kernel, tiles_per_seq=tps),
        grid=(n // TM,),
        in_specs=([row_spec(D_MODEL), row_spec(MLA_HEADS * V_HEAD_DIM), row_spec(D_MODEL),
                   row_spec(D_MODEL)] + consts(wba, wout, g_x, wq)
                  + [pl.BlockSpec((MEM_LEN, D_MODEL), lambda i: (i // tps, 0))]
                  + consts(g_mem, wkv, wo, g_moe, wr, br)),
        out_specs=[row_spec(D_MODEL),
                   pl.BlockSpec((TM // ROW_GROUP, ROW_GROUP, D_MODEL), lambda i: (i, 0, 0)),
                   pl.BlockSpec((META_ROWS, TM), lambda i: (0, i)), row_spec(LANES),
                   _const_spec((ROW_GROUP, LANES))],
        out_shape=[jax.ShapeDtypeStruct((n, D_MODEL), F32),
                   jax.ShapeDtypeStruct((n // ROW_GROUP, ROW_GROUP, D_MODEL), F32),
                   jax.ShapeDtypeStruct((META_ROWS, n), F32),
                   jax.ShapeDtypeStruct((n, LANES), F32),
                   jax.ShapeDtypeStruct((ROW_GROUP, LANES), I32)],
        scratch_shapes=[pltpu.VMEM((1, LANES), F32), pltpu.VMEM((MEM_LEN, xdim), BF16),
                        pltpu.VMEM((MEM_LEN, xdim), BF16)],
        compiler_params=_params(("arbitrary",)),
        name="post_attn",
    )(x2d, attn, ga, mixb, wba, wout, g_x, wq, mem2d, g_mem, wkv, wo, g_moe, wr, br)


def _tile_row(ref, group, k):
    return ref.at[group, pl.ds(k, 1), :]


def _hbm_row(ref, row):
    return ref.at[pl.ds(row, 1), :]


def _dispatch_kernel(fill_ref, dest0_ref, dest1_ref, zeros_ref, h3_ref, xd_ref, row_sem, fill_sem):
    i = pl.program_id(0)

    def fill_copy(b):
        return pltpu.make_async_copy(
            zeros_ref, xd_ref.at[pl.ds(pl.multiple_of(b * SLOT_BLOCK, SLOT_BLOCK), SLOT_BLOCK)],
            fill_sem)

    @pl.when(i == 0)
    def _():
        def start(b, c):
            @pl.when(fill_ref[b] > 0)
            def _():
                fill_copy(b).start()
            return c

        def wait(b, c):
            @pl.when(fill_ref[b] > 0)
            def _():
                fill_copy(b).wait()
            return c

        lax.fori_loop(0, fill_ref.shape[0], start, 0)
        lax.fori_loop(0, fill_ref.shape[0], wait, 0)

    def copies(g, k):
        r = g * ROW_GROUP + k
        src = _tile_row(h3_ref, g, k)
        return (pltpu.make_async_copy(src, _hbm_row(xd_ref, dest0_ref[0, 0, r]), row_sem),
                pltpu.make_async_copy(src, _hbm_row(xd_ref, dest1_ref[0, 0, r]), row_sem))

    def issue(g, c):
        for k in range(ROW_GROUP):
            for j, cp in enumerate(copies(g, k)):
                cp.start(priority=j % DMA_THREADS)
        return c

    def drain(g, c):
        for k in range(ROW_GROUP):
            for cp in copies(g, k):
                cp.wait()
        return c

    lax.fori_loop(0, TROW // ROW_GROUP, issue, 0)
    lax.fori_loop(0, TROW // ROW_GROUP, drain, 0)


def _dispatch(fill_blk, dest0, dest1, h3, n_slots):
    groups, _, d = h3.shape
    n = groups * ROW_GROUP
    grid_spec = pltpu.PrefetchScalarGridSpec(
        num_scalar_prefetch=1,
        grid=(n // TROW,),
        in_specs=[pl.BlockSpec((1, 1, TROW), lambda i, fb: (i, 0, 0), memory_space=pltpu.SMEM),
                  pl.BlockSpec((1, 1, TROW), lambda i, fb: (i, 0, 0), memory_space=pltpu.SMEM),
                  pl.BlockSpec((SLOT_BLOCK, d), lambda i, fb: (0, 0)),
                  pl.BlockSpec((TROW // ROW_GROUP, ROW_GROUP, d), lambda i, fb: (i, 0, 0))],
        out_specs=pl.BlockSpec(memory_space=pl.ANY),
        scratch_shapes=[pltpu.SemaphoreType.DMA, pltpu.SemaphoreType.DMA],
    )
    return pl.pallas_call(
        _dispatch_kernel,
        grid_spec=grid_spec,
        out_shape=jax.ShapeDtypeStruct((n_slots, d), F32),
        compiler_params=_params(("arbitrary",)),
        name="dispatch",
    )(fill_blk, dest0, dest1, jnp.zeros((SLOT_BLOCK, d), F32), h3)


def _experts_kernel(be_ref, nb_ref, nxt_ref, xd_ref, wgu_hbm, wdn_hbm, yd_ref,
                    wgu_f32, wdn_f32, wgu_bf, wdn_bf, slot_ref, sem):
    i = pl.program_id(0)

    def weight_copies(e, slot):
        return (pltpu.make_async_copy(wgu_hbm.at[e], wgu_f32.at[slot], sem.at[slot]),
                pltpu.make_async_copy(wdn_hbm.at[e], wdn_f32.at[slot], sem.at[slot]))

    @pl.when(i == 0)
    def _():
        slot_ref[0] = 0
        for cp in weight_copies(be_ref[0], 0):
            cp.start()

    @pl.when(i < nb_ref[0])
    def _():
        @pl.when((i == 0) | (be_ref[i] != be_ref[jnp.maximum(i - 1, 0)]))
        def _():
            slot = slot_ref[0]
            for cp in weight_copies(be_ref[i], slot):
                cp.wait()
            wgu_bf[...] = wgu_f32[slot].astype(BF16)
            wdn_bf[...] = wdn_f32[slot].astype(BF16)

            @pl.when(nxt_ref[i] >= 0)
            def _():
                for cp in weight_copies(nxt_ref[i], 1 - slot):
                    cp.start()

            slot_ref[0] = 1 - slot

        gu = jnp.dot(xd_ref[...].astype(BF16), wgu_bf[...], preferred_element_type=F32)
        gate = gu[:, :EXPERT_FF]
        up = gu[:, EXPERT_FF:]
        act = (gate * jax.nn.sigmoid(gate) * up).astype(BF16)
        yd_ref[...] = jnp.dot(act, wdn_bf[...], preferred_element_type=F32)


def _experts(block_expert, n_blocks_used, next_expert, xd, w_gate_up, w_down):
    n_slots, d = xd.shape
    slot_map = lambda i, be, nb, nx: (jnp.minimum(i, nb[0] - 1), 0)
    grid_spec = pltpu.PrefetchScalarGridSpec(
        num_scalar_prefetch=3,
        grid=(n_slots // SLOT_BLOCK,),
        in_specs=[pl.BlockSpec((SLOT_BLOCK, d), slot_map),
                  pl.BlockSpec(memory_space=pl.ANY),
                  pl.BlockSpec(memory_space=pl.ANY)],
        out_specs=pl.BlockSpec((SLOT_BLOCK, d), slot_map),
        scratch_shapes=[pltpu.VMEM((2, D_MODEL, 2 * EXPERT_FF), F32),
                        pltpu.VMEM((2, EXPERT_FF, D_MODEL), F32),
                        pltpu.VMEM((D_MODEL, 2 * EXPERT_FF), BF16),
                        pltpu.VMEM((EXPERT_FF, D_MODEL), BF16),
                        pltpu.SMEM((1,), I32),
                        pltpu.SemaphoreType.DMA((2,))],
    )
    return pl.pallas_call(
        _experts_kernel,
        grid_spec=grid_spec,
        out_shape=jax.ShapeDtypeStruct(xd.shape, F32),
        input_output_aliases={3: 0},
        compiler_params=_params(("arbitrary",)),
        name="experts",
    )(block_expert, n_blocks_used, next_expert, xd, w_gate_up, w_down)


def _combine_kernel(dest0_ref, dest1_ref, next0_ref, next1_ref, x2_ref, wts_ref, g_ref, yd_ref, o_ref,
                    buf, sem):
    i = pl.program_id(0)
    last = pl.num_programs(0) - 1
    rows = x2_ref.shape[0]
    slot = i % 2

    def copies(d_refs, slt, g, k):
        r = g * ROW_GROUP + k
        return tuple(pltpu.make_async_copy(_hbm_row(yd_ref, d_ref[0, 0, r]),
                                           _tile_row(buf.at[slt, j], g, k), sem.at[slt])
                     for j, d_ref in enumerate(d_refs))

    def issue(d_ref, slt):
        def body(g, c):
            for k in range(ROW_GROUP):
                for j, cp in enumerate(copies(d_ref, slt, g, k)):
                    cp.start(priority=j % DMA_THREADS)
            return c
        lax.fori_loop(0, rows // ROW_GROUP, body, 0)

    @pl.when(i == 0)
    def _():
        issue((dest0_ref, dest1_ref), 0)

    @pl.when(i < last)
    def _():
        issue((next0_ref, next1_ref), 1 - slot)

    def drain(g, c):
        for k in range(ROW_GROUP):
            for cp in copies((dest0_ref, dest1_ref), slot, g, k):
                cp.wait()
        return c

    lax.fori_loop(0, rows // ROW_GROUP, drain, 0)

    lane = lax.broadcasted_iota(I32, (rows, LANES), 1)
    wts = wts_ref[...]
    w0 = jnp.sum(jnp.where(lane == 0, wts, 0.0), axis=1, keepdims=True)
    w1 = jnp.sum(jnp.where(lane == 1, wts, 0.0), axis=1, keepdims=True)
    y0 = buf[slot, 0].reshape(rows, D_MODEL)
    y1 = buf[slot, 1].reshape(rows, D_MODEL)
    o_ref[...] = _rms(x2_ref[...] + y0 * w0 + y1 * w1, g_ref[...])


def _combine(dest0, dest1, x2, wts, g_final, yd):
    n = x2.shape[0]
    steps = n // TROW
    dest_spec = lambda f: pl.BlockSpec((1, 1, TROW), f, memory_space=pltpu.SMEM)
    cur = lambda i: (i, 0, 0)
    nxt = lambda i: (jnp.minimum(i + 1, steps - 1), 0, 0)
    return pl.pallas_call(
        _combine_kernel,
        grid=(steps,),
        in_specs=[dest_spec(cur), dest_spec(cur), dest_spec(nxt), dest_spec(nxt),
                  pl.BlockSpec((TROW, D_MODEL), lambda i: (i, 0)),
                  pl.BlockSpec((TROW, LANES), lambda i: (i, 0)),
                  _const_spec((1, D_MODEL)),
                  pl.BlockSpec(memory_space=pl.ANY)],
        out_specs=pl.BlockSpec((TROW, D_MODEL), lambda i: (i, 0)),
        out_shape=jax.ShapeDtypeStruct((n, D_MODEL), F32),
        scratch_shapes=[pltpu.VMEM((2, 2, TROW // ROW_GROUP, ROW_GROUP, D_MODEL), F32),
                        pltpu.SemaphoreType.DMA((2,))],
        compiler_params=_params(("arbitrary",)),
        name="combine",
    )(dest0, dest1, dest0, dest1, x2, wts, g_final, yd)


def _lanes(parts, width):
    ref = next(p for p in parts if not isinstance(p, int))
    cols = [jnp.zeros(ref.shape[:-1] + (p,), ref.dtype) if isinstance(p, int) else p for p in parts]
    used = sum(c.shape[-1] for c in cols)
    assert used <= width
    if used < width:
        cols.append(jnp.zeros(ref.shape[:-1] + (width - used,), ref.dtype))
    return jnp.concatenate(cols, axis=-1)


def _head_slots(w, per_head, pieces):
    rows = w.shape[0]
    wh = w.reshape(rows, MLA_HEADS, per_head)
    return _lanes(pieces(wh), HEAD_SLOT).reshape(rows, MLA_HEADS * HEAD_SLOT)


def _layer(x2d, pos3d, invf, mem2d, g_mem, w_kv_mem, batch, seq, g_mix, w_in, b_gate, g_cq, w_uq, g_ckv, w_ukv,
           conv_w, w_branch_a, w_branch_b, w_out, g_xattn, w_q_mem, w_o_mem, g_moe,
           w_router_group, b_router_group, w_router_expert, b_router_expert, w_gate_up, w_down,
           g_final):
    n = x2d.shape[0]
    row = lambda v: v.reshape(1, -1)

    w_kr = w_in[:, COL_CKV:COL_KROPE]
    w1 = _lanes([w_in[:, :COL_CKV], ROPE_LO, w_kr], COL_CKV + HEAD_SLOT).astype(BF16)
    w2 = w_in[:, COL_KROPE:COL_CONV].astype(BF16)
    w3 = w_in[:, COL_CONV:].astype(BF16)
    qd = QK_NOPE_DIM + QK_ROPE_DIM
    wq = _head_slots(w_uq, qd, lambda wh: [wh]).astype(BF16)
    kvd = QK_NOPE_DIM + V_HEAD_DIM
    wkk = _head_slots(w_ukv, kvd, lambda wh: [wh[..., :QK_NOPE_DIM]]).astype(BF16)
    wkv = _head_slots(w_ukv, kvd, lambda wh: [wh[..., QK_NOPE_DIM:]]).astype(BF16)
    conv_w8 = jnp.concatenate(
        [conv_w, jnp.zeros((ROW_GROUP - conv_w.shape[0], CONV_WIDTH), F32)], axis=0)

    q, k, v, ga, mixb = _in_proj(x2d, pos3d, invf, row(g_mix), w1, w2, w3, row(b_gate), row(g_cq),
                                 wq, row(g_ckv), wkk, wkv, conv_w8,
                                 w_branch_b.astype(BF16), seq)
    attn = _mla_attn(q, k, v, batch, seq)

    wr = _lanes([w_router_group, w_router_expert], LANES).astype(BF16)
    br = _lanes([row(b_router_group), row(b_router_expert)], LANES)
    x2, h3, meta, wts, cnt = _post_attn(x2d, attn, ga, mixb, w_branch_a.astype(BF16),
                                   w_out.astype(BF16), row(g_xattn), w_q_mem.astype(BF16), mem2d, row(g_mem),
                                   w_kv_mem.astype(BF16),
                                   w_o_mem.astype(BF16), row(g_moe), wr, br, seq)

    counts = cnt[0, ROUTE_LANE0:ROUTE_LANE0 + N_EXPERTS]
    padded = (counts + SLOT_BLOCK - 1) // SLOT_BLOCK * SLOT_BLOCK
    pad_end = jnp.cumsum(padded)
    pad_start = pad_end - padded
    n_slots = (2 * n + SLOT_BLOCK - 1) // SLOT_BLOCK * SLOT_BLOCK + N_EXPERTS * SLOT_BLOCK
    routed = meta[0:4].astype(I32)
    expert_col = jnp.arange(N_EXPERTS, dtype=I32)[:, None, None]
    dest = jnp.sum(jnp.where(routed[None, 0:2] == expert_col, pad_start[:, None, None], 0),
                   axis=0) + routed[2:4]
    dest0 = dest[0].reshape(n // TROW, 1, TROW)
    dest1 = dest[1].reshape(n // TROW, 1, TROW)
    block_lo = jnp.arange(n_slots // SLOT_BLOCK, dtype=I32) * SLOT_BLOCK
    block_expert = jnp.minimum(jnp.sum(pad_end[None, :] <= block_lo[:, None], axis=1),
                               N_EXPERTS - 1).astype(I32)
    n_blocks_used = (pad_end[-1:] // SLOT_BLOCK).astype(I32)
    fill_blk = ((block_lo + SLOT_BLOCK == pad_end[block_expert])
                | (block_lo >= pad_end[-1])).astype(I32)

    cand = jnp.where(padded > 0, jnp.arange(N_EXPERTS, dtype=I32), N_EXPERTS)
    later = jnp.concatenate([lax.cummin(cand, reverse=True)[1:],
                             jnp.full((1,), N_EXPERTS, I32)])
    next_expert = jnp.where(later < N_EXPERTS, later, -1)[block_expert].astype(I32)

    xd = _dispatch(fill_blk, dest0, dest1, h3, n_slots)
    yd = _experts(block_expert, n_blocks_used, next_expert, xd, w_gate_up, w_down)
    return _combine(dest0, dest1, x2, wts, row(g_final), yd)


def kernel(x, mem, positions, g_mix, w_in, b_gate, g_cq, w_uq, g_ckv, w_ukv, conv_w, w_branch_a,
           w_branch_b, w_out, g_xattn, g_mem, w_q_mem, w_kv_mem, w_o_mem, g_moe, w_router_group,
           b_router_group, w_router_expert, b_router_expert, w_gate_up, w_down, g_final):
    batch, seq, d = x.shape
    depth = g_mix.shape[0]
    assert d == D_MODEL and depth == 1 and seq % TM == 0 and seq % TQ == 0 and TQ == TK
    n = batch * seq
    x2d = x.reshape(n, d)
    pos3d = positions.astype(I32).reshape(n // TM, TM // LANES, LANES)
    inv_freq = ROPE_THETA ** (-jnp.arange(0, QK_ROPE_DIM, 2, dtype=F32) / QK_ROPE_DIM)
    invf = _lanes([ROPE_LO, inv_freq.reshape(1, -1), inv_freq.reshape(1, -1)], HEAD_SLOT)
    l = 0
    out = _layer(x2d, pos3d, invf, mem.reshape(batch * MEM_LEN, d), g_mem[l], w_kv_mem[l], batch,
                 seq, g_mix[l], w_in[l], b_gate[l], g_cq[l],
                 w_uq[l], g_ckv[l], w_ukv[l], conv_w[l], w_branch_a[l], w_branch_b[l], w_out[l],
                 g_xattn[l], w_q_mem[l], w_o_mem[l], g_moe[l], w_router_group[l],
                 b_router_group[l], w_router_expert[l], b_router_expert[l], w_gate_up[l],
                 w_down[l], g_final)
    return out.reshape(batch, seq, d)
```

```python
import functools
import math

import jax
import jax.numpy as jnp
from jax import lax
from jax.experimental import pallas as pl
from jax.experimental.pallas import tpu as pltpu

F32 = jnp.float32
BF16 = jnp.bfloat16
I32 = jnp.int32

D_MODEL = 1024
MEM_LEN = 256
MLA_HEADS = 8
QK_NOPE_DIM = 64
QK_ROPE_DIM = 32
V_HEAD_DIM = 64
Q_LORA_RANK = 256
KV_LORA_RANK = 128
ROPE_THETA = 10000.0
CONV_WIDTH = 512
XATTN_HEADS = 4
XATTN_HEAD_DIM = 128
N_GROUPS = 8
EXPERTS_PER_GROUP = 8
N_EXPERTS = N_GROUPS * EXPERTS_PER_GROUP
EXPERT_FF = 256
EPS = 1e-6

COL_CQ = Q_LORA_RANK
COL_CKV = COL_CQ + KV_LORA_RANK
COL_KROPE = COL_CKV + QK_ROPE_DIM
COL_CONV = COL_KROPE + 3 * CONV_WIDTH

LANES = 128
HEAD_SLOT = LANES
ROPE_LO = QK_NOPE_DIM
ROPE_HALF = QK_ROPE_DIM // 2
L_LANE = V_HEAD_DIM
SLOT_BLOCK = 256
ROUTE_LANE0 = N_GROUPS
NEG_BIG = -1e30

TM = 1024
TQ = 512
TK = 512
META_ROWS = 8
POST_PARTS = 1
DIAG_STRIPS = 2
ATTN_CHAINS = 8
TROW = 256
DMA_THREADS = 2
ROW_GROUP = 8
VMEM_LIMIT = 56 * 1024 * 1024


def _rms(x, g):
    return x * lax.rsqrt(jnp.mean(x * x, axis=-1, keepdims=True) + EPS) * g


def _const_spec(shape):
    nd = len(shape)
    return pl.BlockSpec(shape, lambda *_: (0,) * nd)


def _params(sem):
    return pltpu.CompilerParams(dimension_semantics=sem, vmem_limit_bytes=VMEM_LIMIT)


def _in_proj_kernel(x_ref, pos_ref, invf_ref, g_mix_ref, w_ref, b_gate_ref,
                    g_cq_ref, wq_ref, g_ckv_ref, wkk_ref, wkv_ref, conv_w_ref, wbb_ref,
                    q_ref, k_ref, v_ref, ga_ref, mixb_ref, carry_ref, *, tiles_per_seq):
    step = pl.program_id(0)
    tm = x_ref.shape[0]
    h = _rms(x_ref[...], g_mix_ref[...]).astype(BF16)

    lane = lax.broadcasted_iota(I32, (tm, HEAD_SLOT), 1)
    rope_lane = (lane >= ROPE_LO) & (lane < ROPE_LO + QK_ROPE_DIM)
    pos_rows = pos_ref[0].astype(F32)
    pos_tile = jnp.concatenate([jnp.broadcast_to(pos_rows[g:g + 1, :], (LANES, LANES))
                                for g in range(tm // LANES)], axis=0)
    row_lane = lax.broadcasted_iota(I32, (tm, LANES), 0) & (LANES - 1)
    pos_col = jnp.sum(jnp.where(lane == row_lane, pos_tile, 0.0), axis=1, keepdims=True)
    ang = pos_col * invf_ref[...]
    first_half = lane < ROPE_LO + ROPE_HALF
    cos_t = jnp.where(rope_lane, jnp.cos(ang), 0.0)
    sin_t = jnp.where(rope_lane, jnp.sin(ang), 0.0)
    sin_t = jnp.where(first_half, -sin_t, sin_t)

    def swap_halves(t):
        return jnp.where(first_half, pltpu.roll(t, HEAD_SLOT - ROPE_HALF, 1),
                         pltpu.roll(t, ROPE_HALF, 1))

    proj = jnp.dot(h, w_ref[...], preferred_element_type=F32)
    n1, n2 = COL_CKV + HEAD_SLOT, COL_CKV + HEAD_SLOT + 3 * CONV_WIDTH
    p1 = proj[:, :n1]
    c_q = p1[:, :COL_CQ]
    c_kv = p1[:, COL_CQ:COL_CKV]
    kr = p1[:, COL_CKV:COL_CKV + HEAD_SLOT]
    k_rot = kr * cos_t + swap_halves(kr) * sin_t

    scale = math.log2(math.e) / math.sqrt(QK_NOPE_DIM + QK_ROPE_DIM)
    q_cos = jnp.where(lane < ROPE_LO, scale, cos_t * scale)
    q_sin = sin_t * scale
    cqn = _rms(c_q, g_cq_ref[...]).astype(BF16)
    qq = jnp.dot(cqn, wq_ref[...], preferred_element_type=F32)
    ckvn = _rms(c_kv, g_ckv_ref[...]).astype(BF16)
    kk = jnp.dot(ckvn, wkk_ref[...], preferred_element_type=F32)
    vv = jnp.dot(ckvn, wkv_ref[...], preferred_element_type=F32)
    ones_lane = jnp.where(lane == L_LANE, 1.0, 0.0)
    for hd in range(MLA_HEADS):
        sl = slice(hd * HEAD_SLOT, (hd + 1) * HEAD_SLOT)
        q_ref[:, sl] = (qq[:, sl] * q_cos + swap_halves(qq[:, sl]) * q_sin).astype(BF16)
        k_ref[:, sl] = (kk[:, sl] + k_rot).astype(BF16)
        v_ref[:, sl] = (vv[:, sl] + ones_lane).astype(BF16)

    p2 = proj[:, n1:n2]
    b_gate = p2[:, :CONV_WIDTH]
    u = p2[:, CONV_WIDTH:2 * CONV_WIDTH] * p2[:, 2 * CONV_WIDTH:]

    @pl.when(step % tiles_per_seq == 0)
    def _():
        carry_ref[...] = jnp.zeros_like(carry_ref)

    prev1 = carry_ref[ROW_GROUP - 1:ROW_GROUP, :]
    prev2 = carry_ref[ROW_GROUP - 2:ROW_GROUP - 1, :]
    row = lax.broadcasted_iota(I32, (tm, CONV_WIDTH), 0)
    u1 = jnp.where(row == 0, prev1, pltpu.roll(u, 1, 0))
    u2 = jnp.where(row == 0, prev2, jnp.where(row == 1, prev1, pltpu.roll(u, 2, 0)))
    carry_ref[...] = u[tm - ROW_GROUP:, :]
    cw = conv_w_ref[...]
    conv = b_gate * (cw[0:1, :] * u2 + cw[1:2, :] * u1 + cw[2:3, :] * u)
    branch_b = jnp.dot(conv.astype(BF16), wbb_ref[...], preferred_element_type=F32)

    gates = jax.nn.sigmoid(proj[:, n2:] + b_gate_ref[...])
    ga_ref[...] = gates[:, :D_MODEL].astype(BF16)
    mixb_ref[...] = (gates[:, D_MODEL:] * branch_b).astype(BF16)


def _in_proj(x2d, pos3d, invf, g_mix, w_all, b_gate, g_cq, wq, g_ckv, wkk, wkv,
             conv_w, wbb, seq):
    n = x2d.shape[0]
    row_spec = lambda c: pl.BlockSpec((TM, c), lambda i: (i, 0))
    consts = [invf, g_mix, w_all, b_gate, g_cq, wq, g_ckv, wkk, wkv, conv_w, wbb]
    kv_cols = MLA_HEADS * HEAD_SLOT
    return pl.pallas_call(
        functools.partial(_in_proj_kernel, tiles_per_seq=seq // TM),
        grid=(n // TM,),
        in_specs=[row_spec(D_MODEL), pl.BlockSpec((1, TM // LANES, LANES), lambda i: (i, 0, 0))]
        + [_const_spec(c.shape) for c in consts],
        out_specs=[row_spec(kv_cols), row_spec(kv_cols), row_spec(kv_cols),
                   row_spec(D_MODEL), row_spec(D_MODEL)],
        out_shape=[jax.ShapeDtypeStruct((n, kv_cols), BF16),
                   jax.ShapeDtypeStruct((n, kv_cols), BF16),
                   jax.ShapeDtypeStruct((n, kv_cols), BF16),
                   jax.ShapeDtypeStruct((n, D_MODEL), BF16),
                   jax.ShapeDtypeStruct((n, D_MODEL), BF16)],
        scratch_shapes=[pltpu.VMEM((ROW_GROUP, CONV_WIDTH), F32)],
        compiler_params=_params(("arbitrary",)),
        name="in_proj",
    )(x2d, pos3d, *consts)


def _attn_kernel(q_ref, k_ref, v_ref, o_ref):
    qi = pl.program_id(1)
    tq = q_ref.shape[0]

    def block(carry, heads, col_start, ncols, row_lo, visible):
        out = []
        for c, hd in enumerate(heads):
            m, acc = carry[c]
            hsl = slice(hd * HEAD_SLOT, (hd + 1) * HEAD_SLOT)
            kb = k_ref[pl.ds(col_start, ncols), hsl]
            vb = v_ref[pl.ds(col_start, ncols), hsl]
            s = lax.dot_general(q_ref[row_lo:, hsl], kb, (((1,), (1,)), ((), ())),
                                preferred_element_type=F32)
            if visible is not None:
                s = jnp.where(visible, s, NEG_BIG)
            m_hi, acc_hi = m[row_lo:], acc[row_lo:]
            m_new = jnp.maximum(m_hi, jnp.max(s, axis=1, keepdims=True))
            p = jnp.exp2(s - m_new).astype(BF16)
            acc_new = jnp.exp2(m_hi - m_new) * acc_hi + jnp.dot(p, vb, preferred_element_type=F32)
            if row_lo:
                m_new = jnp.concatenate([m[:row_lo], m_new], axis=0)
                acc_new = jnp.concatenate([acc[:row_lo], acc_new], axis=0)
            out.append((m_new, acc_new))
        return tuple(out)

    def full_step(j, carry, heads):
        return block(carry, heads, pl.multiple_of(j * TK, TK), TK, 0, None)

    lane = lax.broadcasted_iota(I32, (tq, HEAD_SLOT), 1)
    strip_masks = []
    for d in range(DIAG_STRIPS):
        shape = (tq - d * (tq // DIAG_STRIPS), TK // DIAG_STRIPS)
        strip_masks.append(lax.broadcasted_iota(I32, shape, 1) <= lax.broadcasted_iota(I32, shape, 0))
    for g in range(MLA_HEADS // ATTN_CHAINS):
        heads = tuple(range(g * ATTN_CHAINS, (g + 1) * ATTN_CHAINS))
        carry = tuple((jnp.full((tq, 1), NEG_BIG, F32), jnp.zeros((tq, HEAD_SLOT), F32))
                      for _ in heads)
        carry = lax.fori_loop(0, qi, functools.partial(full_step, heads=heads), carry)
        for d in range(DIAG_STRIPS):
            width, row_lo = TK // DIAG_STRIPS, d * (tq // DIAG_STRIPS)
            col = pl.multiple_of(qi * TK + d * width, width)
            carry = block(carry, heads, col, width, row_lo, strip_masks[d])
        outs = []
        for _, acc in carry:
            denom = jnp.sum(jnp.where(lane == L_LANE, acc, 0.0), axis=1, keepdims=True)
            outs.append(acc / denom)
        for c in range(0, ATTN_CHAINS, 2):
            hp = heads[c] // 2
            pair = jnp.where(lane < V_HEAD_DIM, outs[c], pltpu.roll(outs[c + 1], V_HEAD_DIM, 1))
            o_ref[:, hp * HEAD_SLOT:(hp + 1) * HEAD_SLOT] = pair.astype(BF16)


def _mla_attn(q, k, v, batch, seq):
    n = q.shape[0]
    nq = seq // TQ
    out_cols = MLA_HEADS * V_HEAD_DIM
    return pl.pallas_call(
        _attn_kernel,
        grid=(batch, nq),
        in_specs=[pl.BlockSpec((TQ, q.shape[1]), lambda b, i: (b * nq + i, 0)),
                  pl.BlockSpec((seq, k.shape[1]), lambda b, i: (b, 0)),
                  pl.BlockSpec((seq, v.shape[1]), lambda b, i: (b, 0))],
        out_specs=pl.BlockSpec((TQ, out_cols), lambda b, i: (b * nq + i, 0)),
        out_shape=jax.ShapeDtypeStruct((n, out_cols), BF16),
        compiler_params=_params(("arbitrary", "arbitrary")),
        name="mla_attn",
    )(q, k, v)


def _post_attn_kernel(x_ref, attn_ref, ga_ref, mixb_ref, wba_ref, wout_ref, g_x_ref, wq_ref,
                      mem_ref, g_mem_ref, wkv_ref, wo_ref, g_moe_ref, wr_ref, br_ref,
                      x2_ref, h3_ref, meta_ref, wts_ref, cnt_ref, run_ref, km_ref, vm_ref,
                      *, tiles_per_seq):
    step = pl.program_id(0)
    tm = x_ref.shape[0]
    xdim = XATTN_HEADS * XATTN_HEAD_DIM

    @pl.when(step == 0)
    def _():
        run_ref[...] = jnp.zeros_like(run_ref)

    @pl.when(step % tiles_per_seq == 0)
    def _():
        mem_n = _rms(mem_ref[...], g_mem_ref[...]).astype(BF16)
        kv = jnp.dot(mem_n, wkv_ref[...], preferred_element_type=F32)
        km_ref[...] = kv[:, :xdim].astype(BF16)
        vm_ref[...] = kv[:, xdim:].astype(BF16)

    def mixer_and_xattn(rows):
        a = jnp.dot(attn_ref[rows, :], wba_ref[...], preferred_element_type=F32)
        mixed = ga_ref[rows, :].astype(F32) * a + mixb_ref[rows, :].astype(F32)
        x1 = x_ref[rows, :] + jnp.dot(mixed.astype(BF16), wout_ref[...], preferred_element_type=F32)

        h2 = _rms(x1, g_x_ref[...]).astype(BF16)
        qm = jnp.dot(h2, wq_ref[...], preferred_element_type=F32) * (1.0 / math.sqrt(XATTN_HEAD_DIM))
        qm = qm.astype(BF16)
        heads = []
        for hd in range(XATTN_HEADS):
            sl = slice(hd * XATTN_HEAD_DIM, (hd + 1) * XATTN_HEAD_DIM)
            s = lax.dot_general(qm[:, sl], km_ref[:, sl], (((1,), (1,)), ((), ())),
                                preferred_element_type=F32)
            p = jnp.exp(s - jnp.max(s, axis=1, keepdims=True))
            o = jnp.dot(p.astype(BF16), vm_ref[:, sl], preferred_element_type=F32)
            heads.append((o / jnp.sum(p, axis=1, keepdims=True)).astype(BF16))
        o_all = jnp.concatenate(heads, axis=1)
        x2 = x1 + jnp.dot(o_all, wo_ref[...], preferred_element_type=F32)
        x2_ref[rows, :] = x2
        h3 = _rms(x2, g_moe_ref[...])
        g0, g1 = rows.start // ROW_GROUP, rows.stop // ROW_GROUP
        h3_ref[g0:g1] = h3.reshape(g1 - g0, ROW_GROUP, D_MODEL)
        return jnp.dot(h3.astype(BF16), wr_ref[...], preferred_element_type=F32) + br_ref[...]

    part = tm // POST_PARTS
    logits = jnp.concatenate([mixer_and_xattn(slice(c * part, (c + 1) * part))
                              for c in range(POST_PARTS)], axis=0)

    lane = lax.broadcasted_iota(I32, (tm, LANES), 1)
    lane_f = lane.astype(F32)

    def first_lane(hit):
        return jnp.min(jnp.where(hit, lane_f, float(LANES)), axis=1, keepdims=True)

    gl = jnp.where(lane < N_GROUPS, logits, -jnp.inf)
    gmax = jnp.max(gl, axis=1, keepdims=True)
    g_idx = first_lane(gl == gmax).astype(I32)
    p_group = 1.0 / jnp.sum(jnp.exp(gl - gmax), axis=1, keepdims=True)
    e_lo = ROUTE_LANE0 + g_idx * EXPERTS_PER_GROUP
    el = jnp.where((lane >= e_lo) & (lane < e_lo + EXPERTS_PER_GROUP), logits, -jnp.inf)
    v1 = jnp.max(el, axis=1, keepdims=True)
    i1 = first_lane(el == v1)
    el2 = jnp.where(lane_f == i1, -jnp.inf, el)
    v2 = jnp.max(el2, axis=1, keepdims=True)
    i2 = first_lane(el2 == v2)
    t = jnp.exp(v2 - v1)
    w0 = p_group / (1.0 + t)
    w1 = p_group * t / (1.0 + t)

    oh0 = lane_f == i1
    oh1 = lane_f == i2
    oh = (oh0 | oh1).astype(BF16)
    r_i = lax.broadcasted_iota(I32, (tm, tm), 0)
    c_i = lax.broadcasted_iota(I32, (tm, tm), 1)
    lower = (c_i < r_i).astype(BF16)
    before = jnp.dot(lower, oh, preferred_element_type=F32) + run_ref[...]
    rank0 = jnp.sum(jnp.where(oh0, before, 0.0), axis=1, keepdims=True)
    rank1 = jnp.sum(jnp.where(oh1, before, 0.0), axis=1, keepdims=True)
    run_ref[...] = run_ref[...] + jnp.sum(oh.astype(F32), axis=0, keepdims=True)
    cnt_ref[...] = jnp.broadcast_to(run_ref[...], cnt_ref.shape).astype(I32)

    meta = jnp.where(lane == 0, i1 - ROUTE_LANE0, 0.0)
    meta = jnp.where(lane == 1, i2 - ROUTE_LANE0, meta)
    meta = jnp.where(lane == 2, rank0, meta)
    meta = jnp.where(lane == 3, rank1, meta)
    meta_ref[...] = meta.T[:META_ROWS, :]
    wts_ref[...] = jnp.where(lane == 0, w0, jnp.where(lane == 1, w1, 0.0))


def _post_attn(x2d, attn, ga, mixb, wba, wout, g_x, wq, mem2d, g_mem, wkv, wo, g_moe, wr, br, seq):
    n = x2d.shape[0]
    tps = seq // TM
    xdim = XATTN_HEADS * XATTN_HEAD_DIM
    row_spec = lambda c: pl.BlockSpec((TM, c), lambda i: (i, 0))
    consts = lambda *arrs: [_const_spec(a.shape) for a in arrs]
    return pl.pallas_call(
        functools.partial(_post_attn_kernel, tiles_per_seq=tps),
        grid=(n // TM,),
        in_specs=([row_spec(D_MODEL), row_spec(MLA_HEADS * V_HEAD_DIM), row_spec(D_MODEL),
                   row_spec(D_MODEL)] + consts(wba, wout, g_x, wq)
                  + [pl.BlockSpec((MEM_LEN, D_MODEL), lambda i: (i // tps, 0))]
                  + consts(g_mem, wkv, wo, g_moe, wr, br)),
        out_specs=[row_spec(D_MODEL),
                   pl.BlockSpec((TM // ROW_GROUP, ROW_GROUP, D_MODEL), lambda i: (i, 0, 0)),
                   pl.BlockSpec((META_ROWS, TM), lambda i: (0, i)), row_spec(LANES),
                   _const_spec((ROW_GROUP, LANES))],
        out_shape=[jax.ShapeDtypeStruct((n, D_MODEL), F32),
                   jax.ShapeDtypeStruct((n // ROW_GROUP, ROW_GROUP, D_MODEL), F32),
                   jax.ShapeDtypeStruct((META_ROWS, n), F32),
                   jax.ShapeDtypeStruct((n, LANES), F32),
                   jax.ShapeDtypeStruct((ROW_GROUP, LANES), I32)],
        scratch_shapes=[pltpu.VMEM((1, LANES), F32), pltpu.VMEM((MEM_LEN, xdim), BF16),
                        pltpu.VMEM((MEM_LEN, xdim), BF16)],
        compiler_params=_params(("arbitrary",)),
        name="post_attn",
    )(x2d, attn, ga, mixb, wba, wout, g_x, wq, mem2d, g_mem, wkv, wo, g_moe, wr, br)


def _tile_row(ref, group, k):
    return ref.at[group, pl.ds(k, 1), :]


def _hbm_row(ref, row):
    return ref.at[pl.ds(row, 1), :]


def _dispatch_kernel(fill_ref, dest0_ref, dest1_ref, zeros_ref, h3_ref, xd_ref, row_sem, fill_sem):
    i = pl.program_id(0)

    def fill_copy(b):
        return pltpu.make_async_copy(
            zeros_ref, xd_ref.at[pl.ds(pl.multiple_of(b * SLOT_BLOCK, SLOT_BLOCK), SLOT_BLOCK)],
            fill_sem)

    @pl.when(i == 0)
    def _():
        def start(b, c):
            @pl.when(fill_ref[b] > 0)
            def _():
                fill_copy(b).start()
            return c

        def wait(b, c):
            @pl.when(fill_ref[b] > 0)
            def _():
                fill_copy(b).wait()
            return c

        lax.fori_loop(0, fill_ref.shape[0], start, 0)
        lax.fori_loop(0, fill_ref.shape[0], wait, 0)

    def copies(g, k):
        r = g * ROW_GROUP + k
        src = _tile_row(h3_ref, g, k)
        return (pltpu.make_async_copy(src, _hbm_row(xd_ref, dest0_ref[0, 0, r]), row_sem),
                pltpu.make_async_copy(src, _hbm_row(xd_ref, dest1_ref[0, 0, r]), row_sem))

    def issue(g, c):
        for k in range(ROW_GROUP):
            for j, cp in enumerate(copies(g, k)):
                cp.start(priority=j % DMA_THREADS)
        return c

    def drain(g, c):
        for k in range(ROW_GROUP):
            for cp in copies(g, k):
                cp.wait()
        return c

    lax.fori_loop(0, TROW // ROW_GROUP, issue, 0)
    lax.fori_loop(0, TROW // ROW_GROUP, drain, 0)


def _dispatch(fill_blk, dest0, dest1, h3, n_slots):
    groups, _, d = h3.shape
    n = groups * ROW_GROUP
    grid_spec = pltpu.PrefetchScalarGridSpec(
        num_scalar_prefetch=1,
        grid=(n // TROW,),
        in_specs=[pl.BlockSpec((1, 1, TROW), lambda i, fb: (i, 0, 0), memory_space=pltpu.SMEM),
                  pl.BlockSpec((1, 1, TROW), lambda i, fb: (i, 0, 0), memory_space=pltpu.SMEM),
                  pl.BlockSpec((SLOT_BLOCK, d), lambda i, fb: (0, 0)),
                  pl.BlockSpec((TROW // ROW_GROUP, ROW_GROUP, d), lambda i, fb: (i, 0, 0))],
        out_specs=pl.BlockSpec(memory_space=pl.ANY),
        scratch_shapes=[pltpu.SemaphoreType.DMA, pltpu.SemaphoreType.DMA],
    )
    return pl.pallas_call(
        _dispatch_kernel,
        grid_spec=grid_spec,
        out_shape=jax.ShapeDtypeStruct((n_slots, d), F32),
        compiler_params=_params(("arbitrary",)),
        name="dispatch",
    )(fill_blk, dest0, dest1, jnp.zeros((SLOT_BLOCK, d), F32), h3)


def _experts_kernel(be_ref, nb_ref, nxt_ref, xd_ref, wgu_hbm, wdn_hbm, yd_ref,
                    wgu_f32, wdn_f32, wgu_bf, wdn_bf, slot_ref, sem):
    i = pl.program_id(0)

    def weight_copies(e, slot):
        return (pltpu.make_async_copy(wgu_hbm.at[e], wgu_f32.at[slot], sem.at[slot]),
                pltpu.make_async_copy(wdn_hbm.at[e], wdn_f32.at[slot], sem.at[slot]))

    @pl.when(i == 0)
    def _():
        slot_ref[0] = 0
        for cp in weight_copies(be_ref[0], 0):
            cp.start()

    @pl.when(i < nb_ref[0])
    def _():
        @pl.when((i == 0) | (be_ref[i] != be_ref[jnp.maximum(i - 1, 0)]))
        def _():
            slot = slot_ref[0]
            for cp in weight_copies(be_ref[i], slot):
                cp.wait()
            wgu_bf[...] = wgu_f32[slot].astype(BF16)
            wdn_bf[...] = wdn_f32[slot].astype(BF16)

            @pl.when(nxt_ref[i] >= 0)
            def _():
                for cp in weight_copies(nxt_ref[i], 1 - slot):
                    cp.start()

            slot_ref[0] = 1 - slot

        gu = jnp.dot(xd_ref[...].astype(BF16), wgu_bf[...], preferred_element_type=F32)
        gate = gu[:, :EXPERT_FF]
        up = gu[:, EXPERT_FF:]
        act = (gate * jax.nn.sigmoid(gate) * up).astype(BF16)
        yd_ref[...] = jnp.dot(act, wdn_bf[...], preferred_element_type=F32)


def _experts(block_expert, n_blocks_used, next_expert, xd, w_gate_up, w_down):
    n_slots, d = xd.shape
    slot_map = lambda i, be, nb, nx: (jnp.minimum(i, nb[0] - 1), 0)
    grid_spec = pltpu.PrefetchScalarGridSpec(
        num_scalar_prefetch=3,
        grid=(n_slots // SLOT_BLOCK,),
        in_specs=[pl.BlockSpec((SLOT_BLOCK, d), slot_map),
                  pl.BlockSpec(memory_space=pl.ANY),
                  pl.BlockSpec(memory_space=pl.ANY)],
        out_specs=pl.BlockSpec((SLOT_BLOCK, d), slot_map),
        scratch_shapes=[pltpu.VMEM((2, D_MODEL, 2 * EXPERT_FF), F32),
                        pltpu.VMEM((2, EXPERT_FF, D_MODEL), F32),
                        pltpu.VMEM((D_MODEL, 2 * EXPERT_FF), BF16),
                        pltpu.VMEM((EXPERT_FF, D_MODEL), BF16),
                        pltpu.SMEM((1,), I32),
                        pltpu.SemaphoreType.DMA((2,))],
    )
    return pl.pallas_call(
        _experts_kernel,
        grid_spec=grid_spec,
        out_shape=jax.ShapeDtypeStruct(xd.shape, F32),
        input_output_aliases={3: 0},
        compiler_params=_params(("arbitrary",)),
        name="experts",
    )(block_expert, n_blocks_used, next_expert, xd, w_gate_up, w_down)


def _combine_kernel(dest0_ref, dest1_ref, next0_ref, next1_ref, x2_ref, wts_ref, g_ref, yd_ref, o_ref,
                    buf, sem):
    i = pl.program_id(0)
    last = pl.num_programs(0) - 1
    rows = x2_ref.shape[0]
    slot = i % 2

    def copies(d_refs, slt, g, k):
        r = g * ROW_GROUP + k
        return tuple(pltpu.make_async_copy(_hbm_row(yd_ref, d_ref[0, 0, r]),
                                           _tile_row(buf.at[slt, j], g, k), sem.at[slt])
                     for j, d_ref in enumerate(d_refs))

    def issue(d_ref, slt):
        def body(g, c):
            for k in range(ROW_GROUP):
                for j, cp in enumerate(copies(d_ref, slt, g, k)):
                    cp.start(priority=j % DMA_THREADS)
            return c
        lax.fori_loop(0, rows // ROW_GROUP, body, 0)

    @pl.when(i == 0)
    def _():
        issue((dest0_ref, dest1_ref), 0)

    @pl.when(i < last)
    def _():
        issue((next0_ref, next1_ref), 1 - slot)

    def drain(g, c):
        for k in range(ROW_GROUP):
            for cp in copies((dest0_ref, dest1_ref), slot, g, k):
                cp.wait()
        return c

    lax.fori_loop(0, rows // ROW_GROUP, drain, 0)

    lane = lax.broadcasted_iota(I32, (rows, LANES), 1)
    wts = wts_ref[...]
    w0 = jnp.sum(jnp.where(lane == 0, wts, 0.0), axis=1, keepdims=True)
    w1 = jnp.sum(jnp.where(lane == 1, wts, 0.0), axis=1, keepdims=True)
    y0 = buf[slot, 0].reshape(rows, D_MODEL)
    y1 = buf[slot, 1].reshape(rows, D_MODEL)
    o_ref[...] = _rms(x2_ref[...] + y0 * w0 + y1 * w1, g_ref[...])


def _combine(dest0, dest1, x2, wts, g_final, yd):
    n = x2.shape[0]
    steps = n // TROW
    dest_spec = lambda f: pl.BlockSpec((1, 1, TROW), f, memory_space=pltpu.SMEM)
    cur = lambda i: (i, 0, 0)
    nxt = lambda i: (jnp.minimum(i + 1, steps - 1), 0, 0)
    return pl.pallas_call(
        _combine_kernel,
        grid=(steps,),
        in_specs=[dest_spec(cur), dest_spec(cur), dest_spec(nxt), dest_spec(nxt),
                  pl.BlockSpec((TROW, D_MODEL), lambda i: (i, 0)),
                  pl.BlockSpec((TROW, LANES), lambda i: (i, 0)),
                  _const_spec((1, D_MODEL)),
                  pl.BlockSpec(memory_space=pl.ANY)],
        out_specs=pl.BlockSpec((TROW, D_MODEL), lambda i: (i, 0)),
        out_shape=jax.ShapeDtypeStruct((n, D_MODEL), F32),
        scratch_shapes=[pltpu.VMEM((2, 2, TROW // ROW_GROUP, ROW_GROUP, D_MODEL), F32),
                        pltpu.SemaphoreType.DMA((2,))],
        compiler_params=_params(("arbitrary",)),
        name="combine",
    )(dest0, dest1, dest0, dest1, x2, wts, g_final, yd)


def _lanes(parts, width):
    ref = next(p for p in parts if not isinstance(p, int))
    cols = [jnp.zeros(ref.shape[:-1] + (p,), ref.dtype) if isinstance(p, int) else p for p in parts]
    used = sum(c.shape[-1] for c in cols)
    assert used <= width
    if used < width:
        cols.append(jnp.zeros(ref.shape[:-1] + (width - used,), ref.dtype))
    return jnp.concatenate(cols, axis=-1)


def _head_slots(w, per_head, pieces):
    rows = w.shape[0]
    wh = w.reshape(rows, MLA_HEADS, per_head)
    return _lanes(pieces(wh), HEAD_SLOT).reshape(rows, MLA_HEADS * HEAD_SLOT)


def _layer(x2d, pos3d, invf, mem2d, g_mem, w_kv_mem, batch, seq, g_mix, w_in, b_gate, g_cq, w_uq, g_ckv, w_ukv,
           conv_w, w_branch_a, w_branch_b, w_out, g_xattn, w_q_mem, w_o_mem, g_moe,
           w_router_group, b_router_group, w_router_expert, b_router_expert, w_gate_up, w_down,
           g_final):
    n = x2d.shape[0]
    row = lambda v: v.reshape(1, -1)

    w_kr = w_in[:, COL_CKV:COL_KROPE]
    w_all = jnp.concatenate([_lanes([w_in[:, :COL_CKV], ROPE_LO, w_kr], COL_CKV + HEAD_SLOT),
                             w_in[:, COL_KROPE:]], axis=1).astype(BF16)
    qd = QK_NOPE_DIM + QK_ROPE_DIM
    wq = _head_slots(w_uq, qd, lambda wh: [wh]).astype(BF16)
    kvd = QK_NOPE_DIM + V_HEAD_DIM
    wkk = _head_slots(w_ukv, kvd, lambda wh: [wh[..., :QK_NOPE_DIM]]).astype(BF16)
    wkv = _head_slots(w_ukv, kvd, lambda wh: [wh[..., QK_NOPE_DIM:]]).astype(BF16)
    conv_w8 = jnp.concatenate(
        [conv_w, jnp.zeros((ROW_GROUP - conv_w.shape[0], CONV_WIDTH), F32)], axis=0)

    q, k, v, ga, mixb = _in_proj(x2d, pos3d, invf, row(g_mix), w_all, row(b_gate), row(g_cq),
                                 wq, row(g_ckv), wkk, wkv, conv_w8,
                                 w_branch_b.astype(BF16), seq)
    attn = _mla_attn(q, k, v, batch, seq)

    wr = _lanes([w_router_group, w_router_expert], LANES).astype(BF16)
    br = _lanes([row(b_router_group), row(b_router_expert)], LANES)
    x2, h3, meta, wts, cnt = _post_attn(x2d, attn, ga, mixb, w_branch_a.astype(BF16),
                                   w_out.astype(BF16), row(g_xattn), w_q_mem.astype(BF16), mem2d, row(g_mem),
                                   w_kv_mem.astype(BF16),
                                   w_o_mem.astype(BF16), row(g_moe), wr, br, seq)

    counts = cnt[0, ROUTE_LANE0:ROUTE_LANE0 + N_EXPERTS]
    padded = (counts + SLOT_BLOCK - 1) // SLOT_BLOCK * SLOT_BLOCK
    pad_end = jnp.cumsum(padded)
    pad_start = pad_end - padded
    n_slots = (2 * n + SLOT_BLOCK - 1) // SLOT_BLOCK * SLOT_BLOCK + N_EXPERTS * SLOT_BLOCK
    routed = meta[0:4].astype(I32)
    expert_col = jnp.arange(N_EXPERTS, dtype=I32)[:, None, None]
    dest = jnp.sum(jnp.where(routed[None, 0:2] == expert_col, pad_start[:, None, None], 0),
                   axis=0) + routed[2:4]
    dest0 = dest[0].reshape(n // TROW, 1, TROW)
    dest1 = dest[1].reshape(n // TROW, 1, TROW)
    block_lo = jnp.arange(n_slots // SLOT_BLOCK, dtype=I32) * SLOT_BLOCK
    block_expert = jnp.minimum(jnp.sum(pad_end[None, :] <= block_lo[:, None], axis=1),
                               N_EXPERTS - 1).astype(I32)
    n_blocks_used = (pad_end[-1:] // SLOT_BLOCK).astype(I32)
    fill_blk = ((block_lo + SLOT_BLOCK == pad_end[block_expert])
                | (block_lo >= pad_end[-1])).astype(I32)

    cand = jnp.where(padded > 0, jnp.arange(N_EXPERTS, dtype=I32), N_EXPERTS)
    later = jnp.concatenate([lax.cummin(cand, reverse=True)[1:],
                             jnp.full((1,), N_EXPERTS, I32)])
    next_expert = jnp.where(later < N_EXPERTS, later, -1)[block_expert].astype(I32)

    xd = _dispatch(fill_blk, dest0, dest1, h3, n_slots)
    yd = _experts(block_expert, n_blocks_used, next_expert, xd, w_gate_up, w_down)
    return _combine(dest0, dest1, x2, wts, row(g_final), yd)


def kernel(x, mem, positions, g_mix, w_in, b_gate, g_cq, w_uq, g_ckv, w_ukv, conv_w, w_branch_a,
           w_branch_b, w_out, g_xattn, g_mem, w_q_mem, w_kv_mem, w_o_mem, g_moe, w_router_group,
           b_router_group, w_router_expert, b_router_expert, w_gate_up, w_down, g_final):
    batch, seq, d = x.shape
    depth = g_mix.shape[0]
    assert d == D_MODEL and depth == 1 and seq % TM == 0 and seq % TQ == 0 and TQ == TK
    n = batch * seq
    x2d = x.reshape(n, d)
    pos3d = positions.astype(I32).reshape(n // TM, TM // LANES, LANES)
    inv_freq = ROPE_THETA ** (-jnp.arange(0, QK_ROPE_DIM, 2, dtype=F32) / QK_ROPE_DIM)
    invf = _lanes([ROPE_LO, inv_freq.reshape(1, -1), inv_freq.reshape(1, -1)], HEAD_SLOT)
    l = 0
    out = _layer(x2d, pos3d, invf, mem.reshape(batch * MEM_LEN, d), g_mem[l], w_kv_mem[l], batch,
                 seq, g_mix[l], w_in[l], b_gate[l], g_cq[l],
                 w_uq[l], g_ckv[l], w_ukv[l], conv_w[l], w_branch_a[l], w_branch_b[l], w_out[l],
                 g_xattn[l], w_q_mem[l], w_o_mem[l], g_moe[l], w_router_group[l],
                 b_router_group[l], w_router_expert[l], b_router_expert[l], w_gate_up[l],
                 w_down[l], g_final)
    return out.reshape(batch, seq, d)
```

```python
import functools
import math

import jax
import jax.numpy as jnp
from jax import lax
from jax.experimental import pallas as pl
from jax.experimental.pallas import tpu as pltpu

F32 = jnp.float32
BF16 = jnp.bfloat16
I32 = jnp.int32

D_MODEL = 1024
MEM_LEN = 256
MLA_HEADS = 8
QK_NOPE_DIM = 64
QK_ROPE_DIM = 32
V_HEAD_DIM = 64
Q_LORA_RANK = 256
KV_LORA_RANK = 128
ROPE_THETA = 10000.0
CONV_WIDTH = 512
XATTN_HEADS = 4
XATTN_HEAD_DIM = 128
N_GROUPS = 8
EXPERTS_PER_GROUP = 8
N_EXPERTS = N_GROUPS * EXPERTS_PER_GROUP
EXPERT_FF = 256
EPS = 1e-6

COL_CQ = Q_LORA_RANK
COL_CKV = COL_CQ + KV_LORA_RANK
COL_KROPE = COL_CKV + QK_ROPE_DIM
COL_CONV = COL_KROPE + 3 * CONV_WIDTH

LANES = 128
HEAD_SLOT = LANES
ROPE_LO = QK_NOPE_DIM
ROPE_HALF = QK_ROPE_DIM // 2
L_LANE = V_HEAD_DIM
SLOT_BLOCK = 256
ROUTE_LANE0 = N_GROUPS
NEG_BIG = -1e30

TM = 1024
TQ = 512
TK = 512
META_ROWS = 8
POST_PARTS = 1
DIAG_STRIPS = 2
ATTN_CHAINS = 8
TROW = 256
TROW_D = 512
DMA_THREADS = 2
ROW_GROUP = 8
VMEM_LIMIT = 56 * 1024 * 1024


def _rms(x, g):
    return x * lax.rsqrt(jnp.mean(x * x, axis=-1, keepdims=True) + EPS) * g


def _const_spec(shape):
    nd = len(shape)
    return pl.BlockSpec(shape, lambda *_: (0,) * nd)


def _params(sem):
    return pltpu.CompilerParams(dimension_semantics=sem, vmem_limit_bytes=VMEM_LIMIT)


def _in_proj_kernel(x_ref, pos_ref, invf_ref, g_mix_ref, w_ref, b_gate_ref,
                    g_cq_ref, wq_ref, g_ckv_ref, wkk_ref, wkv_ref, conv_w_ref, wbb_ref,
                    q_ref, k_ref, v_ref, ga_ref, mixb_ref, carry_ref, *, tiles_per_seq):
    step = pl.program_id(0)
    tm = x_ref.shape[0]
    h = _rms(x_ref[...], g_mix_ref[...]).astype(BF16)

    lane = lax.broadcasted_iota(I32, (tm, HEAD_SLOT), 1)
    rope_lane = (lane >= ROPE_LO) & (lane < ROPE_LO + QK_ROPE_DIM)
    pos_rows = pos_ref[0].astype(F32)
    pos_tile = jnp.concatenate([jnp.broadcast_to(pos_rows[g:g + 1, :], (LANES, LANES))
                                for g in range(tm // LANES)], axis=0)
    row_lane = lax.broadcasted_iota(I32, (tm, LANES), 0) & (LANES - 1)
    pos_col = jnp.sum(jnp.where(lane == row_lane, pos_tile, 0.0), axis=1, keepdims=True)
    ang = pos_col * invf_ref[...]
    first_half = lane < ROPE_LO + ROPE_HALF
    cos_t = jnp.where(rope_lane, jnp.cos(ang), 0.0)
    sin_t = jnp.where(rope_lane, jnp.sin(ang), 0.0)
    sin_t = jnp.where(first_half, -sin_t, sin_t)

    def swap_halves(t):
        return jnp.where(first_half, pltpu.roll(t, HEAD_SLOT - ROPE_HALF, 1),
                         pltpu.roll(t, ROPE_HALF, 1))

    proj = jnp.dot(h, w_ref[...], preferred_element_type=F32)
    n1, n2 = COL_CKV + HEAD_SLOT, COL_CKV + HEAD_SLOT + 3 * CONV_WIDTH
    p1 = proj[:, :n1]
    c_q = p1[:, :COL_CQ]
    c_kv = p1[:, COL_CQ:COL_CKV]
    kr = p1[:, COL_CKV:COL_CKV + HEAD_SLOT]
    k_rot = kr * cos_t + swap_halves(kr) * sin_t

    scale = math.log2(math.e) / math.sqrt(QK_NOPE_DIM + QK_ROPE_DIM)
    q_cos = jnp.where(lane < ROPE_LO, scale, cos_t * scale)
    q_sin = sin_t * scale
    cqn = _rms(c_q, g_cq_ref[...]).astype(BF16)
    qq = jnp.dot(cqn, wq_ref[...], preferred_element_type=F32)
    ckvn = _rms(c_kv, g_ckv_ref[...]).astype(BF16)
    kk = jnp.dot(ckvn, wkk_ref[...], preferred_element_type=F32)
    vv = jnp.dot(ckvn, wkv_ref[...], preferred_element_type=F32)
    ones_lane = jnp.where(lane == L_LANE, 1.0, 0.0)
    for hd in range(MLA_HEADS):
        sl = slice(hd * HEAD_SLOT, (hd + 1) * HEAD_SLOT)
        q_ref[:, sl] = (qq[:, sl] * q_cos + swap_halves(qq[:, sl]) * q_sin).astype(BF16)
        k_ref[:, sl] = (kk[:, sl] + k_rot).astype(BF16)
        v_ref[:, sl] = (vv[:, sl] + ones_lane).astype(BF16)

    p2 = proj[:, n1:n2]
    b_gate = p2[:, :CONV_WIDTH]
    u = p2[:, CONV_WIDTH:2 * CONV_WIDTH] * p2[:, 2 * CONV_WIDTH:]

    @pl.when(step % tiles_per_seq == 0)
    def _():
        carry_ref[...] = jnp.zeros_like(carry_ref)

    prev1 = carry_ref[ROW_GROUP - 1:ROW_GROUP, :]
    prev2 = carry_ref[ROW_GROUP - 2:ROW_GROUP - 1, :]
    row = lax.broadcasted_iota(I32, (tm, CONV_WIDTH), 0)
    u1 = jnp.where(row == 0, prev1, pltpu.roll(u, 1, 0))
    u2 = jnp.where(row == 0, prev2, jnp.where(row == 1, prev1, pltpu.roll(u, 2, 0)))
    carry_ref[...] = u[tm - ROW_GROUP:, :]
    cw = conv_w_ref[...]
    conv = b_gate * (cw[0:1, :] * u2 + cw[1:2, :] * u1 + cw[2:3, :] * u)
    branch_b = jnp.dot(conv.astype(BF16), wbb_ref[...], preferred_element_type=F32)

    gates = jax.nn.sigmoid(proj[:, n2:] + b_gate_ref[...])
    ga_ref[...] = gates[:, :D_MODEL].astype(BF16)
    mixb_ref[...] = (gates[:, D_MODEL:] * branch_b).astype(BF16)


def _in_proj(x2d, pos3d, invf, g_mix, w_all, b_gate, g_cq, wq, g_ckv, wkk, wkv,
             conv_w, wbb, seq):
    n = x2d.shape[0]
    row_spec = lambda c: pl.BlockSpec((TM, c), lambda i: (i, 0))
    consts = [invf, g_mix, w_all, b_gate, g_cq, wq, g_ckv, wkk, wkv, conv_w, wbb]
    kv_cols = MLA_HEADS * HEAD_SLOT
    return pl.pallas_call(
        functools.partial(_in_proj_kernel, tiles_per_seq=seq // TM),
        grid=(n // TM,),
        in_specs=[row_spec(D_MODEL), pl.BlockSpec((1, TM // LANES, LANES), lambda i: (i, 0, 0))]
        + [_const_spec(c.shape) for c in consts],
        out_specs=[row_spec(kv_cols), row_spec(kv_cols), row_spec(kv_cols),
                   row_spec(D_MODEL), row_spec(D_MODEL)],
        out_shape=[jax.ShapeDtypeStruct((n, kv_cols), BF16),
                   jax.ShapeDtypeStruct((n, kv_cols), BF16),
                   jax.ShapeDtypeStruct((n, kv_cols), BF16),
                   jax.ShapeDtypeStruct((n, D_MODEL), BF16),
                   jax.ShapeDtypeStruct((n, D_MODEL), BF16)],
        scratch_shapes=[pltpu.VMEM((ROW_GROUP, CONV_WIDTH), F32)],
        compiler_params=_params(("arbitrary",)),
        name="in_proj",
    )(x2d, pos3d, *consts)


def _attn_kernel(q_ref, k_ref, v_ref, o_ref):
    qi = pl.program_id(1)
    tq = q_ref.shape[0]

    def block(carry, heads, col_start, ncols, row_lo, visible):
        out = []
        for c, hd in enumerate(heads):
            m, acc = carry[c]
            hsl = slice(hd * HEAD_SLOT, (hd + 1) * HEAD_SLOT)
            kb = k_ref[pl.ds(col_start, ncols), hsl]
            vb = v_ref[pl.ds(col_start, ncols), hsl]
            s = lax.dot_general(q_ref[row_lo:, hsl], kb, (((1,), (1,)), ((), ())),
                                preferred_element_type=F32)
            if visible is not None:
                s = jnp.where(visible, s, NEG_BIG)
            m_hi, acc_hi = m[row_lo:], acc[row_lo:]
            m_new = jnp.maximum(m_hi, jnp.max(s, axis=1, keepdims=True))
            p = jnp.exp2(s - m_new).astype(BF16)
            acc_new = jnp.exp2(m_hi - m_new) * acc_hi + jnp.dot(p, vb, preferred_element_type=F32)
            if row_lo:
                m_new = jnp.concatenate([m[:row_lo], m_new], axis=0)
                acc_new = jnp.concatenate([acc[:row_lo], acc_new], axis=0)
            out.append((m_new, acc_new))
        return tuple(out)

    def full_step(j, carry, heads):
        return block(carry, heads, pl.multiple_of(j * TK, TK), TK, 0, None)

    lane = lax.broadcasted_iota(I32, (tq, HEAD_SLOT), 1)
    strip_masks = []
    for d in range(DIAG_STRIPS):
        shape = (tq - d * (tq // DIAG_STRIPS), TK // DIAG_STRIPS)
        strip_masks.append(lax.broadcasted_iota(I32, shape, 1) <= lax.broadcasted_iota(I32, shape, 0))
    for g in range(MLA_HEADS // ATTN_CHAINS):
        heads = tuple(range(g * ATTN_CHAINS, (g + 1) * ATTN_CHAINS))
        carry = tuple((jnp.full((tq, 1), NEG_BIG, F32), jnp.zeros((tq, HEAD_SLOT), F32))
                      for _ in heads)
        carry = lax.fori_loop(0, qi, functools.partial(full_step, heads=heads), carry)
        for d in range(DIAG_STRIPS):
            width, row_lo = TK // DIAG_STRIPS, d * (tq // DIAG_STRIPS)
            col = pl.multiple_of(qi * TK + d * width, width)
            carry = block(carry, heads, col, width, row_lo, strip_masks[d])
        outs = []
        for _, acc in carry:
            denom = jnp.sum(jnp.where(lane == L_LANE, acc, 0.0), axis=1, keepdims=True)
            outs.append(acc / denom)
        for c in range(0, ATTN_CHAINS, 2):
            hp = heads[c] // 2
            pair = jnp.where(lane < V_HEAD_DIM, outs[c], pltpu.roll(outs[c + 1], V_HEAD_DIM, 1))
            o_ref[:, hp * HEAD_SLOT:(hp + 1) * HEAD_SLOT] = pair.astype(BF16)


def _mla_attn(q, k, v, batch, seq):
    n = q.shape[0]
    nq = seq // TQ
    out_cols = MLA_HEADS * V_HEAD_DIM
    return pl.pallas_call(
        _attn_kernel,
        grid=(batch, nq),
        in_specs=[pl.BlockSpec((TQ, q.shape[1]), lambda b, i: (b * nq + i, 0)),
                  pl.BlockSpec((seq, k.shape[1]), lambda b, i: (b, 0)),
                  pl.BlockSpec((seq, v.shape[1]), lambda b, i: (b, 0))],
        out_specs=pl.BlockSpec((TQ, out_cols), lambda b, i: (b * nq + i, 0)),
        out_shape=jax.ShapeDtypeStruct((n, out_cols), BF16),
        compiler_params=_params(("arbitrary", "arbitrary")),
        name="mla_attn",
    )(q, k, v)


def _post_attn_kernel(x_ref, attn_ref, ga_ref, mixb_ref, wba_ref, wout_ref, g_x_ref, wq_ref,
                      mem_ref, g_mem_ref, wkv_ref, wo_ref, g_moe_ref, wr_ref, br_ref,
                      x2_ref, h3_ref, meta_ref, wts_ref, cnt_ref, run_ref, km_ref, vm_ref,
                      *, tiles_per_seq):
    step = pl.program_id(0)
    tm = x_ref.shape[0]
    xdim = XATTN_HEADS * XATTN_HEAD_DIM

    @pl.when(step == 0)
    def _():
        run_ref[...] = jnp.zeros_like(run_ref)

    @pl.when(step % tiles_per_seq == 0)
    def _():
        mem_n = _rms(mem_ref[...], g_mem_ref[...]).astype(BF16)
        kv = jnp.dot(mem_n, wkv_ref[...], preferred_element_type=F32)
        km_ref[...] = kv[:, :xdim].astype(BF16)
        vm_ref[...] = kv[:, xdim:].astype(BF16)

    def mixer_and_xattn(rows):
        a = jnp.dot(attn_ref[rows, :], wba_ref[...], preferred_element_type=F32)
        mixed = ga_ref[rows, :].astype(F32) * a + mixb_ref[rows, :].astype(F32)
        x1 = x_ref[rows, :] + jnp.dot(mixed.astype(BF16), wout_ref[...], preferred_element_type=F32)

        h2 = _rms(x1, g_x_ref[...]).astype(BF16)
        qm = jnp.dot(h2, wq_ref[...], preferred_element_type=F32) * (1.0 / math.sqrt(XATTN_HEAD_DIM))
        qm = qm.astype(BF16)
        heads = []
        for hd in range(XATTN_HEADS):
            sl = slice(hd * XATTN_HEAD_DIM, (hd + 1) * XATTN_HEAD_DIM)
            s = lax.dot_general(qm[:, sl], km_ref[:, sl], (((1,), (1,)), ((), ())),
                                preferred_element_type=F32)
            p = jnp.exp(s - jnp.max(s, axis=1, keepdims=True))
            o = jnp.dot(p.astype(BF16), vm_ref[:, sl], preferred_element_type=F32)
            heads.append((o / jnp.sum(p, axis=1, keepdims=True)).astype(BF16))
        o_all = jnp.concatenate(heads, axis=1)
        x2 = x1 + jnp.dot(o_all, wo_ref[...], preferred_element_type=F32)
        x2_ref[rows, :] = x2
        h3 = _rms(x2, g_moe_ref[...])
        g0, g1 = rows.start // ROW_GROUP, rows.stop // ROW_GROUP
        h3_ref[g0:g1] = h3.reshape(g1 - g0, ROW_GROUP, D_MODEL)
        return jnp.dot(h3.astype(BF16), wr_ref[...], preferred_element_type=F32) + br_ref[...]

    part = tm // POST_PARTS
    logits = jnp.concatenate([mixer_and_xattn(slice(c * part, (c + 1) * part))
                              for c in range(POST_PARTS)], axis=0)

    lane = lax.broadcasted_iota(I32, (tm, LANES), 1)
    lane_f = lane.astype(F32)

    def first_lane(hit):
        return jnp.min(jnp.where(hit, lane_f, float(LANES)), axis=1, keepdims=True)

    gl = jnp.where(lane < N_GROUPS, logits, -jnp.inf)
    gmax = jnp.max(gl, axis=1, keepdims=True)
    g_idx = first_lane(gl == gmax).astype(I32)
    p_group = 1.0 / jnp.sum(jnp.exp(gl - gmax), axis=1, keepdims=True)
    e_lo = ROUTE_LANE0 + g_idx * EXPERTS_PER_GROUP
    el = jnp.where((lane >= e_lo) & (lane < e_lo + EXPERTS_PER_GROUP), logits, -jnp.inf)
    v1 = jnp.max(el, axis=1, keepdims=True)
    i1 = first_lane(el == v1)
    el2 = jnp.where(lane_f == i1, -jnp.inf, el)
    v2 = jnp.max(el2, axis=1, keepdims=True)
    i2 = first_lane(el2 == v2)
    t = jnp.exp(v2 - v1)
    w0 = p_group / (1.0 + t)
    w1 = p_group * t / (1.0 + t)

    oh0 = lane_f == i1
    oh1 = lane_f == i2
    oh = (oh0 | oh1).astype(BF16)
    r_i = lax.broadcasted_iota(I32, (tm, tm), 0)
    c_i = lax.broadcasted_iota(I32, (tm, tm), 1)
    lower = (c_i < r_i).astype(BF16)
    before = jnp.dot(lower, oh, preferred_element_type=F32) + run_ref[...]
    rank0 = jnp.sum(jnp.where(oh0, before, 0.0), axis=1, keepdims=True)
    rank1 = jnp.sum(jnp.where(oh1, before, 0.0), axis=1, keepdims=True)
    run_ref[...] = run_ref[...] + jnp.sum(oh.astype(F32), axis=0, keepdims=True)
    cnt_ref[...] = jnp.broadcast_to(run_ref[...], cnt_ref.shape).astype(I32)

    meta = jnp.where(lane == 0, i1 - ROUTE_LANE0, 0.0)
    meta = jnp.where(lane == 1, i2 - ROUTE_LANE0, meta)
    meta = jnp.where(lane == 2, rank0, meta)
    meta = jnp.where(lane == 3, rank1, meta)
    meta_ref[...] = meta.T[:META_ROWS, :]
    wts_ref[...] = jnp.where(lane == 0, w0, jnp.where(lane == 1, w1, 0.0))


def _post_attn(x2d, attn, ga, mixb, wba, wout, g_x, wq, mem2d, g_mem, wkv, wo, g_moe, wr, br, seq):
    n = x2d.shape[0]
    tps = seq // TM
    xdim = XATTN_HEADS * XATTN_HEAD_DIM
    row_spec = lambda c: pl.BlockSpec((TM, c), lambda i: (i, 0))
    consts = lambda *arrs: [_const_spec(a.shape) for a in arrs]
    return pl.pallas_call(
        functools.partial(_post_attn_kernel, tiles_per_seq=tps),
        grid=(n // TM,),
        in_specs=([row_spec(D_MODEL), row_spec(MLA_HEADS * V_HEAD_DIM), row_spec(D_MODEL),
                   row_spec(D_MODEL)] + consts(wba, wout, g_x, wq)
                  + [pl.BlockSpec((MEM_LEN, D_MODEL), lambda i: (i // tps, 0))]
                  + consts(g_mem, wkv, wo, g_moe, wr, br)),
        out_specs=[row_spec(D_MODEL),
                   pl.BlockSpec((TM // ROW_GROUP, ROW_GROUP, D_MODEL), lambda i: (i, 0, 0)),
                   pl.BlockSpec((META_ROWS, TM), lambda i: (0, i)), row_spec(LANES),
                   _const_spec((ROW_GROUP, LANES))],
        out_shape=[jax.ShapeDtypeStruct((n, D_MODEL), F32),
                   jax.ShapeDtypeStruct((n // ROW_GROUP, ROW_GROUP, D_MODEL), F32),
                   jax.ShapeDtypeStruct((META_ROWS, n), F32),
                   jax.ShapeDtypeStruct((n, LANES), F32),
                   jax.ShapeDtypeStruct((ROW_GROUP, LANES), I32)],
        scratch_shapes=[pltpu.VMEM((1, LANES), F32), pltpu.VMEM((MEM_LEN, xdim), BF16),
                        pltpu.VMEM((MEM_LEN, xdim), BF16)],
        compiler_params=_params(("arbitrary",)),
        name="post_attn",
    )(x2d, attn, ga, mixb, wba, wout, g_x, wq, mem2d, g_mem, wkv, wo, g_moe, wr, br)


def _tile_row(ref, group, k):
    return ref.at[group, pl.ds(k, 1), :]


def _hbm_row(ref, row):
    return ref.at[pl.ds(row, 1), :]


def _dispatch_kernel(fill_ref, dest0_ref, dest1_ref, zeros_ref, h3_ref, xd_ref, row_sem, fill_sem):
    i = pl.program_id(0)

    def fill_copy(b):
        return pltpu.make_async_copy(
            zeros_ref, xd_ref.at[pl.ds(pl.multiple_of(b * SLOT_BLOCK, SLOT_BLOCK), SLOT_BLOCK)],
            fill_sem)

    @pl.when(i == 0)
    def _():
        def start(b, c):
            @pl.when(fill_ref[b] > 0)
            def _():
                fill_copy(b).start()
            return c

        def wait(b, c):
            @pl.when(fill_ref[b] > 0)
            def _():
                fill_copy(b).wait()
            return c

        lax.fori_loop(0, fill_ref.shape[0], start, 0)
        lax.fori_loop(0, fill_ref.shape[0], wait, 0)

    def copies(g, k):
        r = g * ROW_GROUP + k
        src = _tile_row(h3_ref, g, k)
        return (pltpu.make_async_copy(src, _hbm_row(xd_ref, dest0_ref[0, 0, r]), row_sem),
                pltpu.make_async_copy(src, _hbm_row(xd_ref, dest1_ref[0, 0, r]), row_sem))

    def issue(g, c):
        for k in range(ROW_GROUP):
            for j, cp in enumerate(copies(g, k)):
                cp.start(priority=j % DMA_THREADS)
        return c

    def drain(g, c):
        for k in range(ROW_GROUP):
            for cp in copies(g, k):
                cp.wait()
        return c

    lax.fori_loop(0, TROW_D // ROW_GROUP, issue, 0)
    lax.fori_loop(0, TROW_D // ROW_GROUP, drain, 0)


def _dispatch(fill_blk, dest, h3, n_slots):
    groups, _, d = h3.shape
    n = groups * ROW_GROUP
    dest0, dest1 = (dest[j].reshape(n // TROW_D, 1, TROW_D) for j in range(2))
    grid_spec = pltpu.PrefetchScalarGridSpec(
        num_scalar_prefetch=1,
        grid=(n // TROW_D,),
        in_specs=[pl.BlockSpec((1, 1, TROW_D), lambda i, fb: (i, 0, 0), memory_space=pltpu.SMEM),
                  pl.BlockSpec((1, 1, TROW_D), lambda i, fb: (i, 0, 0), memory_space=pltpu.SMEM),
                  pl.BlockSpec((SLOT_BLOCK, d), lambda i, fb: (0, 0)),
                  pl.BlockSpec((TROW_D // ROW_GROUP, ROW_GROUP, d), lambda i, fb: (i, 0, 0))],
        out_specs=pl.BlockSpec(memory_space=pl.ANY),
        scratch_shapes=[pltpu.SemaphoreType.DMA, pltpu.SemaphoreType.DMA],
    )
    return pl.pallas_call(
        _dispatch_kernel,
        grid_spec=grid_spec,
        out_shape=jax.ShapeDtypeStruct((n_slots, d), F32),
        compiler_params=_params(("arbitrary",)),
        name="dispatch",
    )(fill_blk, dest0, dest1, jnp.zeros((SLOT_BLOCK, d), F32), h3)


def _experts_kernel(be_ref, nb_ref, nxt_ref, xd_ref, wgu_hbm, wdn_hbm, yd_ref,
                    wgu_f32, wdn_f32, wgu_bf, wdn_bf, slot_ref, sem):
    i = pl.program_id(0)

    def weight_copies(e, slot):
        return (pltpu.make_async_copy(wgu_hbm.at[e], wgu_f32.at[slot], sem.at[slot]),
                pltpu.make_async_copy(wdn_hbm.at[e], wdn_f32.at[slot], sem.at[slot]))

    @pl.when(i == 0)
    def _():
        slot_ref[0] = 0
        for cp in weight_copies(be_ref[0], 0):
            cp.start()

    @pl.when(i < nb_ref[0])
    def _():
        @pl.when((i == 0) | (be_ref[i] != be_ref[jnp.maximum(i - 1, 0)]))
        def _():
            slot = slot_ref[0]
            for cp in weight_copies(be_ref[i], slot):
                cp.wait()
            wgu_bf[...] = wgu_f32[slot].astype(BF16)
            wdn_bf[...] = wdn_f32[slot].astype(BF16)

            @pl.when(nxt_ref[i] >= 0)
            def _():
                for cp in weight_copies(nxt_ref[i], 1 - slot):
                    cp.start()

            slot_ref[0] = 1 - slot

        gu = jnp.dot(xd_ref[...].astype(BF16), wgu_bf[...], preferred_element_type=F32)
        gate = gu[:, :EXPERT_FF]
        up = gu[:, EXPERT_FF:]
        act = (gate * jax.nn.sigmoid(gate) * up).astype(BF16)
        yd_ref[...] = jnp.dot(act, wdn_bf[...], preferred_element_type=F32)


def _experts(block_expert, n_blocks_used, next_expert, xd, w_gate_up, w_down):
    n_slots, d = xd.shape
    slot_map = lambda i, be, nb, nx: (jnp.minimum(i, nb[0] - 1), 0)
    grid_spec = pltpu.PrefetchScalarGridSpec(
        num_scalar_prefetch=3,
        grid=(n_slots // SLOT_BLOCK,),
        in_specs=[pl.BlockSpec((SLOT_BLOCK, d), slot_map),
                  pl.BlockSpec(memory_space=pl.ANY),
                  pl.BlockSpec(memory_space=pl.ANY)],
        out_specs=pl.BlockSpec((SLOT_BLOCK, d), slot_map),
        scratch_shapes=[pltpu.VMEM((2, D_MODEL, 2 * EXPERT_FF), F32),
                        pltpu.VMEM((2, EXPERT_FF, D_MODEL), F32),
                        pltpu.VMEM((D_MODEL, 2 * EXPERT_FF), BF16),
                        pltpu.VMEM((EXPERT_FF, D_MODEL), BF16),
                        pltpu.SMEM((1,), I32),
                        pltpu.SemaphoreType.DMA((2,))],
    )
    return pl.pallas_call(
        _experts_kernel,
        grid_spec=grid_spec,
        out_shape=jax.ShapeDtypeStruct(xd.shape, F32),
        input_output_aliases={3: 0},
        compiler_params=_params(("arbitrary",)),
        name="experts",
    )(block_expert, n_blocks_used, next_expert, xd, w_gate_up, w_down)


def _combine_kernel(dest0_ref, dest1_ref, next0_ref, next1_ref, x2_ref, wts_ref, g_ref, yd_ref, o_ref,
                    buf, sem):
    i = pl.program_id(0)
    last = pl.num_programs(0) - 1
    rows = x2_ref.shape[0]
    slot = i % 2

    def copies(d_refs, slt, g, k):
        r = g * ROW_GROUP + k
        return tuple(pltpu.make_async_copy(_hbm_row(yd_ref, d_ref[0, 0, r]),
                                           _tile_row(buf.at[slt, j], g, k), sem.at[slt])
                     for j, d_ref in enumerate(d_refs))

    def issue(d_ref, slt):
        def body(g, c):
            for k in range(ROW_GROUP):
                for j, cp in enumerate(copies(d_ref, slt, g, k)):
                    cp.start(priority=j % DMA_THREADS)
            return c
        lax.fori_loop(0, rows // ROW_GROUP, body, 0)

    @pl.when(i == 0)
    def _():
        issue((dest0_ref, dest1_ref), 0)

    @pl.when(i < last)
    def _():
        issue((next0_ref, next1_ref), 1 - slot)

    def drain(g, c):
        for k in range(ROW_GROUP):
            for cp in copies((dest0_ref, dest1_ref), slot, g, k):
                cp.wait()
        return c

    lax.fori_loop(0, rows // ROW_GROUP, drain, 0)

    lane = lax.broadcasted_iota(I32, (rows, LANES), 1)
    wts = wts_ref[...]
    w0 = jnp.sum(jnp.where(lane == 0, wts, 0.0), axis=1, keepdims=True)
    w1 = jnp.sum(jnp.where(lane == 1, wts, 0.0), axis=1, keepdims=True)
    y0 = buf[slot, 0].reshape(rows, D_MODEL)
    y1 = buf[slot, 1].reshape(rows, D_MODEL)
    o_ref[...] = _rms(x2_ref[...] + y0 * w0 + y1 * w1, g_ref[...])


def _combine(dest, x2, wts, g_final, yd):
    n = x2.shape[0]
    dest0, dest1 = (dest[j].reshape(n // TROW, 1, TROW) for j in range(2))
    steps = n // TROW
    dest_spec = lambda f: pl.BlockSpec((1, 1, TROW), f, memory_space=pltpu.SMEM)
    cur = lambda i: (i, 0, 0)
    nxt = lambda i: (jnp.minimum(i + 1, steps - 1), 0, 0)
    return pl.pallas_call(
        _combine_kernel,
        grid=(steps,),
        in_specs=[dest_spec(cur), dest_spec(cur), dest_spec(nxt), dest_spec(nxt),
                  pl.BlockSpec((TROW, D_MODEL), lambda i: (i, 0)),
                  pl.BlockSpec((TROW, LANES), lambda i: (i, 0)),
                  _const_spec((1, D_MODEL)),
                  pl.BlockSpec(memory_space=pl.ANY)],
        out_specs=pl.BlockSpec((TROW, D_MODEL), lambda i: (i, 0)),
        out_shape=jax.ShapeDtypeStruct((n, D_MODEL), F32),
        scratch_shapes=[pltpu.VMEM((2, 2, TROW // ROW_GROUP, ROW_GROUP, D_MODEL), F32),
                        pltpu.SemaphoreType.DMA((2,))],
        compiler_params=_params(("arbitrary",)),
        name="combine",
    )(dest0, dest1, dest0, dest1, x2, wts, g_final, yd)


def _lanes(parts, width):
    ref = next(p for p in parts if not isinstance(p, int))
    cols = [jnp.zeros(ref.shape[:-1] + (p,), ref.dtype) if isinstance(p, int) else p for p in parts]
    used = sum(c.shape[-1] for c in cols)
    assert used <= width
    if used < width:
        cols.append(jnp.zeros(ref.shape[:-1] + (width - used,), ref.dtype))
    return jnp.concatenate(cols, axis=-1)


def _head_slots(w, per_head, pieces):
    rows = w.shape[0]
    wh = w.reshape(rows, MLA_HEADS, per_head)
    return _lanes(pieces(wh), HEAD_SLOT).reshape(rows, MLA_HEADS * HEAD_SLOT)


def _layer(x2d, pos3d, invf, mem2d, g_mem, w_kv_mem, batch, seq, g_mix, w_in, b_gate, g_cq, w_uq, g_ckv, w_ukv,
           conv_w, w_branch_a, w_branch_b, w_out, g_xattn, w_q_mem, w_o_mem, g_moe,
           w_router_group, b_router_group, w_router_expert, b_router_expert, w_gate_up, w_down,
           g_final):
    n = x2d.shape[0]
    row = lambda v: v.reshape(1, -1)

    w_kr = w_in[:, COL_CKV:COL_KROPE]
    w_all = jnp.concatenate([_lanes([w_in[:, :COL_CKV], ROPE_LO, w_kr], COL_CKV + HEAD_SLOT),
                             w_in[:, COL_KROPE:]], axis=1).astype(BF16)
    qd = QK_NOPE_DIM + QK_ROPE_DIM
    wq = _head_slots(w_uq, qd, lambda wh: [wh]).astype(BF16)
    kvd = QK_NOPE_DIM + V_HEAD_DIM
    wkk = _head_slots(w_ukv, kvd, lambda wh: [wh[..., :QK_NOPE_DIM]]).astype(BF16)
    wkv = _head_slots(w_ukv, kvd, lambda wh: [wh[..., QK_NOPE_DIM:]]).astype(BF16)
    conv_w8 = jnp.concatenate(
        [conv_w, jnp.zeros((ROW_GROUP - conv_w.shape[0], CONV_WIDTH), F32)], axis=0)

    q, k, v, ga, mixb = _in_proj(x2d, pos3d, invf, row(g_mix), w_all, row(b_gate), row(g_cq),
                                 wq, row(g_ckv), wkk, wkv, conv_w8,
                                 w_branch_b.astype(BF16), seq)
    attn = _mla_attn(q, k, v, batch, seq)

    wr = _lanes([w_router_group, w_router_expert], LANES).astype(BF16)
    br = _lanes([row(b_router_group), row(b_router_expert)], LANES)
    x2, h3, meta, wts, cnt = _post_attn(x2d, attn, ga, mixb, w_branch_a.astype(BF16),
                                   w_out.astype(BF16), row(g_xattn), w_q_mem.astype(BF16), mem2d, row(g_mem),
                                   w_kv_mem.astype(BF16),
                                   w_o_mem.astype(BF16), row(g_moe), wr, br, seq)

    counts = cnt[0, ROUTE_LANE0:ROUTE_LANE0 + N_EXPERTS]
    padded = (counts + SLOT_BLOCK - 1) // SLOT_BLOCK * SLOT_BLOCK
    pad_end = jnp.cumsum(padded)
    pad_start = pad_end - padded
    n_slots = (2 * n + SLOT_BLOCK - 1) // SLOT_BLOCK * SLOT_BLOCK + N_EXPERTS * SLOT_BLOCK
    routed = meta[0:4].astype(I32)
    expert_col = jnp.arange(N_EXPERTS, dtype=I32)[:, None, None]
    dest = jnp.sum(jnp.where(routed[None, 0:2] == expert_col, pad_start[:, None, None], 0),
                   axis=0) + routed[2:4]
    block_lo = jnp.arange(n_slots // SLOT_BLOCK, dtype=I32) * SLOT_BLOCK
    block_expert = jnp.minimum(jnp.sum(pad_end[None, :] <= block_lo[:, None], axis=1),
                               N_EXPERTS - 1).astype(I32)
    n_blocks_used = (pad_end[-1:] // SLOT_BLOCK).astype(I32)
    fill_blk = ((block_lo + SLOT_BLOCK == pad_end[block_expert])
                | (block_lo >= pad_end[-1])).astype(I32)

    cand = jnp.where(padded > 0, jnp.arange(N_EXPERTS, dtype=I32), N_EXPERTS)
    later = jnp.concatenate([lax.cummin(cand, reverse=True)[1:],
                             jnp.full((1,), N_EXPERTS, I32)])
    next_expert = jnp.where(later < N_EXPERTS, later, -1)[block_expert].astype(I32)

    xd = _dispatch(fill_blk, dest, h3, n_slots)
    yd = _experts(block_expert, n_blocks_used, next_expert, xd, w_gate_up, w_down)
    return _combine(dest, x2, wts, row(g_final), yd)


def kernel(x, mem, positions, g_mix, w_in, b_gate, g_cq, w_uq, g_ckv, w_ukv, conv_w, w_branch_a,
           w_branch_b, w_out, g_xattn, g_mem, w_q_mem, w_kv_mem, w_o_mem, g_moe, w_router_group,
           b_router_group, w_router_expert, b_router_expert, w_gate_up, w_down, g_final):
    batch, seq, d = x.shape
    depth = g_mix.shape[0]
    assert d == D_MODEL and depth == 1 and seq % TM == 0 and seq % TQ == 0 and TQ == TK
    n = batch * seq
    x2d = x.reshape(n, d)
    pos3d = positions.astype(I32).reshape(n // TM, TM // LANES, LANES)
    inv_freq = ROPE_THETA ** (-jnp.arange(0, QK_ROPE_DIM, 2, dtype=F32) / QK_ROPE_DIM)
    invf = _lanes([ROPE_LO, inv_freq.reshape(1, -1), inv_freq.reshape(1, -1)], HEAD_SLOT)
    l = 0
    out = _layer(x2d, pos3d, invf, mem.reshape(batch * MEM_LEN, d), g_mem[l], w_kv_mem[l], batch,
                 seq, g_mix[l], w_in[l], b_gate[l], g_cq[l],
                 w_uq[l], g_ckv[l], w_ukv[l], conv_w[l], w_branch_a[l], w_branch_b[l], w_out[l],
                 g_xattn[l], w_q_mem[l], w_o_mem[l], g_moe[l], w_router_group[l],
                 b_router_group[l], w_router_expert[l], b_router_expert[l], w_gate_up[l],
                 w_down[l], g_final)
    return out.reshape(batch, seq, d)
```

```python
import functools
import math

import jax
import jax.numpy as jnp
from jax import lax
from jax.experimental import pallas as pl
from jax.experimental.pallas import tpu as pltpu

F32 = jnp.float32
BF16 = jnp.bfloat16
I32 = jnp.int32

D_MODEL = 1024
MEM_LEN = 256
MLA_HEADS = 8
QK_NOPE_DIM = 64
QK_ROPE_DIM = 32
V_HEAD_DIM = 64
Q_LORA_RANK = 256
KV_LORA_RANK = 128
ROPE_THETA = 10000.0
CONV_WIDTH = 512
XATTN_HEADS = 4
XATTN_HEAD_DIM = 128
N_GROUPS = 8
EXPERTS_PER_GROUP = 8
N_EXPERTS = N_GROUPS * EXPERTS_PER_GROUP
EXPERT_FF = 256
EPS = 1e-6

COL_CQ = Q_LORA_RANK
COL_CKV = COL_CQ + KV_LORA_RANK
COL_KROPE = COL_CKV + QK_ROPE_DIM
COL_CONV = COL_KROPE + 3 * CONV_WIDTH

LANES = 128
HEAD_SLOT = LANES
ROPE_LO = QK_NOPE_DIM
ROPE_HALF = QK_ROPE_DIM // 2
L_LANE = V_HEAD_DIM
SLOT_BLOCK = 256
ROUTE_LANE0 = N_GROUPS
NEG_BIG = -1e30

TM = 1024
TQ = 512
TK = 512
META_ROWS = 8
POST_PARTS = 1
DIAG_STRIPS = 2
ATTN_CHAINS = 8
TROW = 256
TROW_D = 512
DMA_THREADS = 2
ROW_GROUP = 8
VMEM_LIMIT = 56 * 1024 * 1024


def _rms(x, g):
    return x * lax.rsqrt(jnp.mean(x * x, axis=-1, keepdims=True) + EPS) * g


def _const_spec(shape):
    nd = len(shape)
    return pl.BlockSpec(shape, lambda *_: (0,) * nd)


def _params(sem):
    return pltpu.CompilerParams(dimension_semantics=sem, vmem_limit_bytes=VMEM_LIMIT)


def _in_proj_kernel(x_ref, pos_ref, invf_ref, g_mix_ref, w1_ref, w2_ref, w3_ref, b_gate_ref,
                    g_cq_ref, wq_ref, g_ckv_ref, wkk_ref, wkv_ref, conv_w_ref, wbb_ref,
                    q_ref, k_ref, v_ref, ga_ref, mixb_ref, carry_ref, *, tiles_per_seq):
    step = pl.program_id(0)
    tm = x_ref.shape[0]
    h = _rms(x_ref[...], g_mix_ref[...]).astype(BF16)

    lane = lax.broadcasted_iota(I32, (tm, HEAD_SLOT), 1)
    rope_lane = (lane >= ROPE_LO) & (lane < ROPE_LO + QK_ROPE_DIM)
    pos_rows = pos_ref[0].astype(F32)
    pos_tile = jnp.concatenate([jnp.broadcast_to(pos_rows[g:g + 1, :], (LANES, LANES))
                                for g in range(tm // LANES)], axis=0)
    row_lane = lax.broadcasted_iota(I32, (tm, LANES), 0) & (LANES - 1)
    pos_col = jnp.sum(jnp.where(lane == row_lane, pos_tile, 0.0), axis=1, keepdims=True)
    ang = pos_col * invf_ref[...]
    first_half = lane < ROPE_LO + ROPE_HALF
    cos_t = jnp.where(rope_lane, jnp.cos(ang), 0.0)
    sin_t = jnp.where(rope_lane, jnp.sin(ang), 0.0)
    sin_t = jnp.where(first_half, -sin_t, sin_t)

    def swap_halves(t):
        return jnp.where(first_half, pltpu.roll(t, HEAD_SLOT - ROPE_HALF, 1),
                         pltpu.roll(t, ROPE_HALF, 1))

    p1 = jnp.dot(h, w1_ref[...], preferred_element_type=F32)
    c_q = p1[:, :COL_CQ]
    c_kv = p1[:, COL_CQ:COL_CKV]
    kr = p1[:, COL_CKV:COL_CKV + HEAD_SLOT]
    k_rot = kr * cos_t + swap_halves(kr) * sin_t

    scale = math.log2(math.e) / math.sqrt(QK_NOPE_DIM + QK_ROPE_DIM)
    q_cos = jnp.where(lane < ROPE_LO, scale, cos_t * scale)
    q_sin = sin_t * scale
    cqn = _rms(c_q, g_cq_ref[...]).astype(BF16)
    qq = jnp.dot(cqn, wq_ref[...], preferred_element_type=F32)
    ckvn = _rms(c_kv, g_ckv_ref[...]).astype(BF16)
    kk = jnp.dot(ckvn, wkk_ref[...], preferred_element_type=F32)
    vv = jnp.dot(ckvn, wkv_ref[...], preferred_element_type=F32)
    ones_lane = jnp.where(lane == L_LANE, 1.0, 0.0)
    for hd in range(MLA_HEADS):
        sl = slice(hd * HEAD_SLOT, (hd + 1) * HEAD_SLOT)
        q_ref[:, sl] = (qq[:, sl] * q_cos + swap_halves(qq[:, sl]) * q_sin).astype(BF16)
        k_ref[:, sl] = (kk[:, sl] + k_rot).astype(BF16)
        v_ref[:, sl] = (vv[:, sl] + ones_lane).astype(BF16)

    p2 = jnp.dot(h, w2_ref[...], preferred_element_type=F32)
    b_gate = p2[:, :CONV_WIDTH]
    u = p2[:, CONV_WIDTH:2 * CONV_WIDTH] * p2[:, 2 * CONV_WIDTH:]

    @pl.when(step % tiles_per_seq == 0)
    def _():
        carry_ref[...] = jnp.zeros_like(carry_ref)

    prev1 = carry_ref[ROW_GROUP - 1:ROW_GROUP, :]
    prev2 = carry_ref[ROW_GROUP - 2:ROW_GROUP - 1, :]
    row = lax.broadcasted_iota(I32, (tm, CONV_WIDTH), 0)
    u1 = jnp.where(row == 0, prev1, pltpu.roll(u, 1, 0))
    u2 = jnp.where(row == 0, prev2, jnp.where(row == 1, prev1, pltpu.roll(u, 2, 0)))
    carry_ref[...] = u[tm - ROW_GROUP:, :]
    cw = conv_w_ref[...]
    conv = b_gate * (cw[0:1, :] * u2 + cw[1:2, :] * u1 + cw[2:3, :] * u)
    branch_b = jnp.dot(conv.astype(BF16), wbb_ref[...], preferred_element_type=F32)

    gates = jax.nn.sigmoid(jnp.dot(h, w3_ref[...], preferred_element_type=F32) + b_gate_ref[...])
    ga_ref[...] = gates[:, :D_MODEL].astype(BF16)
    mixb_ref[...] = (gates[:, D_MODEL:] * branch_b).astype(BF16)


def _in_proj(x2d, pos3d, invf, g_mix, w1, w2, w3, b_gate, g_cq, wq, g_ckv, wkk, wkv,
             conv_w, wbb, seq):
    n = x2d.shape[0]
    row_spec = lambda c: pl.BlockSpec((TM, c), lambda i: (i, 0))
    consts = [invf, g_mix, w1, w2, w3, b_gate, g_cq, wq, g_ckv, wkk, wkv, conv_w, wbb]
    kv_cols = MLA_HEADS * HEAD_SLOT
    return pl.pallas_call(
        functools.partial(_in_proj_kernel, tiles_per_seq=seq // TM),
        grid=(n // TM,),
        in_specs=[row_spec(D_MODEL), pl.BlockSpec((1, TM // LANES, LANES), lambda i: (i, 0, 0))]
        + [_const_spec(c.shape) for c in consts],
        out_specs=[row_spec(kv_cols), row_spec(kv_cols), row_spec(kv_cols),
                   row_spec(D_MODEL), row_spec(D_MODEL)],
        out_shape=[jax.ShapeDtypeStruct((n, kv_cols), BF16),
                   jax.ShapeDtypeStruct((n, kv_cols), BF16),
                   jax.ShapeDtypeStruct((n, kv_cols), BF16),
                   jax.ShapeDtypeStruct((n, D_MODEL), BF16),
                   jax.ShapeDtypeStruct((n, D_MODEL), BF16)],
        scratch_shapes=[pltpu.VMEM((ROW_GROUP, CONV_WIDTH), F32)],
        compiler_params=_params(("arbitrary",)),
        name="in_proj",
    )(x2d, pos3d, *consts)


def _attn_kernel(q_ref, k_ref, v_ref, o_ref):
    qi = pl.program_id(1)
    tq = q_ref.shape[0]

    def block(carry, heads, col_start, ncols, row_lo, visible):
        out = []
        for c, hd in enumerate(heads):
            m, acc = carry[c]
            hsl = slice(hd * HEAD_SLOT, (hd + 1) * HEAD_SLOT)
            kb = k_ref[pl.ds(col_start, ncols), hsl]
            vb = v_ref[pl.ds(col_start, ncols), hsl]
            s = lax.dot_general(q_ref[row_lo:, hsl], kb, (((1,), (1,)), ((), ())),
                                preferred_element_type=F32)
            if visible is not None:
                s = jnp.where(visible, s, NEG_BIG)
            m_hi, acc_hi = m[row_lo:], acc[row_lo:]
            m_new = jnp.maximum(m_hi, jnp.max(s, axis=1, keepdims=True))
            p = jnp.exp2(s - m_new).astype(BF16)
            acc_new = jnp.exp2(m_hi - m_new) * acc_hi + jnp.dot(p, vb, preferred_element_type=F32)
            if row_lo:
                m_new = jnp.concatenate([m[:row_lo], m_new], axis=0)
                acc_new = jnp.concatenate([acc[:row_lo], acc_new], axis=0)
            out.append((m_new, acc_new))
        return tuple(out)

    def full_step(j, carry, heads):
        return block(carry, heads, pl.multiple_of(j * TK, TK), TK, 0, None)

    lane = lax.broadcasted_iota(I32, (tq, HEAD_SLOT), 1)
    strip_masks = []
    for d in range(DIAG_STRIPS):
        shape = (tq - d * (tq // DIAG_STRIPS), TK // DIAG_STRIPS)
        strip_masks.append(lax.broadcasted_iota(I32, shape, 1) <= lax.broadcasted_iota(I32, shape, 0))
    for g in range(MLA_HEADS // ATTN_CHAINS):
        heads = tuple(range(g * ATTN_CHAINS, (g + 1) * ATTN_CHAINS))
        carry = tuple((jnp.full((tq, 1), NEG_BIG, F32), jnp.zeros((tq, HEAD_SLOT), F32))
                      for _ in heads)
        carry = lax.fori_loop(0, qi, functools.partial(full_step, heads=heads), carry)
        for d in range(DIAG_STRIPS):
            width, row_lo = TK // DIAG_STRIPS, d * (tq // DIAG_STRIPS)
            col = pl.multiple_of(qi * TK + d * width, width)
            carry = block(carry, heads, col, width, row_lo, strip_masks[d])
        outs = []
        for _, acc in carry:
            denom = jnp.sum(jnp.where(lane == L_LANE, acc, 0.0), axis=1, keepdims=True)
            outs.append(acc / denom)
        for c in range(0, ATTN_CHAINS, 2):
            hp = heads[c] // 2
            pair = jnp.where(lane < V_HEAD_DIM, outs[c], pltpu.roll(outs[c + 1], V_HEAD_DIM, 1))
            o_ref[:, hp * HEAD_SLOT:(hp + 1) * HEAD_SLOT] = pair.astype(BF16)


def _mla_attn(q, k, v, batch, seq):
    n = q.shape[0]
    nq = seq // TQ
    out_cols = MLA_HEADS * V_HEAD_DIM
    return pl.pallas_call(
        _attn_kernel,
        grid=(batch, nq),
        in_specs=[pl.BlockSpec((TQ, q.shape[1]), lambda b, i: (b * nq + i, 0)),
                  pl.BlockSpec((seq, k.shape[1]), lambda b, i: (b, 0)),
                  pl.BlockSpec((seq, v.shape[1]), lambda b, i: (b, 0))],
        out_specs=pl.BlockSpec((TQ, out_cols), lambda b, i: (b * nq + i, 0)),
        out_shape=jax.ShapeDtypeStruct((n, out_cols), BF16),
        compiler_params=_params(("arbitrary", "arbitrary")),
        name="mla_attn",
    )(q, k, v)


def _post_attn_kernel(x_ref, attn_ref, ga_ref, mixb_ref, wba_ref, wout_ref, g_x_ref, wq_ref,
                      mem_ref, g_mem_ref, wkv_ref, wo_ref, g_moe_ref, wr_ref, br_ref,
                      x2_ref, h3_ref, meta_ref, wts_ref, cnt_ref, run_ref, km_ref, vm_ref,
                      *, tiles_per_seq):
    step = pl.program_id(0)
    tm = x_ref.shape[0]
    xdim = XATTN_HEADS * XATTN_HEAD_DIM

    @pl.when(step == 0)
    def _():
        run_ref[...] = jnp.zeros_like(run_ref)

    @pl.when(step % tiles_per_seq == 0)
    def _():
        mem_n = _rms(mem_ref[...], g_mem_ref[...]).astype(BF16)
        kv = jnp.dot(mem_n, wkv_ref[...], preferred_element_type=F32)
        km_ref[...] = kv[:, :xdim].astype(BF16)
        vm_ref[...] = kv[:, xdim:].astype(BF16)

    def mixer_and_xattn(rows):
        a = jnp.dot(attn_ref[rows, :], wba_ref[...], preferred_element_type=F32)
        mixed = ga_ref[rows, :].astype(F32) * a + mixb_ref[rows, :].astype(F32)
        x1 = x_ref[rows, :] + jnp.dot(mixed.astype(BF16), wout_ref[...], preferred_element_type=F32)

        h2 = _rms(x1, g_x_ref[...]).astype(BF16)
        qm = jnp.dot(h2, wq_ref[...], preferred_element_type=F32) * (1.0 / math.sqrt(XATTN_HEAD_DIM))
        qm = qm.astype(BF16)
        heads = []
        for hd in range(XATTN_HEADS):
            sl = slice(hd * XATTN_HEAD_DIM, (hd + 1) * XATTN_HEAD_DIM)
            s = lax.dot_general(qm[:, sl], km_ref[:, sl], (((1,), (1,)), ((), ())),
                                preferred_element_type=F32)
            p = jnp.exp(s - jnp.max(s, axis=1, keepdims=True))
            o = jnp.dot(p.astype(BF16), vm_ref[:, sl], preferred_element_type=F32)
            heads.append((o / jnp.sum(p, axis=1, keepdims=True)).astype(BF16))
        o_all = jnp.concatenate(heads, axis=1)
        x2 = x1 + jnp.dot(o_all, wo_ref[...], preferred_element_type=F32)
        x2_ref[rows, :] = x2
        h3 = _rms(x2, g_moe_ref[...])
        g0, g1 = rows.start // ROW_GROUP, rows.stop // ROW_GROUP
        h3_ref[g0:g1] = h3.reshape(g1 - g0, ROW_GROUP, D_MODEL)
        return jnp.dot(h3.astype(BF16), wr_ref[...], preferred_element_type=F32) + br_ref[...]

    part = tm // POST_PARTS
    logits = jnp.concatenate([mixer_and_xattn(slice(c * part, (c + 1) * part))
                              for c in range(POST_PARTS)], axis=0)

    lane = lax.broadcasted_iota(I32, (tm, LANES), 1)
    lane_f = lane.astype(F32)

    def first_lane(hit):
        return jnp.min(jnp.where(hit, lane_f, float(LANES)), axis=1, keepdims=True)

    gl = jnp.where(lane < N_GROUPS, logits, -jnp.inf)
    gmax = jnp.max(gl, axis=1, keepdims=True)
    g_idx = first_lane(gl == gmax).astype(I32)
    p_group = 1.0 / jnp.sum(jnp.exp(gl - gmax), axis=1, keepdims=True)
    e_lo = ROUTE_LANE0 + g_idx * EXPERTS_PER_GROUP
    el = jnp.where((lane >= e_lo) & (lane < e_lo + EXPERTS_PER_GROUP), logits, -jnp.inf)
    v1 = jnp.max(el, axis=1, keepdims=True)
    i1 = first_lane(el == v1)
    el2 = jnp.where(lane_f == i1, -jnp.inf, el)
    v2 = jnp.max(el2, axis=1, keepdims=True)
    i2 = first_lane(el2 == v2)
    t = jnp.exp(v2 - v1)
    w0 = p_group / (1.0 + t)
    w1 = p_group * t / (1.0 + t)

    oh0 = lane_f == i1
    oh1 = lane_f == i2
    oh = (oh0 | oh1).astype(BF16)
    r_i = lax.broadcasted_iota(I32, (tm, tm), 0)
    c_i = lax.broadcasted_iota(I32, (tm, tm), 1)
    lower = (c_i < r_i).astype(BF16)
    before = jnp.dot(lower, oh, preferred_element_type=F32) + run_ref[...]
    rank0 = jnp.sum(jnp.where(oh0, before, 0.0), axis=1, keepdims=True)
    rank1 = jnp.sum(jnp.where(oh1, before, 0.0), axis=1, keepdims=True)
    run_ref[...] = run_ref[...] + jnp.sum(oh.astype(F32), axis=0, keepdims=True)
    cnt_ref[...] = jnp.broadcast_to(run_ref[...], cnt_ref.shape).astype(I32)

    meta = jnp.where(lane == 0, i1 - ROUTE_LANE0, 0.0)
    meta = jnp.where(lane == 1, i2 - ROUTE_LANE0, meta)
    meta = jnp.where(lane == 2, rank0, meta)
    meta = jnp.where(lane == 3, rank1, meta)
    meta_ref[...] = meta.T[:META_ROWS, :]
    wts_ref[...] = jnp.where(lane == 0, w0, jnp.where(lane == 1, w1, 0.0))


def _post_attn(x2d, attn, ga, mixb, wba, wout, g_x, wq, mem2d, g_mem, wkv, wo, g_moe, wr, br, seq):
    n = x2d.shape[0]
    tps = seq // TM
    xdim = XATTN_HEADS * XATTN_HEAD_DIM
    row_spec = lambda c: pl.BlockSpec((TM, c), lambda i: (i, 0))
    consts = lambda *arrs: [_const_spec(a.shape) for a in arrs]
    return pl.pallas_call(
        functools.partial(_post_attn_kernel, tiles_per_seq=tps),
        grid=(n // TM,),
        in_specs=([row_spec(D_MODEL), row_spec(MLA_HEADS * V_HEAD_DIM), row_spec(D_MODEL),
                   row_spec(D_MODEL)] + consts(wba, wout, g_x, wq)
                  + [pl.BlockSpec((MEM_LEN, D_MODEL), lambda i: (i // tps, 0))]
                  + consts(g_mem, wkv, wo, g_moe, wr, br)),
        out_specs=[row_spec(D_MODEL),
                   pl.BlockSpec((TM // ROW_GROUP, ROW_GROUP, D_MODEL), lambda i: (i, 0, 0)),
                   pl.BlockSpec((META_ROWS, TM), lambda i: (0, i)), row_spec(LANES),
                   _const_spec((ROW_GROUP, LANES))],
        out_shape=[jax.ShapeDtypeStruct((n, D_MODEL), F32),
                   jax.ShapeDtypeStruct((n // ROW_GROUP, ROW_GROUP, D_MODEL), F32),
                   jax.ShapeDtypeStruct((META_ROWS, n), F32),
                   jax.ShapeDtypeStruct((n, LANES), F32),
                   jax.ShapeDtypeStruct((ROW_GROUP, LANES), I32)],
        scratch_shapes=[pltpu.VMEM((1, LANES), F32), pltpu.VMEM((MEM_LEN, xdim), BF16),
                        pltpu.VMEM((MEM_LEN, xdim), BF16)],
        compiler_params=_params(("arbitrary",)),
        name="post_attn",
    )(x2d, attn, ga, mixb, wba, wout, g_x, wq, mem2d, g_mem, wkv, wo, g_moe, wr, br)


def _tile_row(ref, group, k):
    return ref.at[group, pl.ds(k, 1), :]


def _hbm_row(ref, row):
    return ref.at[pl.ds(row, 1), :]


def _dispatch_kernel(fill_ref, dest0_ref, dest1_ref, zeros_ref, h3_ref, xd_ref, row_sem, fill_sem):
    i = pl.program_id(0)

    def fill_copy(b):
        return pltpu.make_async_copy(
            zeros_ref, xd_ref.at[pl.ds(pl.multiple_of(b * SLOT_BLOCK, SLOT_BLOCK), SLOT_BLOCK)],
            fill_sem)

    @pl.when(i == 0)
    def _():
        def start(b, c):
            @pl.when(fill_ref[b] > 0)
            def _():
                fill_copy(b).start()
            return c

        def wait(b, c):
            @pl.when(fill_ref[b] > 0)
            def _():
                fill_copy(b).wait()
            return c

        lax.fori_loop(0, fill_ref.shape[0], start, 0)
        lax.fori_loop(0, fill_ref.shape[0], wait, 0)

    def copies(g, k):
        r = g * ROW_GROUP + k
        src = _tile_row(h3_ref, g, k)
        return (pltpu.make_async_copy(src, _hbm_row(xd_ref, dest0_ref[0, 0, r]), row_sem),
                pltpu.make_async_copy(src, _hbm_row(xd_ref, dest1_ref[0, 0, r]), row_sem))

    def issue(g, c):
        for k in range(ROW_GROUP):
            for j, cp in enumerate(copies(g, k)):
                cp.start(priority=j % DMA_THREADS)
        return c

    def drain(g, c):
        for k in range(ROW_GROUP):
            for cp in copies(g, k):
                cp.wait()
        return c

    lax.fori_loop(0, TROW_D // ROW_GROUP, issue, 0)
    lax.fori_loop(0, TROW_D // ROW_GROUP, drain, 0)


def _dispatch(fill_blk, dest, h3, n_slots):
    groups, _, d = h3.shape
    n = groups * ROW_GROUP
    dest0, dest1 = (dest[j].reshape(n // TROW_D, 1, TROW_D) for j in range(2))
    grid_spec = pltpu.PrefetchScalarGridSpec(
        num_scalar_prefetch=1,
        grid=(n // TROW_D,),
        in_specs=[pl.BlockSpec((1, 1, TROW_D), lambda i, fb: (i, 0, 0), memory_space=pltpu.SMEM),
                  pl.BlockSpec((1, 1, TROW_D), lambda i, fb: (i, 0, 0), memory_space=pltpu.SMEM),
                  pl.BlockSpec((SLOT_BLOCK, d), lambda i, fb: (0, 0)),
                  pl.BlockSpec((TROW_D // ROW_GROUP, ROW_GROUP, d), lambda i, fb: (i, 0, 0))],
        out_specs=pl.BlockSpec(memory_space=pl.ANY),
        scratch_shapes=[pltpu.SemaphoreType.DMA, pltpu.SemaphoreType.DMA],
    )
    return pl.pallas_call(
        _dispatch_kernel,
        grid_spec=grid_spec,
        out_shape=jax.ShapeDtypeStruct((n_slots, d), F32),
        compiler_params=_params(("arbitrary",)),
        name="dispatch",
    )(fill_blk, dest0, dest1, jnp.zeros((SLOT_BLOCK, d), F32), h3)


def _experts_kernel(be_ref, nb_ref, nxt_ref, xd_ref, wgu_hbm, wdn_hbm, yd_ref,
                    wgu_f32, wdn_f32, wgu_bf, wdn_bf, slot_ref, sem):
    i = pl.program_id(0)

    def weight_copies(e, slot):
        return (pltpu.make_async_copy(wgu_hbm.at[e], wgu_f32.at[slot], sem.at[slot]),
                pltpu.make_async_copy(wdn_hbm.at[e], wdn_f32.at[slot], sem.at[slot]))

    @pl.when(i == 0)
    def _():
        slot_ref[0] = 0
        for cp in weight_copies(be_ref[0], 0):
            cp.start()

    @pl.when(i < nb_ref[0])
    def _():
        @pl.when((i == 0) | (be_ref[i] != be_ref[jnp.maximum(i - 1, 0)]))
        def _():
            slot = slot_ref[0]
            for cp in weight_copies(be_ref[i], slot):
                cp.wait()
            wgu_bf[...] = wgu_f32[slot].astype(BF16)
            wdn_bf[...] = wdn_f32[slot].astype(BF16)

            @pl.when(nxt_ref[i] >= 0)
            def _():
                for cp in weight_copies(nxt_ref[i], 1 - slot):
                    cp.start()

            slot_ref[0] = 1 - slot

        gu = jnp.dot(xd_ref[...].astype(BF16), wgu_bf[...], preferred_element_type=F32)
        gate = gu[:, :EXPERT_FF]
        up = gu[:, EXPERT_FF:]
        act = (gate * jax.nn.sigmoid(gate) * up).astype(BF16)
        yd_ref[...] = jnp.dot(act, wdn_bf[...], preferred_element_type=F32)


def _experts(block_expert, n_blocks_used, next_expert, xd, w_gate_up, w_down):
    n_slots, d = xd.shape
    slot_map = lambda i, be, nb, nx: (jnp.minimum(i, nb[0] - 1), 0)
    grid_spec = pltpu.PrefetchScalarGridSpec(
        num_scalar_prefetch=3,
        grid=(n_slots // SLOT_BLOCK,),
        in_specs=[pl.BlockSpec((SLOT_BLOCK, d), slot_map),
                  pl.BlockSpec(memory_space=pl.ANY),
                  pl.BlockSpec(memory_space=pl.ANY)],
        out_specs=pl.BlockSpec((SLOT_BLOCK, d), slot_map),
        scratch_shapes=[pltpu.VMEM((2, D_MODEL, 2 * EXPERT_FF), F32),
                        pltpu.VMEM((2, EXPERT_FF, D_MODEL), F32),
                        pltpu.VMEM((D_MODEL, 2 * EXPERT_FF), BF16),
                        pltpu.VMEM((EXPERT_FF, D_MODEL), BF16),
                        pltpu.SMEM((1,), I32),
                        pltpu.SemaphoreType.DMA((2,))],
    )
    return pl.pallas_call(
        _experts_kernel,
        grid_spec=grid_spec,
        out_shape=jax.ShapeDtypeStruct(xd.shape, F32),
        input_output_aliases={3: 0},
        compiler_params=_params(("arbitrary",)),
        name="experts",
    )(block_expert, n_blocks_used, next_expert, xd, w_gate_up, w_down)


def _combine_kernel(dest0_ref, dest1_ref, next0_ref, next1_ref, x2_ref, wts_ref, g_ref, yd_ref, o_ref,
                    buf, sem):
    i = pl.program_id(0)
    last = pl.num_programs(0) - 1
    rows = x2_ref.shape[0]
    slot = i % 2

    def copies(d_refs, slt, g, k):
        r = g * ROW_GROUP + k
        return tuple(pltpu.make_async_copy(_hbm_row(yd_ref, d_ref[0, 0, r]),
                                           _tile_row(buf.at[slt, j], g, k), sem.at[slt])
                     for j, d_ref in enumerate(d_refs))

    def issue(d_ref, slt):
        def body(g, c):
            for k in range(ROW_GROUP):
                for j, cp in enumerate(copies(d_ref, slt, g, k)):
                    cp.start(priority=j % DMA_THREADS)
            return c
        lax.fori_loop(0, rows // ROW_GROUP, body, 0)

    @pl.when(i == 0)
    def _():
        issue((dest0_ref, dest1_ref), 0)

    @pl.when(i < last)
    def _():
        issue((next0_ref, next1_ref), 1 - slot)

    def drain(g, c):
        for k in range(ROW_GROUP):
            for cp in copies((dest0_ref, dest1_ref), slot, g, k):
                cp.wait()
        return c

    lax.fori_loop(0, rows // ROW_GROUP, drain, 0)

    lane = lax.broadcasted_iota(I32, (rows, LANES), 1)
    wts = wts_ref[...]
    w0 = jnp.sum(jnp.where(lane == 0, wts, 0.0), axis=1, keepdims=True)
    w1 = jnp.sum(jnp.where(lane == 1, wts, 0.0), axis=1, keepdims=True)
    y0 = buf[slot, 0].reshape(rows, D_MODEL)
    y1 = buf[slot, 1].reshape(rows, D_MODEL)
    o_ref[...] = _rms(x2_ref[...] + y0 * w0 + y1 * w1, g_ref[...])


def _combine(dest, x2, wts, g_final, yd):
    n = x2.shape[0]
    dest0, dest1 = (dest[j].reshape(n // TROW, 1, TROW) for j in range(2))
    steps = n // TROW
    dest_spec = lambda f: pl.BlockSpec((1, 1, TROW), f, memory_space=pltpu.SMEM)
    cur = lambda i: (i, 0, 0)
    nxt = lambda i: (jnp.minimum(i + 1, steps - 1), 0, 0)
    return pl.pallas_call(
        _combine_kernel,
        grid=(steps,),
        in_specs=[dest_spec(cur), dest_spec(cur), dest_spec(nxt), dest_spec(nxt),
                  pl.BlockSpec((TROW, D_MODEL), lambda i: (i, 0)),
                  pl.BlockSpec((TROW, LANES), lambda i: (i, 0)),
                  _const_spec((1, D_MODEL)),
                  pl.BlockSpec(memory_space=pl.ANY)],
        out_specs=pl.BlockSpec((TROW, D_MODEL), lambda i: (i, 0)),
        out_shape=jax.ShapeDtypeStruct((n, D_MODEL), F32),
        scratch_shapes=[pltpu.VMEM((2, 2, TROW // ROW_GROUP, ROW_GROUP, D_MODEL), F32),
                        pltpu.SemaphoreType.DMA((2,))],
        compiler_params=_params(("arbitrary",)),
        name="combine",
    )(dest0, dest1, dest0, dest1, x2, wts, g_final, yd)


def _lanes(parts, width):
    ref = next(p for p in parts if not isinstance(p, int))
    cols = [jnp.zeros(ref.shape[:-1] + (p,), ref.dtype) if isinstance(p, int) else p for p in parts]
    used = sum(c.shape[-1] for c in cols)
    assert used <= width
    if used < width:
        cols.append(jnp.zeros(ref.shape[:-1] + (width - used,), ref.dtype))
    return jnp.concatenate(cols, axis=-1)


def _head_slots(w, per_head, pieces):
    rows = w.shape[0]
    wh = w.reshape(rows, MLA_HEADS, per_head)
    return _lanes(pieces(wh), HEAD_SLOT).reshape(rows, MLA_HEADS * HEAD_SLOT)


def _layer(x2d, pos3d, invf, mem2d, g_mem, w_kv_mem, batch, seq, g_mix, w_in, b_gate, g_cq, w_uq, g_ckv, w_ukv,
           conv_w, w_branch_a, w_branch_b, w_out, g_xattn, w_q_mem, w_o_mem, g_moe,
           w_router_group, b_router_group, w_router_expert, b_router_expert, w_gate_up, w_down,
           g_final):
    n = x2d.shape[0]
    row = lambda v: v.reshape(1, -1)

    w_kr = w_in[:, COL_CKV:COL_KROPE]
    w1 = _lanes([w_in[:, :COL_CKV], ROPE_LO, w_kr], COL_CKV + HEAD_SLOT).astype(BF16)
    w2 = w_in[:, COL_KROPE:COL_CONV].astype(BF16)
    w3 = w_in[:, COL_CONV:].astype(BF16)
    qd = QK_NOPE_DIM + QK_ROPE_DIM
    wq = _head_slots(w_uq, qd, lambda wh: [wh]).astype(BF16)
    kvd = QK_NOPE_DIM + V_HEAD_DIM
    wkk = _head_slots(w_ukv, kvd, lambda wh: [wh[..., :QK_NOPE_DIM]]).astype(BF16)
    wkv = _head_slots(w_ukv, kvd, lambda wh: [wh[..., QK_NOPE_DIM:]]).astype(BF16)
    conv_w8 = jnp.concatenate(
        [conv_w, jnp.zeros((ROW_GROUP - conv_w.shape[0], CONV_WIDTH), F32)], axis=0)

    q, k, v, ga, mixb = _in_proj(x2d, pos3d, invf, row(g_mix), w1, w2, w3, row(b_gate), row(g_cq),
                                 wq, row(g_ckv), wkk, wkv, conv_w8,
                                 w_branch_b.astype(BF16), seq)
    attn = _mla_attn(q, k, v, batch, seq)

    wr = _lanes([w_router_group, w_router_expert], LANES).astype(BF16)
    br = _lanes([row(b_router_group), row(b_router_expert)], LANES)
    x2, h3, meta, wts, cnt = _post_attn(x2d, attn, ga, mixb, w_branch_a.astype(BF16),
                                   w_out.astype(BF16), row(g_xattn), w_q_mem.astype(BF16), mem2d, row(g_mem),
                                   w_kv_mem.astype(BF16),
                                   w_o_mem.astype(BF16), row(g_moe), wr, br, seq)

    counts = cnt[0, ROUTE_LANE0:ROUTE_LANE0 + N_EXPERTS]
    padded = (counts + SLOT_BLOCK - 1) // SLOT_BLOCK * SLOT_BLOCK
    pad_end = jnp.cumsum(padded)
    pad_start = pad_end - padded
    n_slots = (2 * n + SLOT_BLOCK - 1) // SLOT_BLOCK * SLOT_BLOCK + N_EXPERTS * SLOT_BLOCK
    routed = meta[0:4].astype(I32)
    expert_col = jnp.arange(N_EXPERTS, dtype=I32)[:, None, None]
    dest = jnp.sum(jnp.where(routed[None, 0:2] == expert_col, pad_start[:, None, None], 0),
                   axis=0) + routed[2:4]
    block_lo = jnp.arange(n_slots // SLOT_BLOCK, dtype=I32) * SLOT_BLOCK
    block_expert = jnp.minimum(jnp.sum(pad_end[None, :] <= block_lo[:, None], axis=1),
                               N_EXPERTS - 1).astype(I32)
    n_blocks_used = (pad_end[-1:] // SLOT_BLOCK).astype(I32)
    fill_blk = ((block_lo + SLOT_BLOCK == pad_end[block_expert])
                | (block_lo >= pad_end[-1])).astype(I32)

    cand = jnp.where(padded > 0, jnp.arange(N_EXPERTS, dtype=I32), N_EXPERTS)
    later = jnp.concatenate([lax.cummin(cand, reverse=True)[1:],
                             jnp.full((1,), N_EXPERTS, I32)])
    next_expert = jnp.where(later < N_EXPERTS, later, -1)[block_expert].astype(I32)

    xd = _dispatch(fill_blk, dest, h3, n_slots)
    yd = _experts(block_expert, n_blocks_used, next_expert, xd, w_gate_up, w_down)
    return _combine(dest, x2, wts, row(g_final), yd)


def kernel(x, mem, positions, g_mix, w_in, b_gate, g_cq, w_uq, g_ckv, w_ukv, conv_w, w_branch_a,
           w_branch_b, w_out, g_xattn, g_mem, w_q_mem, w_kv_mem, w_o_mem, g_moe, w_router_group,
           b_router_group, w_router_expert, b_router_expert, w_gate_up, w_down, g_final):
    batch, seq, d = x.shape
    depth = g_mix.shape[0]
    assert d == D_MODEL and depth == 1 and seq % TM == 0 and seq % TQ == 0 and TQ == TK
    n = batch * seq
    x2d = x.reshape(n, d)
    pos3d = positions.astype(I32).reshape(n // TM, TM // LANES, LANES)
    inv_freq = ROPE_THETA ** (-jnp.arange(0, QK_ROPE_DIM, 2, dtype=F32) / QK_ROPE_DIM)
    invf = _lanes([ROPE_LO, inv_freq.reshape(1, -1), inv_freq.reshape(1, -1)], HEAD_SLOT)
    l = 0
    out = _layer(x2d, pos3d, invf, mem.reshape(batch * MEM_LEN, d), g_mem[l], w_kv_mem[l], batch,
                 seq, g_mix[l], w_in[l], b_gate[l], g_cq[l],
                 w_uq[l], g_ckv[l], w_ukv[l], conv_w[l], w_branch_a[l], w_branch_b[l], w_out[l],
                 g_xattn[l], w_q_mem[l], w_o_mem[l], g_moe[l], w_router_group[l],
                 b_router_group[l], w_router_expert[l], b_router_expert[l], w_gate_up[l],
                 w_down[l], g_final)
    return out.reshape(batch, seq, d)
```

```python
import functools
import math

import jax
import jax.numpy as jnp
from jax import lax
from jax.experimental import pallas as pl
from jax.experimental.pallas import tpu as pltpu

F32 = jnp.float32
BF16 = jnp.bfloat16
I32 = jnp.int32

D_MODEL = 1024
MEM_LEN = 256
MLA_HEADS = 8
QK_NOPE_DIM = 64
QK_ROPE_DIM = 32
V_HEAD_DIM = 64
Q_LORA_RANK = 256
KV_LORA_RANK = 128
ROPE_THETA = 10000.0
CONV_WIDTH = 512
XATTN_HEADS = 4
XATTN_HEAD_DIM = 128
N_GROUPS = 8
EXPERTS_PER_GROUP = 8
N_EXPERTS = N_GROUPS * EXPERTS_PER_GROUP
EXPERT_FF = 256
EPS = 1e-6

COL_CQ = Q_LORA_RANK
COL_CKV = COL_CQ + KV_LORA_RANK
COL_KROPE = COL_CKV + QK_ROPE_DIM
COL_CONV = COL_KROPE + 3 * CONV_WIDTH

LANES = 128
HEAD_SLOT = LANES
ROPE_LO = QK_NOPE_DIM
ROPE_HALF = QK_ROPE_DIM // 2
L_LANE = V_HEAD_DIM
SLOT_BLOCK = 256
ROUTE_LANE0 = N_GROUPS
NEG_BIG = -1e30

TM = 1024
TQ = 512
TK = 512
META_ROWS = 8
POST_PARTS = 1
DIAG_STRIPS = 2
ATTN_CHAINS = 8
TROW = 256
TROW_D = 512
DMA_THREADS = 2
ROW_GROUP = 8
VMEM_LIMIT = 56 * 1024 * 1024


def _rms(x, g):
    return x * lax.rsqrt(jnp.mean(x * x, axis=-1, keepdims=True) + EPS) * g


def _const_spec(shape):
    nd = len(shape)
    return pl.BlockSpec(shape, lambda *_: (0,) * nd)


def _params(sem):
    return pltpu.CompilerParams(dimension_semantics=sem, vmem_limit_bytes=VMEM_LIMIT)


def _in_proj_kernel(x_ref, pos_ref, invf_ref, g_mix_ref, w1_ref, w2_ref, w3_ref, b_gate_ref,
                    g_cq_ref, wq_ref, g_ckv_ref, wkk_ref, wkv_ref, conv_w_ref, wbb_ref,
                    q_ref, k_ref, v_ref, ga_ref, mixb_ref, carry_ref, *, tiles_per_seq):
    step = pl.program_id(0)
    tm = x_ref.shape[0]
    h = _rms(x_ref[...], g_mix_ref[...]).astype(BF16)

    lane = lax.broadcasted_iota(I32, (tm, HEAD_SLOT), 1)
    rope_lane = (lane >= ROPE_LO) & (lane < ROPE_LO + QK_ROPE_DIM)
    pos_rows = pos_ref[0].astype(F32)
    pos_tile = jnp.concatenate([jnp.broadcast_to(pos_rows[g:g + 1, :], (LANES, LANES))
                                for g in range(tm // LANES)], axis=0)
    row_lane = lax.broadcasted_iota(I32, (tm, LANES), 0) & (LANES - 1)
    pos_col = jnp.sum(jnp.where(lane == row_lane, pos_tile, 0.0), axis=1, keepdims=True)
    ang = pos_col * invf_ref[...]
    first_half = lane < ROPE_LO + ROPE_HALF
    cos_t = jnp.where(rope_lane, jnp.cos(ang), 0.0)
    sin_t = jnp.where(rope_lane, jnp.sin(ang), 0.0)
    sin_t = jnp.where(first_half, -sin_t, sin_t)

    def swap_halves(t):
        return jnp.where(first_half, pltpu.roll(t, HEAD_SLOT - ROPE_HALF, 1),
                         pltpu.roll(t, ROPE_HALF, 1))

    p1 = jnp.dot(h, w1_ref[...], preferred_element_type=F32)
    c_q = p1[:, :COL_CQ]
    c_kv = p1[:, COL_CQ:COL_CKV]
    kr = p1[:, COL_CKV:COL_CKV + HEAD_SLOT]
    k_rot = kr * cos_t + swap_halves(kr) * sin_t

    scale = math.log2(math.e) / math.sqrt(QK_NOPE_DIM + QK_ROPE_DIM)
    q_cos = jnp.where(lane < ROPE_LO, scale, cos_t * scale)
    q_sin = sin_t * scale
    cqn = _rms(c_q, g_cq_ref[...]).astype(BF16)
    qq = jnp.dot(cqn, wq_ref[...], preferred_element_type=F32)
    ckvn = _rms(c_kv, g_ckv_ref[...]).astype(BF16)
    kk = jnp.dot(ckvn, wkk_ref[...], preferred_element_type=F32)
    vv = jnp.dot(ckvn, wkv_ref[...], preferred_element_type=F32)
    ones_lane = jnp.where(lane == L_LANE, 1.0, 0.0)
    for hd in range(MLA_HEADS):
        sl = slice(hd * HEAD_SLOT, (hd + 1) * HEAD_SLOT)
        q_ref[:, sl] = (qq[:, sl] * q_cos + swap_halves(qq[:, sl]) * q_sin).astype(BF16)
        k_ref[:, sl] = (kk[:, sl] + k_rot).astype(BF16)
        v_ref[:, sl] = (vv[:, sl] + ones_lane).astype(BF16)

    p2 = jnp.dot(h, w2_ref[...], preferred_element_type=F32)
    b_gate = p2[:, :CONV_WIDTH]
    u = p2[:, CONV_WIDTH:2 * CONV_WIDTH] * p2[:, 2 * CONV_WIDTH:]

    @pl.when(step % tiles_per_seq == 0)
    def _():
        carry_ref[...] = jnp.zeros_like(carry_ref)

    prev1 = carry_ref[ROW_GROUP - 1:ROW_GROUP, :]
    prev2 = carry_ref[ROW_GROUP - 2:ROW_GROUP - 1, :]
    row = lax.broadcasted_iota(I32, (tm, CONV_WIDTH), 0)
    u1 = jnp.where(row == 0, prev1, pltpu.roll(u, 1, 0))
    u2 = jnp.where(row == 0, prev2, jnp.where(row == 1, prev1, pltpu.roll(u, 2, 0)))
    carry_ref[...] = u[tm - ROW_GROUP:, :]
    cw = conv_w_ref[...]
    conv = b_gate * (cw[0:1, :] * u2 + cw[1:2, :] * u1 + cw[2:3, :] * u)
    branch_b = jnp.dot(conv.astype(BF16), wbb_ref[...], preferred_element_type=F32)

    gates = jax.nn.sigmoid(jnp.dot(h, w3_ref[...], preferred_element_type=F32) + b_gate_ref[...])
    ga_ref[...] = gates[:, :D_MODEL].astype(BF16)
    mixb_ref[...] = (gates[:, D_MODEL:] * branch_b).astype(BF16)


def _in_proj(x2d, pos3d, invf, g_mix, w1, w2, w3, b_gate, g_cq, wq, g_ckv, wkk, wkv,
             conv_w, wbb, seq):
    n = x2d.shape[0]
    row_spec = lambda c: pl.BlockSpec((TM, c), lambda i: (i, 0))
    consts = [invf, g_mix, w1, w2, w3, b_gate, g_cq, wq, g_ckv, wkk, wkv, conv_w, wbb]
    kv_cols = MLA_HEADS * HEAD_SLOT
    return pl.pallas_call(
        functools.partial(_in_proj_kernel, tiles_per_seq=seq // TM),
        grid=(n // TM,),
        in_specs=[row_spec(D_MODEL), pl.BlockSpec((1, TM // LANES, LANES), lambda i: (i, 0, 0))]
        + [_const_spec(c.shape) for c in consts],
        out_specs=[row_spec(kv_cols), row_spec(kv_cols), row_spec(kv_cols),
                   row_spec(D_MODEL), row_spec(D_MODEL)],
        out_shape=[jax.ShapeDtypeStruct((n, kv_cols), BF16),
                   jax.ShapeDtypeStruct((n, kv_cols), BF16),
                   jax.ShapeDtypeStruct((n, kv_cols), BF16),
                   jax.ShapeDtypeStruct((n, D_MODEL), BF16),
                   jax.ShapeDtypeStruct((n, D_MODEL), BF16)],
        scratch_shapes=[pltpu.VMEM((ROW_GROUP, CONV_WIDTH), F32)],
        compiler_params=_params(("arbitrary",)),
        name="in_proj",
    )(x2d, pos3d, *consts)


def _attn_kernel(q_ref, k_ref, v_ref, o_ref):
    qi = pl.program_id(1)
    tq = q_ref.shape[0]

    def block(carry, heads, col_start, ncols, row_lo, visible):
        out = []
        for c, hd in enumerate(heads):
            m, acc = carry[c]
            hsl = slice(hd * HEAD_SLOT, (hd + 1) * HEAD_SLOT)
            kb = k_ref[pl.ds(col_start, ncols), hsl]
            vb = v_ref[pl.ds(col_start, ncols), hsl]
            s = lax.dot_general(q_ref[row_lo:, hsl], kb, (((1,), (1,)), ((), ())),
                                preferred_element_type=F32)
            if visible is not None:
                s = jnp.where(visible, s, NEG_BIG)
            m_hi, acc_hi = m[row_lo:], acc[row_lo:]
            m_new = jnp.maximum(m_hi, jnp.max(s, axis=1, keepdims=True))
            p = jnp.exp2((s - m_new).astype(BF16))
            acc_new = jnp.exp2(m_hi - m_new) * acc_hi + jnp.dot(p, vb, preferred_element_type=F32)
            if row_lo:
                m_new = jnp.concatenate([m[:row_lo], m_new], axis=0)
                acc_new = jnp.concatenate([acc[:row_lo], acc_new], axis=0)
            out.append((m_new, acc_new))
        return tuple(out)

    def full_step(j, carry, heads):
        return block(carry, heads, pl.multiple_of(j * TK, TK), TK, 0, None)

    lane = lax.broadcasted_iota(I32, (tq, HEAD_SLOT), 1)
    strip_masks = []
    for d in range(DIAG_STRIPS):
        shape = (tq - d * (tq // DIAG_STRIPS), TK // DIAG_STRIPS)
        strip_masks.append(lax.broadcasted_iota(I32, shape, 1) <= lax.broadcasted_iota(I32, shape, 0))
    for g in range(MLA_HEADS // ATTN_CHAINS):
        heads = tuple(range(g * ATTN_CHAINS, (g + 1) * ATTN_CHAINS))
        carry = tuple((jnp.full((tq, 1), NEG_BIG, F32), jnp.zeros((tq, HEAD_SLOT), F32))
                      for _ in heads)
        carry = lax.fori_loop(0, qi, functools.partial(full_step, heads=heads), carry)
        for d in range(DIAG_STRIPS):
            width, row_lo = TK // DIAG_STRIPS, d * (tq // DIAG_STRIPS)
            col = pl.multiple_of(qi * TK + d * width, width)
            carry = block(carry, heads, col, width, row_lo, strip_masks[d])
        outs = []
        for _, acc in carry:
            denom = jnp.sum(jnp.where(lane == L_LANE, acc, 0.0), axis=1, keepdims=True)
            outs.append(acc / denom)
        for c in range(0, ATTN_CHAINS, 2):
            hp = heads[c] // 2
            pair = jnp.where(lane < V_HEAD_DIM, outs[c], pltpu.roll(outs[c + 1], V_HEAD_DIM, 1))
            o_ref[:, hp * HEAD_SLOT:(hp + 1) * HEAD_SLOT] = pair.astype(BF16)


def _mla_attn(q, k, v, batch, seq):
    n = q.shape[0]
    nq = seq // TQ
    out_cols = MLA_HEADS * V_HEAD_DIM
    return pl.pallas_call(
        _attn_kernel,
        grid=(batch, nq),
        in_specs=[pl.BlockSpec((TQ, q.shape[1]), lambda b, i: (b * nq + i, 0)),
                  pl.BlockSpec((seq, k.shape[1]), lambda b, i: (b, 0)),
                  pl.BlockSpec((seq, v.shape[1]), lambda b, i: (b, 0))],
        out_specs=pl.BlockSpec((TQ, out_cols), lambda b, i: (b * nq + i, 0)),
        out_shape=jax.ShapeDtypeStruct((n, out_cols), BF16),
        compiler_params=_params(("arbitrary", "arbitrary")),
        name="mla_attn",
    )(q, k, v)


def _post_attn_kernel(x_ref, attn_ref, ga_ref, mixb_ref, wba_ref, wout_ref, g_x_ref, wq_ref,
                      mem_ref, g_mem_ref, wkv_ref, wo_ref, g_moe_ref, wr_ref, br_ref,
                      x2_ref, h3_ref, meta_ref, wts_ref, cnt_ref, run_ref, km_ref, vm_ref,
                      *, tiles_per_seq):
    step = pl.program_id(0)
    tm = x_ref.shape[0]
    xdim = XATTN_HEADS * XATTN_HEAD_DIM

    @pl.when(step == 0)
    def _():
        run_ref[...] = jnp.zeros_like(run_ref)

    @pl.when(step % tiles_per_seq == 0)
    def _():
        mem_n = _rms(mem_ref[...], g_mem_ref[...]).astype(BF16)
        kv = jnp.dot(mem_n, wkv_ref[...], preferred_element_type=F32)
        km_ref[...] = kv[:, :xdim].astype(BF16)
        vm_ref[...] = kv[:, xdim:].astype(BF16)

    def mixer_and_xattn(rows):
        a = jnp.dot(attn_ref[rows, :], wba_ref[...], preferred_element_type=F32)
        mixed = ga_ref[rows, :].astype(F32) * a + mixb_ref[rows, :].astype(F32)
        x1 = x_ref[rows, :] + jnp.dot(mixed.astype(BF16), wout_ref[...], preferred_element_type=F32)

        h2 = _rms(x1, g_x_ref[...]).astype(BF16)
        qm = jnp.dot(h2, wq_ref[...], preferred_element_type=F32) * (1.0 / math.sqrt(XATTN_HEAD_DIM))
        qm = qm.astype(BF16)
        heads = []
        for hd in range(XATTN_HEADS):
            sl = slice(hd * XATTN_HEAD_DIM, (hd + 1) * XATTN_HEAD_DIM)
            s = lax.dot_general(qm[:, sl], km_ref[:, sl], (((1,), (1,)), ((), ())),
                                preferred_element_type=F32)
            p = jnp.exp(s - jnp.max(s, axis=1, keepdims=True))
            o = jnp.dot(p.astype(BF16), vm_ref[:, sl], preferred_element_type=F32)
            heads.append((o / jnp.sum(p, axis=1, keepdims=True)).astype(BF16))
        o_all = jnp.concatenate(heads, axis=1)
        x2 = x1 + jnp.dot(o_all, wo_ref[...], preferred_element_type=F32)
        x2_ref[rows, :] = x2
        h3 = _rms(x2, g_moe_ref[...])
        g0, g1 = rows.start // ROW_GROUP, rows.stop // ROW_GROUP
        h3_ref[g0:g1] = h3.reshape(g1 - g0, ROW_GROUP, D_MODEL)
        return jnp.dot(h3.astype(BF16), wr_ref[...], preferred_element_type=F32) + br_ref[...]

    part = tm // POST_PARTS
    logits = jnp.concatenate([mixer_and_xattn(slice(c * part, (c + 1) * part))
                              for c in range(POST_PARTS)], axis=0)

    lane = lax.broadcasted_iota(I32, (tm, LANES), 1)
    lane_f = lane.astype(F32)

    def first_lane(hit):
        return jnp.min(jnp.where(hit, lane_f, float(LANES)), axis=1, keepdims=True)

    gl = jnp.where(lane < N_GROUPS, logits, -jnp.inf)
    gmax = jnp.max(gl, axis=1, keepdims=True)
    g_idx = first_lane(gl == gmax).astype(I32)
    p_group = 1.0 / jnp.sum(jnp.exp(gl - gmax), axis=1, keepdims=True)
    e_lo = ROUTE_LANE0 + g_idx * EXPERTS_PER_GROUP
    el = jnp.where((lane >= e_lo) & (lane < e_lo + EXPERTS_PER_GROUP), logits, -jnp.inf)
    v1 = jnp.max(el, axis=1, keepdims=True)
    i1 = first_lane(el == v1)
    el2 = jnp.where(lane_f == i1, -jnp.inf, el)
    v2 = jnp.max(el2, axis=1, keepdims=True)
    i2 = first_lane(el2 == v2)
    t = jnp.exp(v2 - v1)
    w0 = p_group / (1.0 + t)
    w1 = p_group * t / (1.0 + t)

    oh0 = lane_f == i1
    oh1 = lane_f == i2
    oh = (oh0 | oh1).astype(BF16)
    r_i = lax.broadcasted_iota(I32, (tm, tm), 0)
    c_i = lax.broadcasted_iota(I32, (tm, tm), 1)
    lower = (c_i < r_i).astype(BF16)
    before = jnp.dot(lower, oh, preferred_element_type=F32) + run_ref[...]
    rank0 = jnp.sum(jnp.where(oh0, before, 0.0), axis=1, keepdims=True)
    rank1 = jnp.sum(jnp.where(oh1, before, 0.0), axis=1, keepdims=True)
    run_ref[...] = run_ref[...] + jnp.sum(oh.astype(F32), axis=0, keepdims=True)
    cnt_ref[...] = jnp.broadcast_to(run_ref[...], cnt_ref.shape).astype(I32)

    meta = jnp.where(lane == 0, i1 - ROUTE_LANE0, 0.0)
    meta = jnp.where(lane == 1, i2 - ROUTE_LANE0, meta)
    meta = jnp.where(lane == 2, rank0, meta)
    meta = jnp.where(lane == 3, rank1, meta)
    meta_ref[...] = meta.T[:META_ROWS, :]
    wts_ref[...] = jnp.where(lane == 0, w0, jnp.where(lane == 1, w1, 0.0))


def _post_attn(x2d, attn, ga, mixb, wba, wout, g_x, wq, mem2d, g_mem, wkv, wo, g_moe, wr, br, seq):
    n = x2d.shape[0]
    tps = seq // TM
    xdim = XATTN_HEADS * XATTN_HEAD_DIM
    row_spec = lambda c: pl.BlockSpec((TM, c), lambda i: (i, 0))
    consts = lambda *arrs: [_const_spec(a.shape) for a in arrs]
    return pl.pallas_call(
        functools.partial(_post_attn_kernel, tiles_per_seq=tps),
        grid=(n // TM,),
        in_specs=([row_spec(D_MODEL), row_spec(MLA_HEADS * V_HEAD_DIM), row_spec(D_MODEL),
                   row_spec(D_MODEL)] + consts(wba, wout, g_x, wq)
                  + [pl.BlockSpec((MEM_LEN, D_MODEL), lambda i: (i // tps, 0))]
                  + consts(g_mem, wkv, wo, g_moe, wr, br)),
        out_specs=[row_spec(D_MODEL),
                   pl.BlockSpec((TM // ROW_GROUP, ROW_GROUP, D_MODEL), lambda i: (i, 0, 0)),
                   pl.BlockSpec((META_ROWS, TM), lambda i: (0, i)), row_spec(LANES),
                   _const_spec((ROW_GROUP, LANES))],
        out_shape=[jax.ShapeDtypeStruct((n, D_MODEL), F32),
                   jax.ShapeDtypeStruct((n // ROW_GROUP, ROW_GROUP, D_MODEL), F32),
                   jax.ShapeDtypeStruct((META_ROWS, n), F32),
                   jax.ShapeDtypeStruct((n, LANES), F32),
                   jax.ShapeDtypeStruct((ROW_GROUP, LANES), I32)],
        scratch_shapes=[pltpu.VMEM((1, LANES), F32), pltpu.VMEM((MEM_LEN, xdim), BF16),
                        pltpu.VMEM((MEM_LEN, xdim), BF16)],
        compiler_params=_params(("arbitrary",)),
        name="post_attn",
    )(x2d, attn, ga, mixb, wba, wout, g_x, wq, mem2d, g_mem, wkv, wo, g_moe, wr, br)


def _tile_row(ref, group, k):
    return ref.at[group, pl.ds(k, 1), :]


def _hbm_row(ref, row):
    return ref.at[pl.ds(row, 1), :]


def _dispatch_kernel(fill_ref, dest0_ref, dest1_ref, zeros_ref, h3_ref, xd_ref, row_sem, fill_sem):
    i = pl.program_id(0)

    def fill_copy(b):
        return pltpu.make_async_copy(
            zeros_ref, xd_ref.at[pl.ds(pl.multiple_of(b * SLOT_BLOCK, SLOT_BLOCK), SLOT_BLOCK)],
            fill_sem)

    @pl.when(i == 0)
    def _():
        def start(b, c):
            @pl.when(fill_ref[b] > 0)
            def _():
                fill_copy(b).start()
            return c

        def wait(b, c):
            @pl.when(fill_ref[b] > 0)
            def _():
                fill_copy(b).wait()
            return c

        lax.fori_loop(0, fill_ref.shape[0], start, 0)
        lax.fori_loop(0, fill_ref.shape[0], wait, 0)

    def copies(g, k):
        r = g * ROW_GROUP + k
        src = _tile_row(h3_ref, g, k)
        return (pltpu.make_async_copy(src, _hbm_row(xd_ref, dest0_ref[0, 0, r]), row_sem),
                pltpu.make_async_copy(src, _hbm_row(xd_ref, dest1_ref[0, 0, r]), row_sem))

    def issue(g, c):
        for k in range(ROW_GROUP):
            for j, cp in enumerate(copies(g, k)):
                cp.start(priority=j % DMA_THREADS)
        return c

    def drain(g, c):
        for k in range(ROW_GROUP):
            for cp in copies(g, k):
                cp.wait()
        return c

    lax.fori_loop(0, TROW_D // ROW_GROUP, issue, 0)
    lax.fori_loop(0, TROW_D // ROW_GROUP, drain, 0)


def _dispatch(fill_blk, dest, h3, n_slots):
    groups, _, d = h3.shape
    n = groups * ROW_GROUP
    dest0, dest1 = (dest[j].reshape(n // TROW_D, 1, TROW_D) for j in range(2))
    grid_spec = pltpu.PrefetchScalarGridSpec(
        num_scalar_prefetch=1,
        grid=(n // TROW_D,),
        in_specs=[pl.BlockSpec((1, 1, TROW_D), lambda i, fb: (i, 0, 0), memory_space=pltpu.SMEM),
                  pl.BlockSpec((1, 1, TROW_D), lambda i, fb: (i, 0, 0), memory_space=pltpu.SMEM),
                  pl.BlockSpec((SLOT_BLOCK, d), lambda i, fb: (0, 0)),
                  pl.BlockSpec((TROW_D // ROW_GROUP, ROW_GROUP, d), lambda i, fb: (i, 0, 0))],
        out_specs=pl.BlockSpec(memory_space=pl.ANY),
        scratch_shapes=[pltpu.SemaphoreType.DMA, pltpu.SemaphoreType.DMA],
    )
    return pl.pallas_call(
        _dispatch_kernel,
        grid_spec=grid_spec,
        out_shape=jax.ShapeDtypeStruct((n_slots, d), F32),
        compiler_params=_params(("arbitrary",)),
        name="dispatch",
    )(fill_blk, dest0, dest1, jnp.zeros((SLOT_BLOCK, d), F32), h3)


def _experts_kernel(be_ref, nb_ref, nxt_ref, xd_ref, wgu_hbm, wdn_hbm, yd_ref,
                    wgu_f32, wdn_f32, wgu_bf, wdn_bf, slot_ref, sem):
    i = pl.program_id(0)

    def weight_copies(e, slot):
        return (pltpu.make_async_copy(wgu_hbm.at[e], wgu_f32.at[slot], sem.at[slot]),
                pltpu.make_async_copy(wdn_hbm.at[e], wdn_f32.at[slot], sem.at[slot]))

    @pl.when(i == 0)
    def _():
        slot_ref[0] = 0
        for cp in weight_copies(be_ref[0], 0):
            cp.start()

    @pl.when(i < nb_ref[0])
    def _():
        @pl.when((i == 0) | (be_ref[i] != be_ref[jnp.maximum(i - 1, 0)]))
        def _():
            slot = slot_ref[0]
            for cp in weight_copies(be_ref[i], slot):
                cp.wait()
            wgu_bf[...] = wgu_f32[slot].astype(BF16)
            wdn_bf[...] = wdn_f32[slot].astype(BF16)

            @pl.when(nxt_ref[i] >= 0)
            def _():
                for cp in weight_copies(nxt_ref[i], 1 - slot):
                    cp.start()

            slot_ref[0] = 1 - slot

        gu = jnp.dot(xd_ref[...].astype(BF16), wgu_bf[...], preferred_element_type=F32)
        gate = gu[:, :EXPERT_FF]
        up = gu[:, EXPERT_FF:]
        act = (gate * jax.nn.sigmoid(gate) * up).astype(BF16)
        yd_ref[...] = jnp.dot(act, wdn_bf[...], preferred_element_type=F32)


def _experts(block_expert, n_blocks_used, next_expert, xd, w_gate_up, w_down):
    n_slots, d = xd.shape
    slot_map = lambda i, be, nb, nx: (jnp.minimum(i, nb[0] - 1), 0)
    grid_spec = pltpu.PrefetchScalarGridSpec(
        num_scalar_prefetch=3,
        grid=(n_slots // SLOT_BLOCK,),
        in_specs=[pl.BlockSpec((SLOT_BLOCK, d), slot_map),
                  pl.BlockSpec(memory_space=pl.ANY),
                  pl.BlockSpec(memory_space=pl.ANY)],
        out_specs=pl.BlockSpec((SLOT_BLOCK, d), slot_map),
        scratch_shapes=[pltpu.VMEM((2, D_MODEL, 2 * EXPERT_FF), F32),
                        pltpu.VMEM((2, EXPERT_FF, D_MODEL), F32),
                        pltpu.VMEM((D_MODEL, 2 * EXPERT_FF), BF16),
                        pltpu.VMEM((EXPERT_FF, D_MODEL), BF16),
                        pltpu.SMEM((1,), I32),
                        pltpu.SemaphoreType.DMA((2,))],
    )
    return pl.pallas_call(
        _experts_kernel,
        grid_spec=grid_spec,
        out_shape=jax.ShapeDtypeStruct(xd.shape, F32),
        input_output_aliases={3: 0},
        compiler_params=_params(("arbitrary",)),
        name="experts",
    )(block_expert, n_blocks_used, next_expert, xd, w_gate_up, w_down)


def _combine_kernel(dest0_ref, dest1_ref, next0_ref, next1_ref, x2_ref, wts_ref, g_ref, yd_ref, o_ref,
                    buf, sem):
    i = pl.program_id(0)
    last = pl.num_programs(0) - 1
    rows = x2_ref.shape[0]
    slot = i % 2

    def copies(d_refs, slt, g, k):
        r = g * ROW_GROUP + k
        return tuple(pltpu.make_async_copy(_hbm_row(yd_ref, d_ref[0, 0, r]),
                                           _tile_row(buf.at[slt, j], g, k), sem.at[slt])
                     for j, d_ref in enumerate(d_refs))

    def issue(d_ref, slt):
        def body(g, c):
            for k in range(ROW_GROUP):
                for j, cp in enumerate(copies(d_ref, slt, g, k)):
                    cp.start(priority=j % DMA_THREADS)
            return c
        lax.fori_loop(0, rows // ROW_GROUP, body, 0)

    @pl.when(i == 0)
    def _():
        issue((dest0_ref, dest1_ref), 0)

    @pl.when(i < last)
    def _():
        issue((next0_ref, next1_ref), 1 - slot)

    def drain(g, c):
        for k in range(ROW_GROUP):
            for cp in copies((dest0_ref, dest1_ref), slot, g, k):
                cp.wait()
        return c

    lax.fori_loop(0, rows // ROW_GROUP, drain, 0)

    lane = lax.broadcasted_iota(I32, (rows, LANES), 1)
    wts = wts_ref[...]
    w0 = jnp.sum(jnp.where(lane == 0, wts, 0.0), axis=1, keepdims=True)
    w1 = jnp.sum(jnp.where(lane == 1, wts, 0.0), axis=1, keepdims=True)
    y0 = buf[slot, 0].reshape(rows, D_MODEL)
    y1 = buf[slot, 1].reshape(rows, D_MODEL)
    o_ref[...] = _rms(x2_ref[...] + y0 * w0 + y1 * w1, g_ref[...])


def _combine(dest, x2, wts, g_final, yd):
    n = x2.shape[0]
    dest0, dest1 = (dest[j].reshape(n // TROW, 1, TROW) for j in range(2))
    steps = n // TROW
    dest_spec = lambda f: pl.BlockSpec((1, 1, TROW), f, memory_space=pltpu.SMEM)
    cur = lambda i: (i, 0, 0)
    nxt = lambda i: (jnp.minimum(i + 1, steps - 1), 0, 0)
    return pl.pallas_call(
        _combine_kernel,
        grid=(steps,),
        in_specs=[dest_spec(cur), dest_spec(cur), dest_spec(nxt), dest_spec(nxt),
                  pl.BlockSpec((TROW, D_MODEL), lambda i: (i, 0)),
                  pl.BlockSpec((TROW, LANES), lambda i: (i, 0)),
                  _const_spec((1, D_MODEL)),
                  pl.BlockSpec(memory_space=pl.ANY)],
        out_specs=pl.BlockSpec((TROW, D_MODEL), lambda i: (i, 0)),
        out_shape=jax.ShapeDtypeStruct((n, D_MODEL), F32),
        scratch_shapes=[pltpu.VMEM((2, 2, TROW // ROW_GROUP, ROW_GROUP, D_MODEL), F32),
                        pltpu.SemaphoreType.DMA((2,))],
        compiler_params=_params(("arbitrary",)),
        name="combine",
    )(dest0, dest1, dest0, dest1, x2, wts, g_final, yd)


def _lanes(parts, width):
    ref = next(p for p in parts if not isinstance(p, int))
    cols = [jnp.zeros(ref.shape[:-1] + (p,), ref.dtype) if isinstance(p, int) else p for p in parts]
    used = sum(c.shape[-1] for c in cols)
    assert used <= width
    if used < width:
        cols.append(jnp.zeros(ref.shape[:-1] + (width - used,), ref.dtype))
    return jnp.concatenate(cols, axis=-1)


def _head_slots(w, per_head, pieces):
    rows = w.shape[0]
    wh = w.reshape(rows, MLA_HEADS, per_head)
    return _lanes(pieces(wh), HEAD_SLOT).reshape(rows, MLA_HEADS * HEAD_SLOT)


def _layer(x2d, pos3d, invf, mem2d, g_mem, w_kv_mem, batch, seq, g_mix, w_in, b_gate, g_cq, w_uq, g_ckv, w_ukv,
           conv_w, w_branch_a, w_branch_b, w_out, g_xattn, w_q_mem, w_o_mem, g_moe,
           w_router_group, b_router_group, w_router_expert, b_router_expert, w_gate_up, w_down,
           g_final):
    n = x2d.shape[0]
    row = lambda v: v.reshape(1, -1)

    w_kr = w_in[:, COL_CKV:COL_KROPE]
    w1 = _lanes([w_in[:, :COL_CKV], ROPE_LO, w_kr], COL_CKV + HEAD_SLOT).astype(BF16)
    w2 = w_in[:, COL_KROPE:COL_CONV].astype(BF16)
    w3 = w_in[:, COL_CONV:].astype(BF16)
    qd = QK_NOPE_DIM + QK_ROPE_DIM
    wq = _head_slots(w_uq, qd, lambda wh: [wh]).astype(BF16)
    kvd = QK_NOPE_DIM + V_HEAD_DIM
    wkk = _head_slots(w_ukv, kvd, lambda wh: [wh[..., :QK_NOPE_DIM]]).astype(BF16)
    wkv = _head_slots(w_ukv, kvd, lambda wh: [wh[..., QK_NOPE_DIM:]]).astype(BF16)
    conv_w8 = jnp.concatenate(
        [conv_w, jnp.zeros((ROW_GROUP - conv_w.shape[0], CONV_WIDTH), F32)], axis=0)

    q, k, v, ga, mixb = _in_proj(x2d, pos3d, invf, row(g_mix), w1, w2, w3, row(b_gate), row(g_cq),
                                 wq, row(g_ckv), wkk, wkv, conv_w8,
                                 w_branch_b.astype(BF16), seq)
    attn = _mla_attn(q, k, v, batch, seq)

    wr = _lanes([w_router_group, w_router_expert], LANES).astype(BF16)
    br = _lanes([row(b_router_group), row(b_router_expert)], LANES)
    x2, h3, meta, wts, cnt = _post_attn(x2d, attn, ga, mixb, w_branch_a.astype(BF16),
                                   w_out.astype(BF16), row(g_xattn), w_q_mem.astype(BF16), mem2d, row(g_mem),
                                   w_kv_mem.astype(BF16),
                                   w_o_mem.astype(BF16), row(g_moe), wr, br, seq)

    counts = cnt[0, ROUTE_LANE0:ROUTE_LANE0 + N_EXPERTS]
    padded = (counts + SLOT_BLOCK - 1) // SLOT_BLOCK * SLOT_BLOCK
    pad_end = jnp.cumsum(padded)
    pad_start = pad_end - padded
    n_slots = (2 * n + SLOT_BLOCK - 1) // SLOT_BLOCK * SLOT_BLOCK + N_EXPERTS * SLOT_BLOCK
    routed = meta[0:4].astype(I32)
    expert_col = jnp.arange(N_EXPERTS, dtype=I32)[:, None, None]
    dest = jnp.sum(jnp.where(routed[None, 0:2] == expert_col, pad_start[:, None, None], 0),
                   axis=0) + routed[2:4]
    block_lo = jnp.arange(n_slots // SLOT_BLOCK, dtype=I32) * SLOT_BLOCK
    block_expert = jnp.minimum(jnp.sum(pad_end[None, :] <= block_lo[:, None], axis=1),
                               N_EXPERTS - 1).astype(I32)
    n_blocks_used = (pad_end[-1:] // SLOT_BLOCK).astype(I32)
    fill_blk = ((block_lo + SLOT_BLOCK == pad_end[block_expert])
                | (block_lo >= pad_end[-1])).astype(I32)

    cand = jnp.where(padded > 0, jnp.arange(N_EXPERTS, dtype=I32), N_EXPERTS)
    later = jnp.concatenate([lax.cummin(cand, reverse=True)[1:],
                             jnp.full((1,), N_EXPERTS, I32)])
    next_expert = jnp.where(later < N_EXPERTS, later, -1)[block_expert].astype(I32)

    xd = _dispatch(fill_blk, dest, h3, n_slots)
    yd = _experts(block_expert, n_blocks_used, next_expert, xd, w_gate_up, w_down)
    return _combine(dest, x2, wts, row(g_final), yd)


def kernel(x, mem, positions, g_mix, w_in, b_gate, g_cq, w_uq, g_ckv, w_ukv, conv_w, w_branch_a,
           w_branch_b, w_out, g_xattn, g_mem, w_q_mem, w_kv_mem, w_o_mem, g_moe, w_router_group,
           b_router_group, w_router_expert, b_router_expert, w_gate_up, w_down, g_final):
    batch, seq, d = x.shape
    depth = g_mix.shape[0]
    assert d == D_MODEL and depth == 1 and seq % TM == 0 and seq % TQ == 0 and TQ == TK
    n = batch * seq
    x2d = x.reshape(n, d)
    pos3d = positions.astype(I32).reshape(n // TM, TM // LANES, LANES)
    inv_freq = ROPE_THETA ** (-jnp.arange(0, QK_ROPE_DIM, 2, dtype=F32) / QK_ROPE_DIM)
    invf = _lanes([ROPE_LO, inv_freq.reshape(1, -1), inv_freq.reshape(1, -1)], HEAD_SLOT)
    l = 0
    out = _layer(x2d, pos3d, invf, mem.reshape(batch * MEM_LEN, d), g_mem[l], w_kv_mem[l], batch,
                 seq, g_mix[l], w_in[l], b_gate[l], g_cq[l],
                 w_uq[l], g_ckv[l], w_ukv[l], conv_w[l], w_branch_a[l], w_branch_b[l], w_out[l],
                 g_xattn[l], w_q_mem[l], w_o_mem[l], g_moe[l], w_router_group[l],
                 b_router_group[l], w_router_expert[l], b_router_expert[l], w_gate_up[l],
                 w_down[l], g_final)
    return out.reshape(batch, seq, d)
```
